```python
import math
import jax, jax.numpy as jnp
from jax import lax
import numpy as np

D_MODEL = 1024
BATCH = 2
SEQ = 8192
DEPTH = 4

N_MIXERS = 3
EPS = 1e-6
D_FF = 4 * D_MODEL

MLSTM_HEADS = 4
MLSTM_DV = D_MODEL // MLSTM_HEADS
MLSTM_DQK = MLSTM_DV // 2
MLSTM_CHUNK = 128
GATE_SOFTCAP = 15.0
MLSTM_QK = MLSTM_HEADS * MLSTM_DQK
MLSTM_IN = 2 * MLSTM_QK + 2 * D_MODEL + 2 * MLSTM_HEADS

DIFF_HEADS = 8
DIFF_DH = D_MODEL // (2 * DIFF_HEADS)
DIFF_DV = 2 * DIFF_DH
Q_BLOCK = 128
ROPE_THETA = 10000.0

SSD_INNER = 2 * D_MODEL
SSD_HEADDIM = 64
SSD_HEADS = SSD_INNER // SSD_HEADDIM
SSD_GROUPS = 8
SSD_STATE = 128
SSD_CONV = 4
SSD_CHUNK = 128
SSD_BC = SSD_GROUPS * SSD_STATE
SSD_CONV_DIM = SSD_INNER + 2 * SSD_BC
SSD_IN = SSD_INNER + SSD_CONV_DIM + SSD_HEADS

kernel_name = 'hybrid_mlstm_diffattn_ssd_trunk'


def rmsnorm(x, g):
    xf = x.astype(jnp.float32)
    y = xf * lax.rsqrt(jnp.mean(xf * xf, axis=-1, keepdims=True) + EPS)
    return (y * g.astype(jnp.float32)).astype(x.dtype)


def softcap(x, cap):
    return cap * jnp.tanh(x / cap)


def rope_tables(seq, dim):
    inv = ROPE_THETA ** (-jnp.arange(0, dim, 2, dtype=jnp.float32) / dim)
    ang = jnp.arange(seq, dtype=jnp.float32)[:, None] * inv[None, :]
    return jnp.cos(ang), jnp.sin(ang)


def apply_rope(x, cos, sin):
    half = x.shape[-1] // 2
    shp = (cos.shape[0],) + (1,) * (x.ndim - 3) + (half,)
    c = cos.reshape(shp).astype(x.dtype)
    s = sin.reshape(shp).astype(x.dtype)
    x1, x2 = x[..., :half], x[..., half:]
    return jnp.concatenate([x1 * c - x2 * s, x2 * c + x1 * s], axis=-1)


def sq_relu_mlp(u, w_up, w_down):
    return jnp.square(jax.nn.relu(u @ w_up)) @ w_down


def mlstm_chunkwise(q, k, v, log_i, log_f):
    bsz, seq, nh, dqk = q.shape
    dv = v.shape[-1]
    L = MLSTM_CHUNK
    nc = seq // L

    def to_chunks(t):
        t = t.reshape((bsz, nc, L, nh) + t.shape[3:])
        return jnp.moveaxis(t, 3, 2)

    q, k, v = to_chunks(q), to_chunks(k), to_chunks(v)
    li, lf = to_chunks(log_i), to_chunks(log_f)
    b = jnp.cumsum(lf, axis=-1)
    b_last = b[..., -1]
    causal = jnp.tril(jnp.ones((L, L), dtype=bool))
    d_log = jnp.where(causal, b[..., :, None] - b[..., None, :] + li[..., None, :], -jnp.inf)
    w_log = b_last[..., None] - b + li
    m_loc = jnp.max(w_log, axis=-1)
    w = jnp.exp(w_log - m_loc[..., None])
    c_loc = jnp.einsum('bchs,bchsk,bchsv->bchkv', w, k, v)
    n_loc = jnp.einsum('bchs,bchsk->bchk', w, k)

    def step(carry, inp):
        c_st, n_st, m_st = carry
        bl, ml, cl, nl = inp
        m_new = jnp.maximum(bl + m_st, ml)
        a_old = jnp.exp(bl + m_st - m_new)
        a_loc = jnp.exp(ml - m_new)
        c_new = a_old[..., None, None] * c_st + a_loc[..., None, None] * cl
        n_new = a_old[..., None] * n_st + a_loc[..., None] * nl
        return (c_new, n_new, m_new), (c_st, n_st, m_st)

    init = (jnp.zeros((bsz, nh, dqk, dv), jnp.float32),
            jnp.zeros((bsz, nh, dqk), jnp.float32),
            jnp.zeros((bsz, nh), jnp.float32))
    xs = (jnp.moveaxis(b_last, 1, 0), jnp.moveaxis(m_loc, 1, 0),
          jnp.moveaxis(c_loc, 1, 0), jnp.moveaxis(n_loc, 1, 0))
    _, (c_in, n_in, m_in) = lax.scan(step, init, xs)
    c_in = jnp.moveaxis(c_in, 0, 1)
    n_in = jnp.moveaxis(n_in, 0, 1)
    m_in = jnp.moveaxis(m_in, 0, 1)

    a_inter = b + m_in[..., None]
    m_t = jnp.maximum(a_inter, jnp.max(d_log, axis=-1))
    p = jnp.exp(d_log - m_t[..., None]) * jnp.einsum('bchld,bchsd->bchls', q, k)
    scale = jnp.exp(a_inter - m_t)
    num = jnp.einsum('bchls,bchsv->bchlv', p, v) + scale[..., None] * jnp.einsum('bchld,bchdv->bchlv', q, c_in)
    den = jnp.sum(p, axis=-1) + scale * jnp.einsum('bchld,bchd->bchl', q, n_in)
    h = num / jnp.maximum(jnp.abs(den), jnp.exp(-m_t))[..., None]
    return jnp.moveaxis(h, 2, 3).reshape(bsz, seq, nh, dv)


def mlstm_mixer(u, w_in, b_gate, norm_g, w_out):
    bsz, seq, _ = u.shape
    nh = MLSTM_HEADS
    f32 = jnp.float32
    splits = [MLSTM_QK, 2 * MLSTM_QK, 2 * MLSTM_QK + D_MODEL,
              2 * MLSTM_QK + 2 * D_MODEL, 2 * MLSTM_QK + 2 * D_MODEL + nh]
    q, k, v, o, ig, fg = jnp.split(u @ w_in, splits, axis=-1)
    bg = b_gate.astype(f32)
    log_i = softcap(ig.astype(f32) + bg[:nh], GATE_SOFTCAP)
    log_f = jax.nn.log_sigmoid(softcap(fg.astype(f32) + bg[nh:], GATE_SOFTCAP))
    hs = mlstm_chunkwise(
        q.reshape(bsz, seq, nh, MLSTM_DQK).astype(f32) * (MLSTM_DQK ** -0.5),
        k.reshape(bsz, seq, nh, MLSTM_DQK).astype(f32),
        v.reshape(bsz, seq, nh, MLSTM_DV).astype(f32),
        log_i, log_f)
    hs = rmsnorm(hs, norm_g.reshape(nh, MLSTM_DV)).reshape(bsz, seq, D_MODEL)
    out = (jax.nn.sigmoid(o.astype(f32)) * hs).astype(u.dtype)
    return out @ w_out


def diff_attention_mixer(u, w_in, lam, norm_g, w_out, layer_idx):
    bsz, seq, _ = u.shape
    nb = seq // Q_BLOCK
    q, k, v = jnp.split(u @ w_in, 3, axis=-1)
    cos, sin = rope_tables(seq, DIFF_DH)
    q = apply_rope(q.reshape(bsz, seq, DIFF_HEADS, 2, DIFF_DH), cos, sin)
    k = apply_rope(k.reshape(bsz, seq, DIFF_HEADS, 2, DIFF_DH), cos, sin)
    v = v.reshape(bsz, seq, DIFF_HEADS, DIFF_DV).transpose(0, 2, 1, 3)
    k = k.transpose(0, 2, 3, 1, 4)
    q_blocks = q.reshape(bsz, nb, Q_BLOCK, DIFF_HEADS, 2, DIFF_DH).transpose(1, 0, 3, 4, 2, 5)
    lam_init = 0.8 - 0.6 * math.exp(-0.3 * layer_idx)
    lf = lam.astype(jnp.float32)
    lam_val = jnp.exp(jnp.sum(lf[0] * lf[1])) - jnp.exp(jnp.sum(lf[2] * lf[3])) + lam_init
    key_pos = jnp.arange(seq)
    scale = DIFF_DH ** -0.5

    def block(args):
        qb, start = args
        s = jnp.einsum('bhcqd,bhckd->bhcqk', qb, k).astype(jnp.float32) * scale
        mask = key_pos[None, :] <= (start + jnp.arange(Q_BLOCK))[:, None]
        p = jax.nn.softmax(jnp.where(mask, s, -jnp.inf), axis=-1)
        a = p[:, :, 0] - lam_val * p[:, :, 1]
        return jnp.einsum('bhqk,bhkv->bhqv', a.astype(v.dtype), v)

    o = lax.map(block, (q_blocks, jnp.arange(nb) * Q_BLOCK))
    o = o.transpose(1, 0, 3, 2, 4).reshape(bsz, seq, DIFF_HEADS, DIFF_DV)
    o = rmsnorm(o, norm_g) * (1.0 - lam_init)
    return o.reshape(bsz, seq, D_MODEL) @ w_out


def causal_depthwise_conv(x, w, b):
    ch = x.shape[-1]
    y = lax.conv_general_dilated(
        x, w.reshape(SSD_CONV, 1, ch).astype(x.dtype), window_strides=(1,),
        padding=[(SSD_CONV - 1, 0)], dimension_numbers=('NWC', 'WIO', 'NWC'),
        feature_group_count=ch)
    return y + b.astype(x.dtype)


def ssd_chunked(x, dt, A, Bm, Cm):
    bsz, seq, nh, hp = x.shape
    ng, ns = Bm.shape[2], Bm.shape[3]
    r = nh // ng
    L = SSD_CHUNK
    nc = seq // L
    x = x.reshape(bsz, nc, L, ng, r, hp)
    dt = dt.reshape(bsz, nc, L, ng, r)
    Bm = Bm.reshape(bsz, nc, L, ng, ns)
    Cm = Cm.reshape(bsz, nc, L, ng, ns)
    cum = jnp.cumsum(dt * A.reshape(ng, r), axis=2)
    cum_t = jnp.moveaxis(cum, 2, -1)
    causal = jnp.tril(jnp.ones((L, L), dtype=bool))
    decay = jnp.exp(jnp.where(causal, cum_t[..., :, None] - cum_t[..., None, :], -jnp.inf))
    xdt = x * dt[..., None]
    cb = jnp.einsum('bclgn,bcsgn->bcgls', Cm, Bm)
    y_diag = jnp.einsum('bcgls,bcgrls,bcsgrp->bclgrp', cb, decay, xdt)
    decay_to_end = jnp.exp(cum[:, :, -1:] - cum)
    states = jnp.einsum('bclgn,bclgr,bclgrp->bcgrpn', Bm, decay_to_end, xdt)
    chunk_decay = jnp.exp(cum[:, :, -1])

    def step(s, inp):
        st, dec = inp
        return dec[..., None, None] * s + st, s

    init = jnp.zeros((bsz, ng, r, hp, ns), jnp.float32)
    _, s_in = lax.scan(step, init, (jnp.moveaxis(states, 1, 0), jnp.moveaxis(chunk_decay, 1, 0)))
    s_in = jnp.moveaxis(s_in, 0, 1)
    y_off = jnp.einsum('bclgn,bcgrpn,bclgr->bclgrp', Cm, s_in, jnp.exp(cum))
    return (y_diag + y_off).reshape(bsz, seq, nh, hp)


def ssd_mixer(u, w_in, conv_w, conv_b, dt_bias, A_log, D_skip, norm_g, w_out):
    bsz, seq, _ = u.shape
    f32 = jnp.float32
    z, xbc, dt = jnp.split(u @ w_in, [SSD_INNER, SSD_INNER + SSD_CONV_DIM], axis=-1)
    xbc = jax.nn.silu(causal_depthwise_conv(xbc, conv_w, conv_b))
    xs, Bm, Cm = jnp.split(xbc, [SSD_INNER, SSD_INNER + SSD_BC], axis=-1)
    xs = xs.reshape(bsz, seq, SSD_HEADS, SSD_HEADDIM).astype(f32)
    Bm = Bm.reshape(bsz, seq, SSD_GROUPS, SSD_STATE).astype(f32)
    Cm = Cm.reshape(bsz, seq, SSD_GROUPS, SSD_STATE).astype(f32)
    dt = jax.nn.softplus(dt.astype(f32) + dt_bias.astype(f32))
    A = -jnp.exp(A_log.astype(f32))
    y = ssd_chunked(xs, dt, A, Bm, Cm) + D_skip.astype(f32)[:, None] * xs
    y = y.reshape(bsz, seq, SSD_INNER).astype(u.dtype) * jax.nn.silu(z)
    y = rmsnorm(y.reshape(bsz, seq, SSD_GROUPS, SSD_INNER // SSD_GROUPS),
                norm_g.reshape(SSD_GROUPS, SSD_INNER // SSD_GROUPS)).reshape(bsz, seq, SSD_INNER)
    return y @ w_out


def setup_inputs(seed: int = 0) -> dict:
    key = jax.random.key(seed)
    ks = list(jax.random.split(key, 24))
    f32 = jnp.float32
    n_a, n_b, n_c = (len(range(m, DEPTH, N_MIXERS)) for m in range(N_MIXERS))
    res_scale = (2 * DEPTH) ** -0.5

    def dense(k, shape, fan_in, scale=1.0):
        return jax.random.normal(k, shape, f32) * (scale * fan_in ** -0.5)

    def gain(k, shape):
        return 1.0 + 0.02 * jax.random.normal(k, shape, f32)

    x = jax.random.normal(ks[0], (BATCH, SEQ, D_MODEL), f32)
    ln_mix = gain(ks[1], (DEPTH, D_MODEL))
    ln_mlp = gain(ks[2], (DEPTH, D_MODEL))
    w_up = dense(ks[3], (DEPTH, D_MODEL, D_FF), D_MODEL)
    w_down = dense(ks[4], (DEPTH, D_FF, D_MODEL), D_FF, res_scale)
    ln_f = gain(ks[5], (D_MODEL,))
    mlstm_w_in = dense(ks[6], (n_a, D_MODEL, MLSTM_IN), D_MODEL)
    mlstm_b_gate = jnp.concatenate([
        0.1 * jax.random.normal(ks[7], (n_a, MLSTM_HEADS), f32),
        jax.random.uniform(ks[8], (n_a, MLSTM_HEADS), f32, 3.0, 6.0)], axis=-1)
    mlstm_norm = gain(ks[9], (n_a, D_MODEL))
    mlstm_w_out = dense(ks[10], (n_a, D_MODEL, D_MODEL), D_MODEL, res_scale)
    diff_w_in = dense(ks[11], (n_b, D_MODEL, 3 * D_MODEL), D_MODEL)
    diff_lam = 0.1 * jax.random.normal(ks[12], (n_b, 4, DIFF_DH), f32)
    diff_norm = gain(ks[13], (n_b, DIFF_DV))
    diff_w_out = dense(ks[14], (n_b, D_MODEL, D_MODEL), D_MODEL, res_scale)
    ssd_w_in = dense(ks[15], (n_c, D_MODEL, SSD_IN), D_MODEL)
    ssd_conv_w = dense(ks[16], (n_c, SSD_CONV, SSD_CONV_DIM), SSD_CONV)
    ssd_conv_b = 0.02 * jax.random.normal(ks[17], (n_c, SSD_CONV_DIM), f32)
    dt0 = jnp.exp(jax.random.uniform(ks[18], (n_c, SSD_HEADS), f32, math.log(1e-3), math.log(1e-1)))
    ssd_dt_bias = dt0 + jnp.log(-jnp.expm1(-dt0))
    ssd_A_log = jnp.log(jax.random.uniform(ks[19], (n_c, SSD_HEADS), f32, 1.0, 16.0))
    ssd_D = 1.0 + 0.1 * jax.random.normal(ks[20], (n_c, SSD_HEADS), f32)
    ssd_norm = gain(ks[21], (n_c, SSD_INNER))
    ssd_w_out = dense(ks[22], (n_c, SSD_INNER, D_MODEL), SSD_INNER, res_scale)
    return {'x': x, 'ln_mix': ln_mix, 'ln_mlp': ln_mlp, 'w_up': w_up, 'w_down': w_down, 'ln_f': ln_f,
            'mlstm_w_in': mlstm_w_in, 'mlstm_b_gate': mlstm_b_gate, 'mlstm_norm': mlstm_norm,
            'mlstm_w_out': mlstm_w_out, 'diff_w_in': diff_w_in, 'diff_lam': diff_lam,
            'diff_norm': diff_norm, 'diff_w_out': diff_w_out, 'ssd_w_in': ssd_w_in,
            'ssd_conv_w': ssd_conv_w, 'ssd_conv_b': ssd_conv_b, 'ssd_dt_bias': ssd_dt_bias,
            'ssd_A_log': ssd_A_log, 'ssd_D': ssd_D, 'ssd_norm': ssd_norm, 'ssd_w_out': ssd_w_out}


def reference(x, ln_mix, ln_mlp, w_up, w_down, ln_f,
              mlstm_w_in, mlstm_b_gate, mlstm_norm, mlstm_w_out,
              diff_w_in, diff_lam, diff_norm, diff_w_out,
              ssd_w_in, ssd_conv_w, ssd_conv_b, ssd_dt_bias, ssd_A_log, ssd_D, ssd_norm, ssd_w_out):
    h = x
    for i in range(DEPTH):
        kind, j = i % N_MIXERS, i // N_MIXERS
        u = rmsnorm(h, ln_mix[i])
        if kind == 0:
            mix = mlstm_mixer(u, mlstm_w_in[j], mlstm_b_gate[j], mlstm_norm[j], mlstm_w_out[j])
        elif kind == 1:
            mix = diff_attention_mixer(u, diff_w_in[j], diff_lam[j], diff_norm[j], diff_w_out[j], i)
        else:
            mix = ssd_mixer(u, ssd_w_in[j], ssd_conv_w[j], ssd_conv_b[j], ssd_dt_bias[j],
                            ssd_A_log[j], ssd_D[j], ssd_norm[j], ssd_w_out[j])
        h = h + mix
        h = h + sq_relu_mlp(rmsnorm(h, ln_mlp[i]), w_up[i], w_down[i])
    return rmsnorm(h, ln_f)
```

```python
import functools
import math

import jax
import jax.numpy as jnp
from jax import lax
from jax.experimental import pallas as pl
from jax.experimental.pallas import tpu as pltpu

F32 = jnp.float32
BF16 = jnp.bfloat16

DEPTH = 4
N_MIXERS = 3
EPS = 1e-6

MLSTM_HEADS = 4
MLSTM_DV = 256
MLSTM_DQK = 128
MLSTM_CHUNK = 128
GATE_SOFTCAP = 15.0

DIFF_HEADS = 8
DIFF_DH = 64
DIFF_DV = 128
ROPE_THETA = 10000.0

SSD_INNER = 2048
SSD_HEADDIM = 64
SSD_HEADS = 32
SSD_GROUPS = 8
SSD_STATE = 128
SSD_CONV = 4
SSD_CHUNK = 128
SSD_GROUP_WIDTH = SSD_INNER // SSD_GROUPS
SSD_HEADS_PER_GROUP = SSD_HEADS // SSD_GROUPS

LANES = 128
CONV_HALO = 8
VMEM_LIMIT = 48 * 1024 * 1024

_NT = (((1,), (1,)), ((), ()))


def _params(*sem):
    return pltpu.CompilerParams(dimension_semantics=sem, vmem_limit_bytes=VMEM_LIMIT)


def _rms(x, g):
    ms = jnp.mean(x * x, axis=-1, keepdims=True)
    return x * lax.rsqrt(ms + EPS) * g


def _cumsum_rows(x):
    n = x.shape[0]
    row = lax.broadcasted_iota(jnp.int32, x.shape, 0)
    s = 1
    while s < n:
        x = x + jnp.where(row >= s, pltpu.roll(x, s, 0), 0.0)
        s *= 2
    return x


def _softplus(x):
    return jnp.maximum(x, 0.0) + jnp.log1p(jnp.exp(-jnp.abs(x)))


def _pad_cols(w, n):
    return jnp.pad(w, ((0, 0), (0, n - w.shape[1])))


def _norm_proj_body(x_ref, g_ref, w_ref, o_ref, xn_ref):
    @pl.when(pl.program_id(1) == 0)
    def _():
        xn_ref[...] = _rms(x_ref[...], g_ref[...]).astype(BF16)

    o_ref[...] = jnp.dot(xn_ref[...], w_ref[...], preferred_element_type=F32).astype(o_ref.dtype)


def _norm_proj(h, g, w, tn, tm=1024):
    n, d = h.shape
    nout = w.shape[1]
    tm = min(tm, n)
    return pl.pallas_call(
        _norm_proj_body,
        grid=(n // tm, nout // tn),
        in_specs=[pl.BlockSpec((tm, d), lambda i, j: (i, 0)),
                  pl.BlockSpec((1, d), lambda i, j: (0, 0)),
                  pl.BlockSpec((d, tn), lambda i, j: (0, j))],
        out_specs=pl.BlockSpec((tm, tn), lambda i, j: (i, j)),
        out_shape=jax.ShapeDtypeStruct((n, nout), F32),
        scratch_shapes=[pltpu.VMEM((tm, d), BF16)],
        compiler_params=_params("parallel", "arbitrary"),
        name="norm_proj",
    )(h, g.reshape(1, d), w)


def _rope_proj_body(x_ref, g_ref, w_ref, cos_ref, sin_ref, o_ref, xn_ref):
    j = pl.program_id(1)

    @pl.when(j == 0)
    def _():
        xn_ref[...] = _rms(x_ref[...], g_ref[...]).astype(BF16)

    acc = jnp.dot(xn_ref[...], w_ref[...], preferred_element_type=F32)

    @pl.when(j < 2)
    def _():
        cos = cos_ref[...]
        sin = sin_ref[...]
        lane = lax.broadcasted_iota(jnp.int32, cos.shape, 1)
        first_half = (lane & (DIFF_DH - 1)) < DIFF_DH // 2
        qscale = jnp.where(j == 0, DIFF_DH ** -0.5, 1.0).astype(F32)
        for c in range(acc.shape[1] // LANES):
            xc = acc[:, c * LANES:(c + 1) * LANES]
            rot = jnp.where(first_half, pltpu.roll(xc, LANES - DIFF_DH // 2, 1),
                            pltpu.roll(xc, DIFF_DH // 2, 1))
            o_ref[:, c * LANES:(c + 1) * LANES] = ((xc * cos + rot * sin) * qscale).astype(o_ref.dtype)

    @pl.when(j >= 2)
    def _():
        o_ref[...] = acc.astype(o_ref.dtype)


def _rope_proj(h, g, w, cos, sin, seq, tm=1024):
    n, d = h.shape
    nout = w.shape[1]
    tm = min(tm, seq)
    tn = d
    nseq = seq // tm
    return pl.pallas_call(
        _rope_proj_body,
        grid=(n // tm, nout // tn),
        in_specs=[pl.BlockSpec((tm, d), lambda i, j: (i, 0)),
                  pl.BlockSpec((1, d), lambda i, j: (0, 0)),
                  pl.BlockSpec((d, tn), lambda i, j: (0, j)),
                  pl.BlockSpec((tm, LANES), lambda i, j: (i % nseq, 0)),
                  pl.BlockSpec((tm, LANES), lambda i, j: (i % nseq, 0))],
        out_specs=pl.BlockSpec((tm, tn), lambda i, j: (i, j)),
        out_shape=jax.ShapeDtypeStruct((n, nout), BF16),
        scratch_shapes=[pltpu.VMEM((tm, d), BF16)],
        compiler_params=_params("parallel", "arbitrary"),
        name="rope_proj",
    )(h, g.reshape(1, d), w, cos, sin)


def _out_proj_body(a_ref, w_ref, h_ref, o_ref):
    o_ref[...] = h_ref[...] + jnp.dot(a_ref[...], w_ref[...], preferred_element_type=F32)


def _out_proj(a, w, h, tm=1024):
    n, k = a.shape
    d = w.shape[1]
    tm = min(tm, n)
    return pl.pallas_call(
        _out_proj_body,
        grid=(n // tm,),
        in_specs=[pl.BlockSpec((tm, k), lambda i: (i, 0)),
                  pl.BlockSpec((k, d), lambda i: (0, 0)),
                  pl.BlockSpec((tm, d), lambda i: (i, 0))],
        out_specs=pl.BlockSpec((tm, d), lambda i: (i, 0)),
        out_shape=jax.ShapeDtypeStruct((n, d), F32),
        compiler_params=_params("parallel"),
        name="out_proj",
    )(a, w, h)


def _mlp_body(x_ref, g_ref, wu_ref, wd_ref, gf_ref, o_ref, xn_ref, acc_ref, *, final_norm):
    j = pl.program_id(1)

    @pl.when(j == 0)
    def _():
        xn_ref[...] = _rms(x_ref[...], g_ref[...]).astype(BF16)
        acc_ref[...] = jnp.zeros_like(acc_ref)

    a = jnp.dot(xn_ref[...], wu_ref[...], preferred_element_type=F32)
    a = jnp.square(jnp.maximum(a, 0.0)).astype(BF16)
    acc_ref[...] += jnp.dot(a, wd_ref[...], preferred_element_type=F32)

    @pl.when(j == pl.num_programs(1) - 1)
    def _():
        y = x_ref[...] + acc_ref[...]
        if final_norm:
            y = _rms(y, gf_ref[...])
        o_ref[...] = y


def _mlp(h, g, w_up, w_down, g_final, final_norm, tm=1024, tf=1024):
    n, d = h.shape
    dff = w_up.shape[1]
    tm = min(tm, n)
    return pl.pallas_call(
        functools.partial(_mlp_body, final_norm=final_norm),
        grid=(n // tm, dff // tf),
        in_specs=[pl.BlockSpec((tm, d), lambda i, j: (i, 0)),
                  pl.BlockSpec((1, d), lambda i, j: (0, 0)),
                  pl.BlockSpec((d, tf), lambda i, j: (0, j)),
                  pl.BlockSpec((tf, d), lambda i, j: (j, 0)),
                  pl.BlockSpec((1, d), lambda i, j: (0, 0))],
        out_specs=pl.BlockSpec((tm, d), lambda i, j: (i, 0)),
        out_shape=jax.ShapeDtypeStruct((n, d), F32),
        scratch_shapes=[pltpu.VMEM((tm, d), BF16), pltpu.VMEM((tm, d), F32)],
        compiler_params=_params("parallel", "arbitrary"),
        name="mlp",
    )(h, g.reshape(1, d), w_up, w_down, g_final.reshape(1, d))


MLSTM_QK = MLSTM_HEADS * MLSTM_DQK
MLSTM_GATE_COL = 2 * MLSTM_QK + 2 * MLSTM_HEADS * MLSTM_DV
MLSTM_PROJ = MLSTM_GATE_COL + LANES


def _mlstm_body(p_ref, bias_ref, ng_ref, out_ref, c_ref, n_ref, m_ref):
    L = MLSTM_CHUNK
    nh = MLSTM_HEADS

    @pl.when(pl.program_id(1) == 0)
    def _():
        c_ref[...] = jnp.zeros_like(c_ref)
        n_ref[...] = jnp.zeros_like(n_ref)
        m_ref[...] = jnp.zeros_like(m_ref)

    gates = p_ref[:, MLSTM_GATE_COL:MLSTM_GATE_COL + LANES] + bias_ref[...]
    capped = GATE_SOFTCAP * jnp.tanh(gates / GATE_SOFTCAP)
    log_i = capped
    log_f = -_softplus(-capped)
    b = _cumsum_rows(log_f)
    b_t = b.T
    log_i_t = log_i.T
    row = lax.broadcasted_iota(jnp.int32, (L, L), 0)
    col = lax.broadcasted_iota(jnp.int32, (L, L), 1)
    causal = col <= row
    qscale = MLSTM_DQK ** -0.5

    for h in range(nh):
        q32 = p_ref[:, h * MLSTM_DQK:(h + 1) * MLSTM_DQK]
        k32 = p_ref[:, MLSTM_QK + h * MLSTM_DQK:MLSTM_QK + (h + 1) * MLSTM_DQK]
        v_off = 2 * MLSTM_QK + h * MLSTM_DV
        o_off = v_off + nh * MLSTM_DV
        v = p_ref[:, v_off:v_off + MLSTM_DV].astype(BF16)
        q = q32.astype(BF16)
        k = k32.astype(BF16)

        bc = b[:, nh + h:nh + h + 1]
        br = b_t[nh + h:nh + h + 1, :]
        lic = log_i[:, h:h + 1]
        lir = log_i_t[h:h + 1, :]
        b_last = bc[L - 1:L, :]
        c_in = c_ref[h]
        n_in = n_ref[h]
        m_in = m_ref[h][:, 0:1]

        d_log = jnp.where(causal, bc - br + lir, -jnp.inf)
        a_inter = bc + m_in
        m_t = jnp.maximum(a_inter, jnp.max(d_log, axis=1, keepdims=True))
        qk = lax.dot_general(q, k, _NT, preferred_element_type=F32) * qscale
        p = jnp.exp(d_log - m_t) * qk
        scale = jnp.exp(a_inter - m_t)
        q_c = jnp.dot(q, c_in.astype(BF16), preferred_element_type=F32) * qscale
        num = jnp.dot(p.astype(BF16), v, preferred_element_type=F32) + scale * q_c
        q_n = jnp.sum(q32 * n_in, axis=1, keepdims=True) * qscale
        den = jnp.sum(p, axis=1, keepdims=True) + scale * q_n
        hs = num / jnp.maximum(jnp.abs(den), jnp.exp(-m_t))

        w_log = b_last - bc + lic
        m_loc = jnp.max(w_log, axis=0, keepdims=True)
        kw = k32 * jnp.exp(w_log - m_loc)
        c_loc = jnp.dot(kw.T.astype(BF16), v, preferred_element_type=F32)
        n_loc = jnp.sum(kw, axis=0, keepdims=True)
        m_new = jnp.maximum(b_last + m_in, m_loc)
        a_old = jnp.exp(b_last + m_in - m_new)
        a_loc = jnp.exp(m_loc - m_new)
        c_ref[h] = a_old * c_in + a_loc * c_loc
        n_ref[h] = a_old * n_in + a_loc * n_loc
        m_ref[h] = jnp.broadcast_to(m_new, (1, LANES))

        hn = _rms(hs, ng_ref[:, h * MLSTM_DV:(h + 1) * MLSTM_DV])
        o_gate = jax.nn.sigmoid(p_ref[:, o_off:o_off + MLSTM_DV])
        out_ref[:, h * MLSTM_DV:(h + 1) * MLSTM_DV] = (o_gate * hn).astype(out_ref.dtype)


def _mlstm_core(proj, bias, norm_g, bsz, seq):
    L = MLSTM_CHUNK
    nc = seq // L
    d = MLSTM_HEADS * MLSTM_DV
    return pl.pallas_call(
        _mlstm_body,
        grid=(bsz, nc),
        in_specs=[pl.BlockSpec((L, MLSTM_PROJ), lambda b, c: (b * nc + c, 0)),
                  pl.BlockSpec((1, LANES), lambda b, c: (0, 0)),
                  pl.BlockSpec((1, d), lambda b, c: (0, 0))],
        out_specs=pl.BlockSpec((L, d), lambda b, c: (b * nc + c, 0)),
        out_shape=jax.ShapeDtypeStruct((bsz * seq, d), BF16),
        scratch_shapes=[pltpu.VMEM((MLSTM_HEADS, MLSTM_DQK, MLSTM_DV), F32),
                        pltpu.VMEM((MLSTM_HEADS, 1, MLSTM_DQK), F32),
                        pltpu.VMEM((MLSTM_HEADS, 1, LANES), F32)],
        compiler_params=_params("parallel", "arbitrary"),
        name="mlstm_core",
    )(proj, bias, norm_g.reshape(1, d))


def _mlstm_layer(h, ln_g, w_in, b_gate, norm_g, w_out, bsz, seq):
    w = _pad_cols(w_in, MLSTM_PROJ).astype(BF16)
    bias = jnp.pad(b_gate.astype(F32), (0, LANES - b_gate.shape[0])).reshape(1, LANES)
    proj = _norm_proj(h, ln_g, w, tn=MLSTM_PROJ // 5)
    mixed = _mlstm_core(proj, bias, norm_g, bsz, seq)
    return _out_proj(mixed, w_out.astype(BF16), h)


def _diff_attn_body(lam_ref, g_ref, q_ref, k_ref, v_ref, o_ref, *, tq, lam_init):
    qi = pl.program_id(2)
    q = q_ref[...]
    lane = lax.broadcasted_iota(jnp.int32, q.shape, 1)
    zero = jnp.zeros_like(q)
    q_parts = (jnp.where(lane < DIFF_DH, q, zero), jnp.where(lane >= DIFF_DH, q, zero))
    row = lax.broadcasted_iota(jnp.int32, (tq, tq), 0)
    col = lax.broadcasted_iota(jnp.int32, (tq, tq), 1)
    causal = col <= row

    def block(ki, carry, masked):
        start = pl.multiple_of(ki * tq, tq)
        k = k_ref[pl.ds(start, tq), :]
        v = v_ref[pl.ds(start, tq), :]
        new = []
        for c in range(2):
            m, l, acc = carry[c]
            s = lax.dot_general(q_parts[c], k, _NT, preferred_element_type=F32)
            if masked:
                s = jnp.where(causal, s, -jnp.inf)
            m_new = jnp.maximum(m, jnp.max(s, axis=1, keepdims=True))
            alpha = jnp.exp(m - m_new)
            p = jnp.exp(s - m_new)
            l = alpha * l + jnp.sum(p, axis=1, keepdims=True)
            acc = alpha * acc + jnp.dot(p.astype(BF16), v, preferred_element_type=F32)
            new.append((m_new, l, acc))
        return tuple(new)

    init = tuple((jnp.full((tq, 1), -jnp.inf, F32), jnp.zeros((tq, 1), F32),
                  jnp.zeros((tq, DIFF_DV), F32)) for _ in range(2))
    carry = lax.fori_loop(0, qi, lambda ki, c: block(ki, c, False), init)
    (_, l0, a0), (_, l1, a1) = block(qi, carry, True)

    lam = lam_ref[...]
    lam_val = (jnp.exp(jnp.sum(lam[0:1] * lam[1:2], axis=1, keepdims=True))
               - jnp.exp(jnp.sum(lam[2:3] * lam[3:4], axis=1, keepdims=True)) + lam_init)
    o = a0 / l0 - lam_val * (a1 / l1)
    o_ref[...] = (_rms(o, g_ref[...]) * (1.0 - lam_init)).astype(o_ref.dtype)


def _diff_attn(qkv, lam, norm_g, bsz, seq, layer_idx, tq=512):
    tq = min(tq, seq)
    nq = seq // tq
    nh = DIFF_HEADS
    lam_init = 0.8 - 0.6 * math.exp(-0.3 * layer_idx)
    return pl.pallas_call(
        functools.partial(_diff_attn_body, tq=tq, lam_init=lam_init),
        grid=(bsz, nh, nq),
        in_specs=[pl.BlockSpec((4, DIFF_DH), lambda b, h, i: (0, 0)),
                  pl.BlockSpec((1, DIFF_DV), lambda b, h, i: (0, 0)),
                  pl.BlockSpec((tq, DIFF_DV), lambda b, h, i: (b * nq + i, h)),
                  pl.BlockSpec((seq, DIFF_DV), lambda b, h, i: (b, nh + h)),
                  pl.BlockSpec((seq, DIFF_DV), lambda b, h, i: (b, 2 * nh + h))],
        out_specs=pl.BlockSpec((tq, DIFF_DV), lambda b, h, i: (b * nq + i, h)),
        out_shape=jax.ShapeDtypeStruct((bsz * seq, nh * DIFF_DV), BF16),
        compiler_params=_params("parallel", "parallel", "arbitrary"),
        name="diff_attn",
    )(lam.astype(F32), norm_g.reshape(1, DIFF_DV).astype(F32), qkv, qkv, qkv)


def _rope_tables(seq):
    inv = ROPE_THETA ** (-jnp.arange(0, DIFF_DH, 2, dtype=F32) / DIFF_DH)
    ang = jnp.arange(seq, dtype=F32)[:, None] * inv[None, :]
    cos, sin = jnp.cos(ang), jnp.sin(ang)
    cos_t = jnp.tile(cos, (1, LANES // cos.shape[1]))
    sin_t = jnp.tile(jnp.concatenate([-sin, sin], axis=1), (1, LANES // (2 * sin.shape[1])))
    return cos_t, sin_t


def _diff_layer(h, ln_g, w_in, lam, norm_g, w_out, bsz, seq, layer_idx):
    cos, sin = _rope_tables(seq)
    qkv = _rope_proj(h, ln_g, w_in.astype(BF16), cos, sin, seq)
    mixed = _diff_attn(qkv, lam, norm_g, bsz, seq, layer_idx)
    return _out_proj(mixed, w_out.astype(BF16), h)


SSD_BC = SSD_GROUPS * SSD_STATE
SSD_CONV_DIM = SSD_INNER + 2 * SSD_BC
SSD_DT_COL = SSD_INNER + SSD_CONV_DIM
SSD_PROJ = SSD_DT_COL + LANES


def _expand_heads(a, g, width):
    nr = SSD_HEADS_PER_GROUP
    lane = lax.broadcasted_iota(jnp.int32, (a.shape[0], width), 1)
    out = jnp.broadcast_to(a[:, nr * g + nr - 1:nr * g + nr], (a.shape[0], width))
    for r in range(nr - 2, -1, -1):
        out = jnp.where(lane < SSD_HEADDIM * (r + 1), a[:, nr * g + r:nr * g + r + 1], out)
    return out


def _ssd_body(z_ref, x_ref, bc_ref, dt_ref, cw_ref, cb_ref, dtb_ref, alog_ref, dsk_ref, ng_ref,
              out_ref, ext_ref, s_ref):
    L = SSD_CHUNK
    gw = SSD_GROUP_WIDTH
    ns = SSD_STATE

    @pl.when(pl.program_id(1) == 0)
    def _():
        ext_ref[0:CONV_HALO, :] = jnp.zeros((CONV_HALO, SSD_CONV_DIM), F32)
        s_ref[...] = jnp.zeros_like(s_ref)

    ext_ref[CONV_HALO:CONV_HALO + L, 0:SSD_INNER] = x_ref[...]
    ext_ref[CONV_HALO:CONV_HALO + L, SSD_INNER:SSD_CONV_DIM] = bc_ref[...]

    def conv_silu(lo, width):
        acc = jnp.broadcast_to(cb_ref[:, lo:lo + width], (L, width))
        for j in range(SSD_CONV):
            r0 = CONV_HALO - (SSD_CONV - 1) + j
            acc = acc + cw_ref[j:j + 1, lo:lo + width] * ext_ref[r0:r0 + L, lo:lo + width]
        return jax.nn.silu(acc)

    dt = _softplus(dt_ref[...] + dtb_ref[...])
    cum = _cumsum_rows(dt * (-jnp.exp(alog_ref[...])))
    cum_t = cum.T
    cum_last = cum[L - 1:L, :]
    exp_cum = jnp.exp(cum)
    decay_to_end = jnp.exp(cum_last - cum)
    chunk_decay = jnp.exp(cum_last)
    row = lax.broadcasted_iota(jnp.int32, (L, L), 0)
    col = lax.broadcasted_iota(jnp.int32, (L, L), 1)
    causal = col <= row
    lane_gw = lax.broadcasted_iota(jnp.int32, (L, gw), 1)

    for g in range(SSD_GROUPS):
        xs = conv_silu(g * gw, gw)
        bm = conv_silu(SSD_INNER + g * ns, ns).astype(BF16)
        cm = conv_silu(SSD_INNER + SSD_BC + g * ns, ns).astype(BF16)
        xdt = xs * _expand_heads(dt, g, gw)
        s_in = s_ref[g]
        cb = lax.dot_general(cm, bm, _NT, preferred_element_type=F32)
        y = _expand_heads(exp_cum, g, gw) * jnp.dot(cm, s_in.astype(BF16), preferred_element_type=F32)
        for r in range(SSD_HEADS_PER_GROUP):
            hh = SSD_HEADS_PER_GROUP * g + r
            decay = jnp.exp(jnp.where(causal, cum[:, hh:hh + 1] - cum_t[hh:hh + 1, :], -jnp.inf))
            in_head = (lane_gw >= SSD_HEADDIM * r) & (lane_gw < SSD_HEADDIM * (r + 1))
            y = y + jnp.dot((cb * decay).astype(BF16), jnp.where(in_head, xdt, 0.0).astype(BF16),
                            preferred_element_type=F32)
        xdte = (xdt * _expand_heads(decay_to_end, g, gw)).astype(BF16)
        states = jnp.dot(bm.astype(F32).T.astype(BF16), xdte, preferred_element_type=F32)
        s_ref[g] = _expand_heads(chunk_decay, g, gw) * s_in + states
        y = y + dsk_ref[:, g * gw:(g + 1) * gw] * xs
        y = y * jax.nn.silu(z_ref[:, g * gw:(g + 1) * gw])
        out_ref[:, g * gw:(g + 1) * gw] = _rms(y, ng_ref[:, g * gw:(g + 1) * gw]).astype(out_ref.dtype)

    ext_ref[0:CONV_HALO, :] = ext_ref[L:L + CONV_HALO, :]


def _ssd_core(proj, conv_w, conv_b, dt_bias, a_log, d_skip, norm_g, bsz, seq):
    L = SSD_CHUNK
    nc = seq // L
    rowmap = lambda b, c: (b * nc + c, 0)
    const = lambda b, c: (0, 0)
    return pl.pallas_call(
        _ssd_body,
        grid=(bsz, nc),
        in_specs=[pl.BlockSpec((L, SSD_INNER), rowmap),
                  pl.BlockSpec((L, SSD_INNER), lambda b, c: (b * nc + c, 1)),
                  pl.BlockSpec((L, 2 * SSD_BC), lambda b, c: (b * nc + c, 2)),
                  pl.BlockSpec((L, LANES), lambda b, c: (b * nc + c, SSD_DT_COL // LANES)),
                  pl.BlockSpec((SSD_CONV, SSD_CONV_DIM), const),
                  pl.BlockSpec((1, SSD_CONV_DIM), const),
                  pl.BlockSpec((1, LANES), const),
                  pl.BlockSpec((1, LANES), const),
                  pl.BlockSpec((1, SSD_INNER), const),
                  pl.BlockSpec((1, SSD_INNER), const)],
        out_specs=pl.BlockSpec((L, SSD_INNER), rowmap),
        out_shape=jax.ShapeDtypeStruct((bsz * seq, SSD_INNER), BF16),
        scratch_shapes=[pltpu.VMEM((L + CONV_HALO, SSD_CONV_DIM), F32),
                        pltpu.VMEM((SSD_GROUPS, SSD_STATE, SSD_GROUP_WIDTH), F32)],
        compiler_params=_params("parallel", "arbitrary"),
        name="ssd_core",
    )(proj, proj, proj, proj, conv_w, conv_b, dt_bias, a_log, d_skip, norm_g)


def _ssd_layer(h, ln_g, w_in, conv_w, conv_b, dt_bias, a_log, d_skip, norm_g, w_out, bsz, seq):
    w = _pad_cols(w_in, SSD_PROJ).astype(BF16)
    proj = _norm_proj(h, ln_g, w, tn=SSD_PROJ // 7)
    pad_heads = lambda a: jnp.pad(a.astype(F32), (0, LANES - SSD_HEADS)).reshape(1, LANES)
    mixed = _ssd_core(proj, conv_w.astype(F32), conv_b.astype(F32).reshape(1, SSD_CONV_DIM),
                      pad_heads(dt_bias), pad_heads(a_log),
                      jnp.repeat(d_skip.astype(F32), SSD_HEADDIM).reshape(1, SSD_INNER),
                      norm_g.astype(F32).reshape(1, SSD_INNER), bsz, seq)
    return _out_proj(mixed, w_out.astype(BF16), h)


def kernel(x, ln_mix, ln_mlp, w_up, w_down, ln_f, mlstm_w_in, mlstm_b_gate, mlstm_norm, mlstm_w_out, diff_w_in, diff_lam, diff_norm, diff_w_out, ssd_w_in, ssd_conv_w, ssd_conv_b, ssd_dt_bias, ssd_A_log, ssd_D, ssd_norm, ssd_w_out):
    bsz, seq, d = x.shape
    h = x.reshape(bsz * seq, d)
    for i in range(DEPTH):
        kind, j = i % N_MIXERS, i // N_MIXERS
        if kind == 0:
            h = _mlstm_layer(h, ln_mix[i], mlstm_w_in[j], mlstm_b_gate[j], mlstm_norm[j],
                             mlstm_w_out[j], bsz, seq)
        elif kind == 1:
            h = _diff_layer(h, ln_mix[i], diff_w_in[j], diff_lam[j], diff_norm[j], diff_w_out[j],
                            bsz, seq, i)
        else:
            h = _ssd_layer(h, ln_mix[i], ssd_w_in[j], ssd_conv_w[j], ssd_conv_b[j], ssd_dt_bias[j],
                           ssd_A_log[j], ssd_D[j], ssd_norm[j], ssd_w_out[j], bsz, seq)
        h = _mlp(h, ln_mlp[i], w_up[i].astype(BF16), w_down[i].astype(BF16), ln_f,
                 final_norm=(i == DEPTH - 1))
    return h.reshape(bsz, seq, d)
```

```python
import functools
import math

import jax
import jax.numpy as jnp
from jax import lax
from jax.experimental import pallas as pl
from jax.experimental.pallas import tpu as pltpu

F32 = jnp.float32
BF16 = jnp.bfloat16

DEPTH = 4
N_MIXERS = 3
EPS = 1e-6

MLSTM_HEADS = 4
MLSTM_DV = 256
MLSTM_DQK = 128
MLSTM_CHUNK = 128
GATE_SOFTCAP = 15.0

DIFF_HEADS = 8
DIFF_DH = 64
DIFF_DV = 128
ROPE_THETA = 10000.0

SSD_INNER = 2048
SSD_HEADDIM = 64
SSD_HEADS = 32
SSD_GROUPS = 8
SSD_STATE = 128
SSD_CONV = 4
SSD_CHUNK = 128
SSD_GROUP_WIDTH = SSD_INNER // SSD_GROUPS
SSD_HEADS_PER_GROUP = SSD_HEADS // SSD_GROUPS

LANES = 128
CONV_HALO = 8
VMEM_LIMIT = 48 * 1024 * 1024

_NT = (((1,), (1,)), ((), ()))


def _params(*sem):
    return pltpu.CompilerParams(dimension_semantics=sem, vmem_limit_bytes=VMEM_LIMIT)


def _rms(x, g):
    ms = jnp.mean(x * x, axis=-1, keepdims=True)
    return x * lax.rsqrt(ms + EPS) * g


def _cumsum_rows(x):
    n = x.shape[0]
    row = lax.broadcasted_iota(jnp.int32, x.shape, 0)
    s = 1
    while s < n:
        x = x + jnp.where(row >= s, pltpu.roll(x, s, 0), 0.0)
        s *= 2
    return x


COL_REDUCE_SLAB = 64


def _col_reduce(x, pair_op, reduce_fn):
    slab = min(COL_REDUCE_SLAB, x.shape[0])
    acc = x[:slab]
    for r0 in range(slab, x.shape[0], slab):
        acc = pair_op(acc, x[r0:r0 + slab])
    while acc.shape[0] > 8:
        half = acc.shape[0] // 2
        acc = pair_op(acc[:half], acc[half:])
    return reduce_fn(acc, axis=0, keepdims=True)


def _softplus(x):
    return jnp.maximum(x, 0.0) + jnp.log1p(jnp.exp(-jnp.abs(x)))


def _pad_cols(w, n):
    return jnp.pad(w, ((0, 0), (0, n - w.shape[1])))


def _norm_proj_body(x_ref, g_ref, w_ref, ws_ref, o_ref, os_ref, xn_ref):
    @pl.when(pl.program_id(1) == 0)
    def _():
        xn = _rms(x_ref[...], g_ref[...]).astype(BF16)
        xn_ref[...] = xn
        os_ref[...] = jnp.dot(xn, ws_ref[...], preferred_element_type=F32)

    o_ref[...] = jnp.dot(xn_ref[...], w_ref[...], preferred_element_type=F32).astype(o_ref.dtype)


def _norm_proj(h, g, w, w_side, tn=1024, tm=1024):
    n, d = h.shape
    nout = w.shape[1]
    tm = min(tm, n)
    return pl.pallas_call(
        _norm_proj_body,
        grid=(n // tm, nout // tn),
        in_specs=[pl.BlockSpec((tm, d), lambda i, j: (i, 0)),
                  pl.BlockSpec((1, d), lambda i, j: (0, 0)),
                  pl.BlockSpec((d, tn), lambda i, j: (0, j)),
                  pl.BlockSpec((d, LANES), lambda i, j: (0, 0))],
        out_specs=[pl.BlockSpec((tm, tn), lambda i, j: (i, j)),
                   pl.BlockSpec((tm, LANES), lambda i, j: (i, 0))],
        out_shape=[jax.ShapeDtypeStruct((n, nout), F32), jax.ShapeDtypeStruct((n, LANES), F32)],
        scratch_shapes=[pltpu.VMEM((tm, d), BF16)],
        compiler_params=_params("parallel", "arbitrary"),
        name="norm_proj",
    )(h, g.reshape(1, d), w, w_side)


def _rope_proj_body(x_ref, g_ref, w_ref, cos_ref, sin_ref, o_ref, xn_ref):
    j = pl.program_id(1)

    @pl.when(j == 0)
    def _():
        xn_ref[...] = _rms(x_ref[...], g_ref[...]).astype(BF16)

    acc = jnp.dot(xn_ref[...], w_ref[...], preferred_element_type=F32)

    @pl.when(j < 2)
    def _():
        cos = cos_ref[...]
        sin = sin_ref[...]
        qscale = jnp.where(j == 0, DIFF_DH ** -0.5 * math.log2(math.e), 1.0).astype(F32)
        for c in range(acc.shape[1] // LANES):
            xc = acc[:, c * LANES:(c + 1) * LANES]
            rot = pltpu.roll(xc, LANES // 2, 1)
            o_ref[:, c * LANES:(c + 1) * LANES] = ((xc * cos + rot * sin) * qscale).astype(o_ref.dtype)

    @pl.when(j >= 2)
    def _():
        o_ref[...] = acc.astype(o_ref.dtype)


def _rope_proj(h, g, w, cos, sin, seq, tm=1024):
    n, d = h.shape
    nout = w.shape[1]
    tm = min(tm, seq)
    tn = d
    nseq = seq // tm
    return pl.pallas_call(
        _rope_proj_body,
        grid=(n // tm, nout // tn),
        in_specs=[pl.BlockSpec((tm, d), lambda i, j: (i, 0)),
                  pl.BlockSpec((1, d), lambda i, j: (0, 0)),
                  pl.BlockSpec((d, tn), lambda i, j: (0, j)),
                  pl.BlockSpec((tm, LANES), lambda i, j: (i % nseq, 0)),
                  pl.BlockSpec((tm, LANES), lambda i, j: (i % nseq, 0))],
        out_specs=pl.BlockSpec((tm, tn), lambda i, j: (i, j)),
        out_shape=jax.ShapeDtypeStruct((n, nout), BF16),
        scratch_shapes=[pltpu.VMEM((tm, d), BF16)],
        compiler_params=_params("parallel", "arbitrary"),
        name="rope_proj",
    )(h, g.reshape(1, d), w, cos, sin)


def _out_proj_body(a_ref, w_ref, h_ref, o_ref):
    o_ref[...] = h_ref[...] + jnp.dot(a_ref[...], w_ref[...], preferred_element_type=F32)


def _out_proj(a, w, h, tm=1024):
    n, k = a.shape
    d = w.shape[1]
    tm = min(tm, n)
    return pl.pallas_call(
        _out_proj_body,
        grid=(n // tm,),
        in_specs=[pl.BlockSpec((tm, k), lambda i: (i, 0)),
                  pl.BlockSpec((k, d), lambda i: (0, 0)),
                  pl.BlockSpec((tm, d), lambda i: (i, 0))],
        out_specs=pl.BlockSpec((tm, d), lambda i: (i, 0)),
        out_shape=jax.ShapeDtypeStruct((n, d), F32),
        compiler_params=_params("parallel"),
        name="out_proj",
    )(a, w, h)


def _mlp_body(x_ref, g_ref, wu_ref, wd_ref, gf_ref, o_ref, xn_ref, acc_ref, *, final_norm):
    j = pl.program_id(1)

    @pl.when(j == 0)
    def _():
        xn_ref[...] = _rms(x_ref[...], g_ref[...]).astype(BF16)
        acc_ref[...] = jnp.zeros_like(acc_ref)

    a = jnp.dot(xn_ref[...], wu_ref[...], preferred_element_type=F32)
    a = jnp.square(jnp.maximum(a, 0.0)).astype(BF16)
    acc_ref[...] += jnp.dot(a, wd_ref[...], preferred_element_type=F32)

    @pl.when(j == pl.num_programs(1) - 1)
    def _():
        y = x_ref[...] + acc_ref[...]
        if final_norm:
            y = _rms(y, gf_ref[...])
        o_ref[...] = y


def _mlp(h, g, w_up, w_down, g_final, final_norm, tm=1024, tf=1024):
    n, d = h.shape
    dff = w_up.shape[1]
    tm = min(tm, n)
    return pl.pallas_call(
        functools.partial(_mlp_body, final_norm=final_norm),
        grid=(n // tm, dff // tf),
        in_specs=[pl.BlockSpec((tm, d), lambda i, j: (i, 0)),
                  pl.BlockSpec((1, d), lambda i, j: (0, 0)),
                  pl.BlockSpec((d, tf), lambda i, j: (0, j)),
                  pl.BlockSpec((tf, d), lambda i, j: (j, 0)),
                  pl.BlockSpec((1, d), lambda i, j: (0, 0))],
        out_specs=pl.BlockSpec((tm, d), lambda i, j: (i, 0)),
        out_shape=jax.ShapeDtypeStruct((n, d), F32),
        scratch_shapes=[pltpu.VMEM((tm, d), BF16), pltpu.VMEM((tm, d), F32)],
        compiler_params=_params("parallel", "arbitrary"),
        name="mlp",
    )(h, g.reshape(1, d), w_up, w_down, g_final.reshape(1, d))


MLSTM_QK = MLSTM_HEADS * MLSTM_DQK
MLSTM_MAIN = 2 * MLSTM_QK + 2 * MLSTM_HEADS * MLSTM_DV


def _mlstm_body(p_ref, gate_ref, bias_ref, ng_ref, out_ref, *state_refs):
    L = MLSTM_CHUNK
    nh = MLSTM_HEADS
    n_chain = len(state_refs) // 3
    c_refs, n_refs, m_refs = (state_refs[i * n_chain:(i + 1) * n_chain] for i in range(3))

    @pl.when(pl.program_id(0) == 0)
    def _():
        for ref in state_refs:
            ref[...] = jnp.zeros_like(ref)

    row = lax.broadcasted_iota(jnp.int32, (L, L), 0)
    col = lax.broadcasted_iota(jnp.int32, (L, L), 1)
    causal = col <= row
    qscale = MLSTM_DQK ** -0.5

    for bi in range(p_ref.shape[0]):
        gates = gate_ref[bi] + bias_ref[...]
        capped = GATE_SOFTCAP * jnp.tanh(gates / GATE_SOFTCAP)
        log_i = capped
        log_f = -_softplus(-capped)
        b = _cumsum_rows(log_f)
        b_t = b.T
        log_i_t = log_i.T

        for h in range(nh):
            st = bi * nh + h
            q32 = p_ref[bi, :, h * MLSTM_DQK:(h + 1) * MLSTM_DQK]
            k32 = p_ref[bi, :, MLSTM_QK + h * MLSTM_DQK:MLSTM_QK + (h + 1) * MLSTM_DQK]
            v_off = 2 * MLSTM_QK + h * MLSTM_DV
            o_off = v_off + nh * MLSTM_DV
            v = p_ref[bi, :, v_off:v_off + MLSTM_DV].astype(BF16)
            q = q32.astype(BF16)
            k = k32.astype(BF16)

            bc = b[:, nh + h:nh + h + 1]
            br = b_t[nh + h:nh + h + 1, :]
            lic = log_i[:, h:h + 1]
            lir = log_i_t[h:h + 1, :]
            b_last = bc[L - 1:L, :]
            c_in = c_refs[st][...]
            n_in = n_refs[st][...]
            m_in = m_refs[st][:, 0:1]

            d_log = jnp.where(causal, bc - br + lir, -jnp.inf)
            a_inter = bc + m_in
            m_t = jnp.maximum(a_inter, jnp.max(d_log, axis=1, keepdims=True))
            qk = lax.dot_general(q, k, _NT, preferred_element_type=F32) * qscale
            p = jnp.exp(d_log - m_t) * qk
            scale = jnp.exp(a_inter - m_t)
            q_c = jnp.dot(q, c_in.astype(BF16), preferred_element_type=F32) * qscale
            num = jnp.dot(p.astype(BF16), v, preferred_element_type=F32) + scale * q_c
            q_n = jnp.sum(q32 * n_in, axis=1, keepdims=True) * qscale
            den = jnp.sum(p, axis=1, keepdims=True) + scale * q_n
            hs = num / jnp.maximum(jnp.abs(den), jnp.exp(-m_t))

            w_log = b_last - bc + lic
            m_loc = jnp.max(w_log, axis=0, keepdims=True)
            kw = k32 * jnp.exp(w_log - m_loc)
            c_loc = jnp.dot(kw.T.astype(BF16), v, preferred_element_type=F32)
            n_loc = jnp.sum(kw, axis=0, keepdims=True)
            m_new = jnp.maximum(b_last + m_in, m_loc)
            a_old = jnp.exp(b_last + m_in - m_new)
            a_loc = jnp.exp(m_loc - m_new)
            c_refs[st][...] = a_old * c_in + a_loc * c_loc
            n_refs[st][...] = a_old * n_in + a_loc * n_loc
            m_refs[st][...] = jnp.broadcast_to(m_new, (1, LANES))

            hn = _rms(hs, ng_ref[:, h * MLSTM_DV:(h + 1) * MLSTM_DV])
            o_gate = jax.nn.sigmoid(p_ref[bi, :, o_off:o_off + MLSTM_DV])
            out_ref[bi, :, h * MLSTM_DV:(h + 1) * MLSTM_DV] = (o_gate * hn).astype(out_ref.dtype)


def _mlstm_core(proj, gates, bias, norm_g, bsz, seq):
    L = MLSTM_CHUNK
    d = MLSTM_HEADS * MLSTM_DV
    n_chain = bsz * MLSTM_HEADS
    chunk = lambda c: (0, c, 0)
    return pl.pallas_call(
        _mlstm_body,
        grid=(seq // L,),
        in_specs=[pl.BlockSpec((bsz, L, MLSTM_MAIN), chunk),
                  pl.BlockSpec((bsz, L, LANES), chunk),
                  pl.BlockSpec((1, LANES), lambda c: (0, 0)),
                  pl.BlockSpec((1, d), lambda c: (0, 0))],
        out_specs=pl.BlockSpec((bsz, L, d), chunk),
        out_shape=jax.ShapeDtypeStruct((bsz, seq, d), BF16),
        scratch_shapes=([pltpu.VMEM((MLSTM_DQK, MLSTM_DV), F32)] * n_chain
                        + [pltpu.VMEM((1, MLSTM_DQK), F32)] * n_chain
                        + [pltpu.VMEM((1, LANES), F32)] * n_chain),
        compiler_params=_params("arbitrary"),
        name="mlstm_core",
    )(proj.reshape(bsz, seq, MLSTM_MAIN), gates.reshape(bsz, seq, LANES), bias, norm_g.reshape(1, d))


def _mlstm_layer(h, ln_g, w_in, b_gate, norm_g, w_out, bsz, seq):
    w_main = w_in[:, :MLSTM_MAIN].astype(BF16)
    w_gate = _pad_cols(w_in[:, MLSTM_MAIN:], LANES).astype(BF16)
    bias = jnp.pad(b_gate.astype(F32), (0, LANES - b_gate.shape[0])).reshape(1, LANES)
    proj, gates = _norm_proj(h, ln_g, w_main, w_gate)
    mixed = _mlstm_core(proj, gates, bias, norm_g, bsz, seq)
    return _out_proj(mixed.reshape(bsz * seq, -1), w_out.astype(BF16), h)


def _diff_attn_body(lam_ref, g_ref, q_ref, k_ref, vt_ref, o_ref, sc_a, sc_b, m_ref, l_ref, acc_ref, *,
                    tq, lam_init):
    qi = pl.program_id(2)
    q = q_ref[...]
    lane = lax.broadcasted_iota(jnp.int32, q.shape, 1)
    zero = jnp.zeros_like(q)
    is_c1 = (lane & (DIFF_DH // 2)) != 0
    q_parts = (jnp.where(is_c1, zero, q), jnp.where(is_c1, q, zero))
    th = tq // 2
    chains = [(c, half) for c in range(2) for half in range(2)]
    q_ch = [q_parts[c][half * th:(half + 1) * th] for c, half in chains]
    kpos = lax.broadcasted_iota(jnp.int32, (tq, th), 0)
    qpos = lax.broadcasted_iota(jnp.int32, (tq, th), 1)
    causal = [kpos <= qpos + half * th for half in range(2)]

    def put_scores(sc_ref, ki):
        k = k_ref[pl.ds(pl.multiple_of(ki * tq, tq), tq), :]
        for x in range(4):
            sc_ref[x] = lax.dot_general(k, q_ch[x], _NT, preferred_element_type=F32)

    def softmax_pv(ki, sc_ref, masked):
        vt = vt_ref[:, pl.ds(pl.multiple_of(ki * tq, tq), tq)]
        for x, (c, half) in enumerate(chains):
            def load_scores():
                s = sc_ref[x]
                return jnp.where(causal[half], s, -jnp.inf) if masked else s

            m = m_ref[x]
            m_new = jnp.maximum(m, _col_reduce(load_scores(), jnp.maximum, jnp.max))
            alpha = jnp.exp2(m - m_new)
            p = jnp.exp2(load_scores() - m_new)
            m_ref[x] = m_new
            l_ref[x] = alpha * l_ref[x] + _col_reduce(p, jnp.add, jnp.sum)
            acc_ref[x] = alpha * acc_ref[x] + jnp.dot(vt, p.astype(BF16), preferred_element_type=F32)

    m_ref[...] = jnp.full(m_ref.shape, -jnp.inf, F32)
    l_ref[...] = jnp.zeros_like(l_ref)
    acc_ref[...] = jnp.zeros_like(acc_ref)

    odd = qi & 1

    @pl.when(odd == 0)
    def _():
        put_scores(sc_a, 0)

    @pl.when(odd == 1)
    def _():
        put_scores(sc_b, 0)
        put_scores(sc_a, 1)
        softmax_pv(0, sc_b, False)

    def pair(j, carry):
        b0 = odd + 2 * j
        put_scores(sc_b, b0 + 1)
        softmax_pv(b0, sc_a, False)
        put_scores(sc_a, b0 + 2)
        softmax_pv(b0 + 1, sc_b, False)
        return carry

    lax.fori_loop(0, lax.shift_right_logical(qi, 1), pair, 0)
    softmax_pv(qi, sc_a, True)
    a0, a1 = (jnp.concatenate([acc_ref[2 * c] / l_ref[2 * c], acc_ref[2 * c + 1] / l_ref[2 * c + 1]], axis=1)
              for c in range(2))

    lam = lam_ref[...]
    lam_val = (jnp.exp(jnp.sum(lam[0:1] * lam[1:2], axis=1, keepdims=True))
               - jnp.exp(jnp.sum(lam[2:3] * lam[3:4], axis=1, keepdims=True)) + lam_init)
    o = a0 - lam_val * a1
    ms = jnp.mean(o * o, axis=0, keepdims=True)
    y = o * lax.rsqrt(ms + EPS) * g_ref[...] * (1.0 - lam_init)
    o_ref[...] = y.T.astype(o_ref.dtype)


def _diff_attn(qkv, lam, norm_g, bsz, seq, layer_idx, tq=512):
    tq = min(tq, seq)
    nq = seq // tq
    nh = DIFF_HEADS
    lam_init = 0.8 - 0.6 * math.exp(-0.3 * layer_idx)
    v_t = qkv[:, 2 * nh * DIFF_DV:].reshape(bsz, seq, nh, DIFF_DV).transpose(0, 2, 3, 1)
    v_t = v_t.reshape(bsz * nh * DIFF_DV, seq)
    return pl.pallas_call(
        functools.partial(_diff_attn_body, tq=tq, lam_init=lam_init),
        grid=(bsz, nh, nq),
        in_specs=[pl.BlockSpec((4, DIFF_DH), lambda b, h, i: (0, 0)),
                  pl.BlockSpec((DIFF_DV, 1), lambda b, h, i: (0, 0)),
                  pl.BlockSpec((tq, DIFF_DV), lambda b, h, i: (b * nq + i, h)),
                  pl.BlockSpec((seq, DIFF_DV), lambda b, h, i: (b, nh + h)),
                  pl.BlockSpec((DIFF_DV, seq), lambda b, h, i: (b * nh + h, 0))],
        out_specs=pl.BlockSpec((tq, DIFF_DV), lambda b, h, i: (b * nq + i, h)),
        out_shape=jax.ShapeDtypeStruct((bsz * seq, nh * DIFF_DV), BF16),
        scratch_shapes=[pltpu.VMEM((4, tq, tq // 2), F32), pltpu.VMEM((4, tq, tq // 2), F32),
                        pltpu.VMEM((4, 1, tq // 2), F32), pltpu.VMEM((4, 1, tq // 2), F32),
                        pltpu.VMEM((4, DIFF_DV, tq // 2), F32)],
        compiler_params=_params("parallel", "parallel", "arbitrary"),
        name="diff_attn",
    )(lam.astype(F32), norm_g.reshape(DIFF_DV, 1).astype(F32), qkv, qkv, v_t)


def _rope_tables(seq):
    inv = ROPE_THETA ** (-jnp.arange(0, DIFF_DH, 2, dtype=F32) / DIFF_DH)
    ang = jnp.arange(seq, dtype=F32)[:, None] * inv[None, :]
    cos, sin = jnp.cos(ang), jnp.sin(ang)
    cos_t = jnp.tile(cos, (1, LANES // cos.shape[1]))
    sin_t = jnp.concatenate([-sin, -sin, sin, sin], axis=1)
    return cos_t, sin_t


def _rope_lane_order():
    half = DIFF_DH // 2
    return [c * DIFF_DH + part * half + i for part in range(2) for c in range(2) for i in range(half)]


def _diff_layer(h, ln_g, w_in, lam, norm_g, w_out, bsz, seq, layer_idx):
    cos, sin = _rope_tables(seq)
    order = jnp.asarray(_rope_lane_order(), jnp.int32)
    nqk = 2 * DIFF_HEADS * DIFF_DV
    qk_cols = (jnp.arange(nqk // LANES, dtype=jnp.int32) * LANES)[:, None] + order[None, :]
    w_in = jnp.concatenate([w_in[:, qk_cols.reshape(-1)], w_in[:, nqk:]], axis=1)
    qkv = _rope_proj(h, ln_g, w_in.astype(BF16), cos, sin, seq)
    mixed = _diff_attn(qkv, lam, norm_g, bsz, seq, layer_idx)
    return _out_proj(mixed, w_out.astype(BF16), h)


SSD_BC = SSD_GROUPS * SSD_STATE
SSD_CONV_DIM = SSD_INNER + 2 * SSD_BC
SSD_MAIN = SSD_INNER + SSD_CONV_DIM


def _expand_heads(a, g, width):
    nr = SSD_HEADS_PER_GROUP
    lane = lax.broadcasted_iota(jnp.int32, (a.shape[0], width), 1)
    out = jnp.broadcast_to(a[:, nr * g + nr - 1:nr * g + nr], (a.shape[0], width))
    for r in range(nr - 2, -1, -1):
        out = jnp.where(lane < SSD_HEADDIM * (r + 1), a[:, nr * g + r:nr * g + r + 1], out)
    return out


def _ssd_body(z_ref, x_ref, bc_ref, dt_ref, cw_ref, cb_ref, dtb_ref, alog_ref, dsk_ref, ng_ref,
              out_ref, ext_ref, s_ref):
    L = SSD_CHUNK
    gw = SSD_GROUP_WIDTH
    ns = SSD_STATE

    @pl.when(pl.program_id(1) == 0)
    def _():
        ext_ref[0:CONV_HALO, :] = jnp.zeros((CONV_HALO, SSD_CONV_DIM), F32)
        s_ref[...] = jnp.zeros_like(s_ref)

    ext_ref[CONV_HALO:CONV_HALO + L, 0:SSD_INNER] = x_ref[...]
    ext_ref[CONV_HALO:CONV_HALO + L, SSD_INNER:SSD_CONV_DIM] = bc_ref[...]

    def conv_silu(lo, width):
        acc = jnp.broadcast_to(cb_ref[:, lo:lo + width], (L, width))
        for j in range(SSD_CONV):
            r0 = CONV_HALO - (SSD_CONV - 1) + j
            acc = acc + cw_ref[j:j + 1, lo:lo + width] * ext_ref[r0:r0 + L, lo:lo + width]
        return jax.nn.silu(acc)

    dt = _softplus(dt_ref[...] + dtb_ref[...])
    cum = _cumsum_rows(dt * (-jnp.exp(alog_ref[...])))
    cum_t = cum.T
    cum_last = cum[L - 1:L, :]
    exp_cum = jnp.exp(cum)
    decay_to_end = jnp.exp(cum_last - cum)
    chunk_decay = jnp.exp(cum_last)
    row = lax.broadcasted_iota(jnp.int32, (L, L), 0)
    col = lax.broadcasted_iota(jnp.int32, (L, L), 1)
    causal = col <= row
    lane_gw = lax.broadcasted_iota(jnp.int32, (L, gw), 1)
    in_head = [(lane_gw >= SSD_HEADDIM * r) & (lane_gw < SSD_HEADDIM * (r + 1))
               for r in range(SSD_HEADS_PER_GROUP)]

    for g in range(SSD_GROUPS):
        xs = conv_silu(g * gw, gw)
        bm = conv_silu(SSD_INNER + g * ns, ns).astype(BF16)
        cm = conv_silu(SSD_INNER + SSD_BC + g * ns, ns).astype(BF16)
        xdt = xs * _expand_heads(dt, g, gw)
        s_in = s_ref[g]
        cb = lax.dot_general(cm, bm, _NT, preferred_element_type=F32)
        y = _expand_heads(exp_cum, g, gw) * jnp.dot(cm, s_in.astype(BF16), preferred_element_type=F32)
        for r in range(SSD_HEADS_PER_GROUP):
            hh = SSD_HEADS_PER_GROUP * g + r
            decay = jnp.exp(jnp.where(causal, cum[:, hh:hh + 1] - cum_t[hh:hh + 1, :], -jnp.inf))
            y = y + jnp.dot((cb * decay).astype(BF16), jnp.where(in_head[r], xdt, 0.0).astype(BF16),
                            preferred_element_type=F32)
        xdte = (xdt * _expand_heads(decay_to_end, g, gw)).astype(BF16)
        states = jnp.dot(bm.astype(F32).T.astype(BF16), xdte, preferred_element_type=F32)
        s_ref[g] = _expand_heads(chunk_decay, g, gw) * s_in + states
        y = y + dsk_ref[:, g * gw:(g + 1) * gw] * xs
        y = y * jax.nn.silu(z_ref[:, g * gw:(g + 1) * gw])
        out_ref[:, g * gw:(g + 1) * gw] = _rms(y, ng_ref[:, g * gw:(g + 1) * gw]).astype(out_ref.dtype)

    ext_ref[0:CONV_HALO, :] = ext_ref[L:L + CONV_HALO, :]


def _ssd_core(proj, dt, conv_w, conv_b, dt_bias, a_log, d_skip, norm_g, bsz, seq):
    L = SSD_CHUNK
    nc = seq // L
    rowmap = lambda b, c: (b * nc + c, 0)
    const = lambda b, c: (0, 0)
    return pl.pallas_call(
        _ssd_body,
        grid=(bsz, nc),
        in_specs=[pl.BlockSpec((L, SSD_INNER), rowmap),
                  pl.BlockSpec((L, SSD_INNER), lambda b, c: (b * nc + c, 1)),
                  pl.BlockSpec((L, 2 * SSD_BC), lambda b, c: (b * nc + c, 2)),
                  pl.BlockSpec((L, LANES), rowmap),
                  pl.BlockSpec((SSD_CONV, SSD_CONV_DIM), const),
                  pl.BlockSpec((1, SSD_CONV_DIM), const),
                  pl.BlockSpec((1, LANES), const),
                  pl.BlockSpec((1, LANES), const),
                  pl.BlockSpec((1, SSD_INNER), const),
                  pl.BlockSpec((1, SSD_INNER), const)],
        out_specs=pl.BlockSpec((L, SSD_INNER), rowmap),
        out_shape=jax.ShapeDtypeStruct((bsz * seq, SSD_INNER), BF16),
        scratch_shapes=[pltpu.VMEM((L + CONV_HALO, SSD_CONV_DIM), F32),
                        pltpu.VMEM((SSD_GROUPS, SSD_STATE, SSD_GROUP_WIDTH), F32)],
        compiler_params=_params("parallel", "arbitrary"),
        name="ssd_core",
    )(proj, proj, proj, dt, conv_w, conv_b, dt_bias, a_log, d_skip, norm_g)


def _ssd_layer(h, ln_g, w_in, conv_w, conv_b, dt_bias, a_log, d_skip, norm_g, w_out, bsz, seq):
    w_main = w_in[:, :SSD_MAIN].astype(BF16)
    w_dt = _pad_cols(w_in[:, SSD_MAIN:], LANES).astype(BF16)
    proj, dt = _norm_proj(h, ln_g, w_main, w_dt)
    pad_heads = lambda a: jnp.pad(a.astype(F32), (0, LANES - SSD_HEADS)).reshape(1, LANES)
    mixed = _ssd_core(proj, dt, conv_w.astype(F32), conv_b.astype(F32).reshape(1, SSD_CONV_DIM),
                      pad_heads(dt_bias), pad_heads(a_log),
                      jnp.repeat(d_skip.astype(F32), SSD_HEADDIM).reshape(1, SSD_INNER),
                      norm_g.astype(F32).reshape(1, SSD_INNER), bsz, seq)
    return _out_proj(mixed, w_out.astype(BF16), h)


def kernel(x, ln_mix, ln_mlp, w_up, w_down, ln_f, mlstm_w_in, mlstm_b_gate, mlstm_norm, mlstm_w_out, diff_w_in, diff_lam, diff_norm, diff_w_out, ssd_w_in, ssd_conv_w, ssd_conv_b, ssd_dt_bias, ssd_A_log, ssd_D, ssd_norm, ssd_w_out):
    bsz, seq, d = x.shape
    h = x.reshape(bsz * seq, d)
    for i in range(DEPTH):
        kind, j = i % N_MIXERS, i // N_MIXERS
        if kind == 0:
            h = _mlstm_layer(h, ln_mix[i], mlstm_w_in[j], mlstm_b_gate[j], mlstm_norm[j],
                             mlstm_w_out[j], bsz, seq)
        elif kind == 1:
            h = _diff_layer(h, ln_mix[i], diff_w_in[j], diff_lam[j], diff_norm[j], diff_w_out[j],
                            bsz, seq, i)
        else:
            h = _ssd_layer(h, ln_mix[i], ssd_w_in[j], ssd_conv_w[j], ssd_conv_b[j], ssd_dt_bias[j],
                           ssd_A_log[j], ssd_D[j], ssd_norm[j], ssd_w_out[j], bsz, seq)
        h = _mlp(h, ln_mlp[i], w_up[i].astype(BF16), w_down[i].astype(BF16), ln_f,
                 final_norm=(i == DEPTH - 1))
    return h.reshape(bsz, seq, d)
```

```python
import functools
import math

import jax
import jax.numpy as jnp
from jax import lax
from jax.experimental import pallas as pl
from jax.experimental.pallas import tpu as pltpu

F32 = jnp.float32
BF16 = jnp.bfloat16

DEPTH = 4
N_MIXERS = 3
EPS = 1e-6

MLSTM_HEADS = 4
MLSTM_DV = 256
MLSTM_DQK = 128
MLSTM_CHUNK = 128
GATE_SOFTCAP = 15.0

DIFF_HEADS = 8
DIFF_DH = 64
DIFF_DV = 128
ROPE_THETA = 10000.0

SSD_INNER = 2048
SSD_HEADDIM = 64
SSD_HEADS = 32
SSD_GROUPS = 8
SSD_STATE = 128
SSD_CONV = 4
SSD_CHUNK = 128
SSD_GROUP_WIDTH = SSD_INNER // SSD_GROUPS
SSD_HEADS_PER_GROUP = SSD_HEADS // SSD_GROUPS

LANES = 128
CONV_HALO = 8
VMEM_LIMIT = 48 * 1024 * 1024
VMEM_LIMIT_MLP = 56 * 1024 * 1024

_NT = (((1,), (1,)), ((), ()))


def _params(*sem):
    return pltpu.CompilerParams(dimension_semantics=sem, vmem_limit_bytes=VMEM_LIMIT)


def _rms(x, g):
    ms = jnp.mean(x * x, axis=-1, keepdims=True)
    return x * lax.rsqrt(ms + EPS) * g


def _cumsum_rows(x):
    n = x.shape[0]
    row = lax.broadcasted_iota(jnp.int32, x.shape, 0)
    s = 1
    while s < n:
        x = x + jnp.where(row >= s, pltpu.roll(x, s, 0), 0.0)
        s *= 2
    return x


def _cummax_rows(x):
    n = x.shape[0]
    row = lax.broadcasted_iota(jnp.int32, x.shape, 0)
    s = 1
    while s < n:
        x = jnp.maximum(x, jnp.where(row >= s, pltpu.roll(x, s, 0), -jnp.inf))
        s *= 2
    return x


COL_REDUCE_SLAB = 64


def _col_reduce(x, pair_op, reduce_fn):
    slab = min(COL_REDUCE_SLAB, x.shape[0])
    acc = x[:slab]
    for r0 in range(slab, x.shape[0], slab):
        acc = pair_op(acc, x[r0:r0 + slab])
    while acc.shape[0] > 8:
        half = acc.shape[0] // 2
        acc = pair_op(acc[:half], acc[half:])
    return reduce_fn(acc, axis=0, keepdims=True)


def _sigmoid(x):
    return 0.5 + 0.5 * jnp.tanh(0.5 * x)


def _silu(x):
    half = 0.5 * x
    return half + half * jnp.tanh(half)


def _softplus(x):
    return jnp.maximum(x, 0.0) + jnp.log1p(jnp.exp(-jnp.abs(x)))


def _pad_cols(w, n):
    return jnp.pad(w, ((0, 0), (0, n - w.shape[1])))


def _norm_proj_body(x_ref, g_ref, w_ref, ws_ref, o_ref, os_ref, xn_ref):
    @pl.when(pl.program_id(1) == 0)
    def _():
        xn = _rms(x_ref[...], g_ref[...]).astype(BF16)
        xn_ref[...] = xn
        os_ref[...] = jnp.dot(xn, ws_ref[...], preferred_element_type=F32)

    o_ref[...] = jnp.dot(xn_ref[...], w_ref[...], preferred_element_type=F32).astype(o_ref.dtype)


def _norm_proj(h, g, w, w_side, tn=1024, tm=1024):
    n, d = h.shape
    nout = w.shape[1]
    tm = min(tm, n)
    return pl.pallas_call(
        _norm_proj_body,
        grid=(n // tm, nout // tn),
        in_specs=[pl.BlockSpec((tm, d), lambda i, j: (i, 0)),
                  pl.BlockSpec((1, d), lambda i, j: (0, 0)),
                  pl.BlockSpec((d, tn), lambda i, j: (0, j)),
                  pl.BlockSpec((d, LANES), lambda i, j: (0, 0))],
        out_specs=[pl.BlockSpec((tm, tn), lambda i, j: (i, j)),
                   pl.BlockSpec((tm, LANES), lambda i, j: (i, 0))],
        out_shape=[jax.ShapeDtypeStruct((n, nout), BF16), jax.ShapeDtypeStruct((n, LANES), F32)],
        scratch_shapes=[pltpu.VMEM((tm, d), BF16)],
        compiler_params=_params("parallel", "arbitrary"),
        name="norm_proj",
    )(h, g.reshape(1, d), w, w_side)


def _rope_proj_body(x_ref, g_ref, w_ref, cos_ref, sin_ref, o_ref, xn_ref):
    j = pl.program_id(1)

    @pl.when(j == 0)
    def _():
        xn_ref[...] = _rms(x_ref[...], g_ref[...]).astype(BF16)

    acc = jnp.dot(xn_ref[...], w_ref[...], preferred_element_type=F32)

    @pl.when(j < 2)
    def _():
        cos = cos_ref[...]
        sin = sin_ref[...]
        qscale = jnp.where(j == 0, DIFF_DH ** -0.5 * math.log2(math.e), 1.0).astype(F32)
        for c in range(acc.shape[1] // LANES):
            xc = acc[:, c * LANES:(c + 1) * LANES]
            rot = pltpu.roll(xc, LANES // 2, 1)
            o_ref[:, c * LANES:(c + 1) * LANES] = ((xc * cos + rot * sin) * qscale).astype(o_ref.dtype)

    @pl.when(j >= 2)
    def _():
        o_ref[...] = acc.astype(o_ref.dtype)


def _rope_proj(h, g, w, cos, sin, seq, tm=1024):
    n, d = h.shape
    nout = w.shape[1]
    tm = min(tm, seq)
    tn = d
    nseq = seq // tm
    return pl.pallas_call(
        _rope_proj_body,
        grid=(n // tm, nout // tn),
        in_specs=[pl.BlockSpec((tm, d), lambda i, j: (i, 0)),
                  pl.BlockSpec((1, d), lambda i, j: (0, 0)),
                  pl.BlockSpec((d, tn), lambda i, j: (0, j)),
                  pl.BlockSpec((tm, LANES), lambda i, j: (i % nseq, 0)),
                  pl.BlockSpec((tm, LANES), lambda i, j: (i % nseq, 0))],
        out_specs=pl.BlockSpec((tm, tn), lambda i, j: (i, j)),
        out_shape=jax.ShapeDtypeStruct((n, nout), BF16),
        scratch_shapes=[pltpu.VMEM((tm, d), BF16)],
        compiler_params=_params("parallel", "arbitrary"),
        name="rope_proj",
    )(h, g.reshape(1, d), w, cos, sin)


def _mix_mlp_body(a_ref, wo_ref, h_ref, g_ref, wu_ref, wd_ref, gf_ref, o_ref, xn_ref, *, final_norm):
    j = pl.program_id(1)

    @pl.when(j == 0)
    def _():
        h1 = h_ref[...] + jnp.dot(a_ref[...], wo_ref[...], preferred_element_type=F32)
        o_ref[...] = h1
        xn_ref[...] = _rms(h1, g_ref[...]).astype(BF16)

    up = jnp.dot(xn_ref[...], wu_ref[...], preferred_element_type=F32)
    up = jnp.square(jnp.maximum(up, 0.0)).astype(BF16)
    o_ref[...] += jnp.dot(up, wd_ref[...], preferred_element_type=F32)

    if final_norm:
        @pl.when(j == pl.num_programs(1) - 1)
        def _():
            o_ref[...] = _rms(o_ref[...], gf_ref[...])


def _mix_mlp(a, w_out, h, g, w_up, w_down, g_final, final_norm, tm=1024, tf=1024):
    n, d = h.shape
    k = a.shape[1]
    dff = w_up.shape[1]
    tm = min(tm, n)
    row = lambda i, j: (i, 0)
    const = lambda i, j: (0, 0)
    return pl.pallas_call(
        functools.partial(_mix_mlp_body, final_norm=final_norm),
        grid=(n // tm, dff // tf),
        in_specs=[pl.BlockSpec((tm, k), row),
                  pl.BlockSpec((k, d), const),
                  pl.BlockSpec((tm, d), row),
                  pl.BlockSpec((1, d), const),
                  pl.BlockSpec((d, tf), lambda i, j: (0, j)),
                  pl.BlockSpec((tf, d), lambda i, j: (j, 0)),
                  pl.BlockSpec((1, d), const)],
        out_specs=pl.BlockSpec((tm, d), row),
        out_shape=jax.ShapeDtypeStruct((n, d), F32),
        scratch_shapes=[pltpu.VMEM((tm, d), BF16)],
        compiler_params=pltpu.CompilerParams(dimension_semantics=("parallel", "arbitrary"),
                                             vmem_limit_bytes=VMEM_LIMIT_MLP),
        name="mix_mlp",
    )(a, w_out, h, g.reshape(1, d), w_up, w_down, g_final.reshape(1, d))


MLSTM_QK = MLSTM_HEADS * MLSTM_DQK
MLSTM_MAIN = 2 * MLSTM_QK + 2 * MLSTM_HEADS * MLSTM_DV


def _mlstm_body(p_ref, gate_ref, bias_ref, ng_ref, out_ref, *state_refs):
    L = MLSTM_CHUNK
    nh = MLSTM_HEADS
    n_batch = p_ref.shape[0]
    n_chain = n_batch * nh
    c_refs = state_refs[:n_chain]
    n_refs = state_refs[n_chain:2 * n_chain]
    m_refs = state_refs[2 * n_chain:]

    @pl.when(pl.program_id(0) == 0)
    def _():
        for ref in state_refs:
            ref[...] = jnp.zeros_like(ref)

    row = lax.broadcasted_iota(jnp.int32, (L, L), 0)
    col = lax.broadcasted_iota(jnp.int32, (L, L), 1)
    causal = col <= row
    qscale = MLSTM_DQK ** -0.5

    for bi in range(n_batch):
        gates = gate_ref[bi] + bias_ref[...]
        capped = GATE_SOFTCAP * jnp.tanh(gates / GATE_SOFTCAP)
        log_f = -_softplus(-capped)
        log_i = pltpu.roll(capped, nh, 1)
        b = _cumsum_rows(log_f)
        g = log_i - b
        g_max = _cummax_rows(g)
        m_in = m_refs[bi][...]
        m_run = jnp.maximum(m_in, g_max)
        m_t = b + m_run
        scale_all = jnp.exp(m_in - m_run)
        floor_all = jnp.exp(-m_t)
        b_last = b[L - 1:L, :]
        g_max_last = g_max[L - 1:L, :]
        m_run_last = m_run[L - 1:L, :]
        w_all = jnp.exp(g - g_max_last)
        a_old_all = jnp.exp(m_in - m_run_last)
        a_loc_all = jnp.exp(g_max_last - m_run_last)
        m_refs[bi][...] = b_last + m_run_last
        g_t = g.T

        for h in range(nh):
            st = bi * nh + h
            ln = nh + h
            v_off = 2 * MLSTM_QK + h * MLSTM_DV
            o_off = v_off + nh * MLSTM_DV
            q = p_ref[bi, :, h * MLSTM_DQK:(h + 1) * MLSTM_DQK]
            k = p_ref[bi, :, MLSTM_QK + h * MLSTM_DQK:MLSTM_QK + (h + 1) * MLSTM_DQK]
            v = p_ref[bi, :, v_off:v_off + MLSTM_DV]
            c_in = c_refs[st][...]
            n_in = n_refs[st][...]

            decay = jnp.exp(jnp.where(causal, g_t[ln:ln + 1, :] - m_run[:, ln:ln + 1], -jnp.inf))
            qk = lax.dot_general(q, k, _NT, preferred_element_type=F32) * qscale
            p = decay * qk
            scale = scale_all[:, ln:ln + 1]
            q_c = jnp.dot(q, c_in.astype(BF16), preferred_element_type=F32) * qscale
            num = jnp.dot(p.astype(BF16), v, preferred_element_type=F32) + scale * q_c
            q_n = jnp.sum(q.astype(F32) * n_in, axis=1, keepdims=True) * qscale
            den = jnp.sum(p, axis=1, keepdims=True) + scale * q_n
            hs = num / jnp.maximum(jnp.abs(den), floor_all[:, ln:ln + 1])

            kw = k.astype(F32) * w_all[:, ln:ln + 1]
            c_loc = jnp.dot(kw.T.astype(BF16), v, preferred_element_type=F32)
            n_loc = jnp.sum(kw, axis=0, keepdims=True)
            a_old = a_old_all[:, ln:ln + 1]
            a_loc = a_loc_all[:, ln:ln + 1]
            c_refs[st][...] = a_old * c_in + a_loc * c_loc
            n_refs[st][...] = a_old * n_in + a_loc * n_loc

            hn = _rms(hs, ng_ref[:, h * MLSTM_DV:(h + 1) * MLSTM_DV])
            o_gate = _sigmoid(p_ref[bi, :, o_off:o_off + MLSTM_DV].astype(F32))
            out_ref[bi, :, h * MLSTM_DV:(h + 1) * MLSTM_DV] = (o_gate * hn).astype(out_ref.dtype)


def _mlstm_core(proj, gates, bias, norm_g, bsz, seq):
    L = MLSTM_CHUNK
    d = MLSTM_HEADS * MLSTM_DV
    n_chain = bsz * MLSTM_HEADS
    chunk = lambda c: (0, c, 0)
    return pl.pallas_call(
        _mlstm_body,
        grid=(seq // L,),
        in_specs=[pl.BlockSpec((bsz, L, MLSTM_MAIN), chunk),
                  pl.BlockSpec((bsz, L, LANES), chunk),
                  pl.BlockSpec((1, LANES), lambda c: (0, 0)),
                  pl.BlockSpec((1, d), lambda c: (0, 0))],
        out_specs=pl.BlockSpec((bsz, L, d), chunk),
        out_shape=jax.ShapeDtypeStruct((bsz, seq, d), BF16),
        scratch_shapes=([pltpu.VMEM((MLSTM_DQK, MLSTM_DV), F32)] * n_chain
                        + [pltpu.VMEM((1, MLSTM_DQK), F32)] * n_chain
                        + [pltpu.VMEM((1, LANES), F32)] * bsz),
        compiler_params=_params("arbitrary"),
        name="mlstm_core",
    )(proj.reshape(bsz, seq, MLSTM_MAIN), gates.reshape(bsz, seq, LANES), bias, norm_g.reshape(1, d))


def _mlstm_layer(h, ln_g, w_in, b_gate, norm_g, bsz, seq):
    w_main = w_in[:, :MLSTM_MAIN].astype(BF16)
    w_gate = _pad_cols(w_in[:, MLSTM_MAIN:], LANES).astype(BF16)
    bias = jnp.pad(b_gate.astype(F32), (0, LANES - b_gate.shape[0])).reshape(1, LANES)
    proj, gates = _norm_proj(h, ln_g, w_main, w_gate)
    mixed = _mlstm_core(proj, gates, bias, norm_g, bsz, seq)
    return mixed.reshape(bsz * seq, -1)


def _diff_attn_body(lam_ref, g_ref, q_ref, k_ref, vt_ref, o_ref, sc_a, sc_b, m_ref, l_ref, acc_ref, *,
                    tq, lam_init):
    qi = pl.program_id(2)
    q = q_ref[...]
    lane = lax.broadcasted_iota(jnp.int32, q.shape, 1)
    zero = jnp.zeros_like(q)
    is_c1 = (lane & (DIFF_DH // 2)) != 0
    q_parts = (jnp.where(is_c1, zero, q), jnp.where(is_c1, q, zero))
    th = tq // 2
    chains = [(c, half) for c in range(2) for half in range(2)]
    q_ch = [q_parts[c][half * th:(half + 1) * th] for c, half in chains]
    kpos = lax.broadcasted_iota(jnp.int32, (tq, th), 0)
    qpos = lax.broadcasted_iota(jnp.int32, (tq, th), 1)
    causal = [kpos <= qpos + half * th for half in range(2)]

    def put_scores(sc_ref, ki):
        k = k_ref[pl.ds(pl.multiple_of(ki * tq, tq), tq), :]
        for x in range(4):
            sc_ref[x] = lax.dot_general(k, q_ch[x], _NT, preferred_element_type=F32)

    def softmax_pv(ki, sc_ref, masked):
        vt = vt_ref[:, pl.ds(pl.multiple_of(ki * tq, tq), tq)]
        for x, (c, half) in enumerate(chains):
            def load_scores():
                s = sc_ref[x]
                return jnp.where(causal[half], s, -jnp.inf) if masked else s

            m = m_ref[x]
            m_new = jnp.maximum(m, _col_reduce(load_scores(), jnp.maximum, jnp.max))
            alpha = jnp.exp2(m - m_new)
            p = jnp.exp2(load_scores() - m_new)
            m_ref[x] = m_new
            l_ref[x] = alpha * l_ref[x] + _col_reduce(p, jnp.add, jnp.sum)
            acc_ref[x] = alpha * acc_ref[x] + jnp.dot(vt, p.astype(BF16), preferred_element_type=F32)

    m_ref[...] = jnp.full(m_ref.shape, -jnp.inf, F32)
    l_ref[...] = jnp.zeros_like(l_ref)
    acc_ref[...] = jnp.zeros_like(acc_ref)

    odd = qi & 1

    @pl.when(odd == 0)
    def _():
        put_scores(sc_a, 0)

    @pl.when(odd == 1)
    def _():
        put_scores(sc_b, 0)
        put_scores(sc_a, 1)
        softmax_pv(0, sc_b, False)

    def pair(j, carry):
        b0 = odd + 2 * j
        put_scores(sc_b, b0 + 1)
        softmax_pv(b0, sc_a, False)
        put_scores(sc_a, b0 + 2)
        softmax_pv(b0 + 1, sc_b, False)
        return carry

    lax.fori_loop(0, lax.shift_right_logical(qi, 1), pair, 0)
    softmax_pv(qi, sc_a, True)
    a0, a1 = (jnp.concatenate([acc_ref[2 * c] / l_ref[2 * c], acc_ref[2 * c + 1] / l_ref[2 * c + 1]], axis=1)
              for c in range(2))

    lam = lam_ref[...]
    lam_val = (jnp.exp(jnp.sum(lam[0:1] * lam[1:2], axis=1, keepdims=True))
               - jnp.exp(jnp.sum(lam[2:3] * lam[3:4], axis=1, keepdims=True)) + lam_init)
    o = a0 - lam_val * a1
    ms = jnp.mean(o * o, axis=0, keepdims=True)
    y = o * lax.rsqrt(ms + EPS) * g_ref[...] * (1.0 - lam_init)
    o_ref[...] = y.T.astype(o_ref.dtype)


def _diff_attn(qkv, lam, norm_g, bsz, seq, layer_idx, tq=512):
    tq = min(tq, seq)
    nq = seq // tq
    nh = DIFF_HEADS
    lam_init = 0.8 - 0.6 * math.exp(-0.3 * layer_idx)
    v_t = qkv[:, 2 * nh * DIFF_DV:].reshape(bsz, seq, nh, DIFF_DV).transpose(0, 2, 3, 1)
    v_t = v_t.reshape(bsz * nh * DIFF_DV, seq)
    return pl.pallas_call(
        functools.partial(_diff_attn_body, tq=tq, lam_init=lam_init),
        grid=(bsz, nh, nq),
        in_specs=[pl.BlockSpec((4, DIFF_DH), lambda b, h, i: (0, 0)),
                  pl.BlockSpec((DIFF_DV, 1), lambda b, h, i: (0, 0)),
                  pl.BlockSpec((tq, DIFF_DV), lambda b, h, i: (b * nq + i, h)),
                  pl.BlockSpec((seq, DIFF_DV), lambda b, h, i: (b, nh + h)),
                  pl.BlockSpec((DIFF_DV, seq), lambda b, h, i: (b * nh + h, 0))],
        out_specs=pl.BlockSpec((tq, DIFF_DV), lambda b, h, i: (b * nq + i, h)),
        out_shape=jax.ShapeDtypeStruct((bsz * seq, nh * DIFF_DV), BF16),
        scratch_shapes=[pltpu.VMEM((4, tq, tq // 2), F32), pltpu.VMEM((4, tq, tq // 2), F32),
                        pltpu.VMEM((4, 1, tq // 2), F32), pltpu.VMEM((4, 1, tq // 2), F32),
                        pltpu.VMEM((4, DIFF_DV, tq // 2), F32)],
        compiler_params=_params("parallel", "parallel", "arbitrary"),
        name="diff_attn",
    )(lam.astype(F32), norm_g.reshape(DIFF_DV, 1).astype(F32), qkv, qkv, v_t)


def _rope_tables(seq):
    inv = ROPE_THETA ** (-jnp.arange(0, DIFF_DH, 2, dtype=F32) / DIFF_DH)
    ang = jnp.arange(seq, dtype=F32)[:, None] * inv[None, :]
    cos, sin = jnp.cos(ang), jnp.sin(ang)
    cos_t = jnp.tile(cos, (1, LANES // cos.shape[1]))
    sin_t = jnp.concatenate([-sin, -sin, sin, sin], axis=1)
    return cos_t, sin_t


def _rope_lane_order():
    half = DIFF_DH // 2
    return [c * DIFF_DH + part * half + i for part in range(2) for c in range(2) for i in range(half)]


def _diff_layer(h, ln_g, w_in, lam, norm_g, bsz, seq, layer_idx):
    cos, sin = _rope_tables(seq)
    order = jnp.asarray(_rope_lane_order(), jnp.int32)
    nqk = 2 * DIFF_HEADS * DIFF_DV
    qk_cols = (jnp.arange(nqk // LANES, dtype=jnp.int32) * LANES)[:, None] + order[None, :]
    w_in = jnp.concatenate([w_in[:, qk_cols.reshape(-1)], w_in[:, nqk:]], axis=1)
    qkv = _rope_proj(h, ln_g, w_in.astype(BF16), cos, sin, seq)
    return _diff_attn(qkv, lam, norm_g, bsz, seq, layer_idx)


SSD_BC = SSD_GROUPS * SSD_STATE
SSD_CONV_DIM = SSD_INNER + 2 * SSD_BC
SSD_MAIN = SSD_INNER + SSD_CONV_DIM


def _expand_heads(a, g, width):
    nr = SSD_HEADS_PER_GROUP
    lane = lax.broadcasted_iota(jnp.int32, (a.shape[0], width), 1)
    out = jnp.broadcast_to(a[:, nr * g + nr - 1:nr * g + nr], (a.shape[0], width))
    for r in range(nr - 2, -1, -1):
        out = jnp.where(lane < SSD_HEADDIM * (r + 1), a[:, nr * g + r:nr * g + r + 1], out)
    return out


def _ssd_body(z_ref, x_ref, bc_ref, dt_ref, cw_ref, cb_ref, dtb_ref, alog_ref, dsk_ref, ng_ref, e3_ref,
              out_ref, ext_ref, s_ref):
    L = SSD_CHUNK
    gw = SSD_GROUP_WIDTH
    ns = SSD_STATE

    @pl.when(pl.program_id(1) == 0)
    def _():
        ext_ref[0:CONV_HALO, :] = jnp.zeros((CONV_HALO, SSD_CONV_DIM), F32)
        s_ref[...] = jnp.zeros_like(s_ref)

    ext_ref[CONV_HALO:CONV_HALO + L, 0:SSD_INNER] = x_ref[...].astype(F32)
    ext_ref[CONV_HALO:CONV_HALO + L, SSD_INNER:SSD_CONV_DIM] = bc_ref[...].astype(F32)

    def conv_silu(lo, width):
        ext = ext_ref[:, lo:lo + width]
        acc = cb_ref[:, lo:lo + width] + cw_ref[SSD_CONV - 1:SSD_CONV, lo:lo + width] * ext[CONV_HALO:]
        for j in range(SSD_CONV - 1):
            shifted = pltpu.roll(ext, SSD_CONV - 1 - j, 0)[CONV_HALO:]
            acc = acc + cw_ref[j:j + 1, lo:lo + width] * shifted
        return _silu(acc)

    dt = _softplus(dt_ref[...] + dtb_ref[...])
    cum = _cumsum_rows(dt * (-jnp.exp(alog_ref[...])))
    cum_t = cum.T
    cum_last = cum[L - 1:L, :]
    exp_cum = jnp.exp(cum)
    decay_to_end = jnp.exp(cum_last - cum)
    chunk_decay = jnp.exp(cum_last)
    row = lax.broadcasted_iota(jnp.int32, (L, L), 0)
    col = lax.broadcasted_iota(jnp.int32, (L, L), 1)
    causal = col <= row
    lane_gw = lax.broadcasted_iota(jnp.int32, (1, gw), 1)
    head_rows = [jnp.where(lane_gw // SSD_HEADDIM == r, 1.0, 0.0).astype(BF16)
                 for r in range(SSD_HEADS_PER_GROUP)]

    def expand(a):
        hi = a.astype(BF16)
        rest = a - hi.astype(F32)
        mid = rest.astype(BF16)
        lo = (rest - mid.astype(F32)).astype(BF16)
        return jnp.dot(jnp.concatenate([hi, mid, lo], axis=1), e3_ref[...], preferred_element_type=F32)

    dt_x = expand(dt)
    exp_cum_x = expand(exp_cum)
    decay_to_end_x = expand(decay_to_end)

    for g in range(SSD_GROUPS):
        cols = slice(g * gw, (g + 1) * gw)
        xs = conv_silu(g * gw, gw)
        bm = conv_silu(SSD_INNER + g * ns, ns).astype(BF16)
        cm = conv_silu(SSD_INNER + SSD_BC + g * ns, ns).astype(BF16)
        xdt = xs * dt_x[:, cols]
        s_in = s_ref[g]
        cb = lax.dot_general(cm, bm, _NT, preferred_element_type=F32)
        y = exp_cum_x[:, cols] * jnp.dot(cm, s_in.astype(BF16), preferred_element_type=F32)
        weights, inputs = [], []
        xdt_lo = xdt.astype(BF16)
        for r in range(SSD_HEADS_PER_GROUP):
            hh = SSD_HEADS_PER_GROUP * g + r
            decay = jnp.exp(jnp.where(causal, cum[:, hh:hh + 1] - cum_t[hh:hh + 1, :], -jnp.inf))
            weights.append((cb * decay).astype(BF16))
            inputs.append(xdt_lo * head_rows[r])
        y = y + jnp.dot(jnp.concatenate(weights, axis=1), jnp.concatenate(inputs, axis=0),
                        preferred_element_type=F32)
        xdte = (xdt * decay_to_end_x[:, cols]).astype(BF16)
        states = jnp.dot(bm.astype(F32).T.astype(BF16), xdte, preferred_element_type=F32)
        s_ref[g] = _expand_heads(chunk_decay, g, gw) * s_in + states
        y = y + dsk_ref[:, g * gw:(g + 1) * gw] * xs
        y = y * _silu(z_ref[:, g * gw:(g + 1) * gw].astype(F32))
        out_ref[:, g * gw:(g + 1) * gw] = _rms(y, ng_ref[:, g * gw:(g + 1) * gw]).astype(out_ref.dtype)

    ext_ref[0:CONV_HALO, :] = ext_ref[L:L + CONV_HALO, :]


def _ssd_core(proj, dt, conv_w, conv_b, dt_bias, a_log, d_skip, norm_g, bsz, seq):
    L = SSD_CHUNK
    nc = seq // L
    rowmap = lambda b, c: (b * nc + c, 0)
    const = lambda b, c: (0, 0)
    head_of_channel = jnp.arange(SSD_INNER, dtype=jnp.int32) // SSD_HEADDIM
    expand1 = (jnp.arange(LANES, dtype=jnp.int32)[:, None] == head_of_channel[None, :]).astype(BF16)
    expand3 = jnp.concatenate([expand1] * 3, axis=0)
    return pl.pallas_call(
        _ssd_body,
        grid=(bsz, nc),
        in_specs=[pl.BlockSpec((L, SSD_INNER), rowmap),
                  pl.BlockSpec((L, SSD_INNER), lambda b, c: (b * nc + c, 1)),
                  pl.BlockSpec((L, 2 * SSD_BC), lambda b, c: (b * nc + c, 2)),
                  pl.BlockSpec((L, LANES), rowmap),
                  pl.BlockSpec((SSD_CONV, SSD_CONV_DIM), const),
                  pl.BlockSpec((1, SSD_CONV_DIM), const),
                  pl.BlockSpec((1, LANES), const),
                  pl.BlockSpec((1, LANES), const),
                  pl.BlockSpec((1, SSD_INNER), const),
                  pl.BlockSpec((1, SSD_INNER), const),
                  pl.BlockSpec((3 * LANES, SSD_INNER), const)],
        out_specs=pl.BlockSpec((L, SSD_INNER), rowmap),
        out_shape=jax.ShapeDtypeStruct((bsz * seq, SSD_INNER), BF16),
        scratch_shapes=[pltpu.VMEM((L + CONV_HALO, SSD_CONV_DIM), F32),
                        pltpu.VMEM((SSD_GROUPS, SSD_STATE, SSD_GROUP_WIDTH), F32)],
        compiler_params=_params("parallel", "arbitrary"),
        name="ssd_core",
    )(proj, proj, proj, dt, conv_w, conv_b, dt_bias, a_log, d_skip, norm_g, expand3)


def _ssd_layer(h, ln_g, w_in, conv_w, conv_b, dt_bias, a_log, d_skip, norm_g, bsz, seq):
    w_main = w_in[:, :SSD_MAIN].astype(BF16)
    w_dt = _pad_cols(w_in[:, SSD_MAIN:], LANES).astype(BF16)
    proj, dt = _norm_proj(h, ln_g, w_main, w_dt)
    pad_heads = lambda a: jnp.pad(a.astype(F32), (0, LANES - SSD_HEADS)).reshape(1, LANES)
    mixed = _ssd_core(proj, dt, conv_w.astype(F32), conv_b.astype(F32).reshape(1, SSD_CONV_DIM),
                      pad_heads(dt_bias), pad_heads(a_log),
                      jnp.repeat(d_skip.astype(F32), SSD_HEADDIM).reshape(1, SSD_INNER),
                      norm_g.astype(F32).reshape(1, SSD_INNER), bsz, seq)
    return mixed


def kernel(x, ln_mix, ln_mlp, w_up, w_down, ln_f, mlstm_w_in, mlstm_b_gate, mlstm_norm, mlstm_w_out, diff_w_in, diff_lam, diff_norm, diff_w_out, ssd_w_in, ssd_conv_w, ssd_conv_b, ssd_dt_bias, ssd_A_log, ssd_D, ssd_norm, ssd_w_out):
    bsz, seq, d = x.shape
    h = x.reshape(bsz * seq, d)
    for i in range(DEPTH):
        kind, j = i % N_MIXERS, i // N_MIXERS
        if kind == 0:
            mixed = _mlstm_layer(h, ln_mix[i], mlstm_w_in[j], mlstm_b_gate[j], mlstm_norm[j], bsz, seq)
            w_out = mlstm_w_out[j]
        elif kind == 1:
            mixed = _diff_layer(h, ln_mix[i], diff_w_in[j], diff_lam[j], diff_norm[j], bsz, seq, i)
            w_out = diff_w_out[j]
        else:
            mixed = _ssd_layer(h, ln_mix[i], ssd_w_in[j], ssd_conv_w[j], ssd_conv_b[j], ssd_dt_bias[j],
                               ssd_A_log[j], ssd_D[j], ssd_norm[j], bsz, seq)
            w_out = ssd_w_out[j]
        h = _mix_mlp(mixed, w_out.astype(BF16), h, ln_mlp[i], w_up[i].astype(BF16),
                     w_down[i].astype(BF16), ln_f, final_norm=(i == DEPTH - 1))
    return h.reshape(bsz, seq, d)
```

```python
import functools
import math

import jax
import jax.numpy as jnp
from jax import lax
from jax.experimental import pallas as pl
from jax.experimental.pallas import tpu as pltpu

F32 = jnp.float32
BF16 = jnp.bfloat16

DEPTH = 4
N_MIXERS = 3
EPS = 1e-6

MLSTM_HEADS = 4
MLSTM_DV = 256
MLSTM_DQK = 128
MLSTM_CHUNK = 128
GATE_SOFTCAP = 15.0

DIFF_HEADS = 8
DIFF_DH = 64
DIFF_DV = 128
ROPE_THETA = 10000.0

SSD_INNER = 2048
SSD_HEADDIM = 64
SSD_HEADS = 32
SSD_GROUPS = 8
SSD_STATE = 128
SSD_CONV = 4
SSD_CHUNK = 128
SSD_GROUP_WIDTH = SSD_INNER // SSD_GROUPS
SSD_HEADS_PER_GROUP = SSD_HEADS // SSD_GROUPS

LANES = 128
CONV_HALO = 8
VMEM_LIMIT = 48 * 1024 * 1024
VMEM_LIMIT_MLP = 56 * 1024 * 1024

_NT = (((1,), (1,)), ((), ()))


def _params(*sem):
    return pltpu.CompilerParams(dimension_semantics=sem, vmem_limit_bytes=VMEM_LIMIT)


def _rms(x, g):
    ms = jnp.mean(x * x, axis=-1, keepdims=True)
    return x * lax.rsqrt(ms + EPS) * g


def _cumsum_rows(x):
    n = x.shape[0]
    row = lax.broadcasted_iota(jnp.int32, x.shape, 0)
    s = 1
    while s < n:
        x = x + jnp.where(row >= s, pltpu.roll(x, s, 0), 0.0)
        s *= 2
    return x


def _cummax_rows(x):
    n = x.shape[0]
    row = lax.broadcasted_iota(jnp.int32, x.shape, 0)
    s = 1
    while s < n:
        x = jnp.maximum(x, jnp.where(row >= s, pltpu.roll(x, s, 0), -jnp.inf))
        s *= 2
    return x


COL_REDUCE_SLAB = 64


def _col_reduce(x, pair_op, reduce_fn):
    slab = min(COL_REDUCE_SLAB, x.shape[0])
    acc = x[:slab]
    for r0 in range(slab, x.shape[0], slab):
        acc = pair_op(acc, x[r0:r0 + slab])
    while acc.shape[0] > 8:
        half = acc.shape[0] // 2
        acc = pair_op(acc[:half], acc[half:])
    return reduce_fn(acc, axis=0, keepdims=True)


def _sigmoid(x):
    return 0.5 + 0.5 * jnp.tanh(0.5 * x)


def _silu(x):
    half = 0.5 * x
    return half + half * jnp.tanh(half)


def _softplus(x):
    return jnp.maximum(x, 0.0) + jnp.log1p(jnp.exp(-jnp.abs(x)))


def _pad_cols(w, n):
    return jnp.pad(w, ((0, 0), (0, n - w.shape[1])))


def _norm_proj_body(x_ref, g_ref, w_ref, ws_ref, o_ref, os_ref, *scratch):
    first = pl.program_id(1) == 0
    if scratch:
        xn_ref, = scratch

        @pl.when(first)
        def _():
            xn_ref[...] = _rms(x_ref[...], g_ref[...]).astype(BF16)
    else:
        xn_ref = x_ref

    @pl.when(first)
    def _():
        os_ref[...] = jnp.dot(xn_ref[...], ws_ref[...], preferred_element_type=F32)

    o_ref[...] = jnp.dot(xn_ref[...], w_ref[...], preferred_element_type=F32).astype(o_ref.dtype)


def _norm_proj(x, g, w, w_side, n_main, prenormed, tn=1024, tm=1024):
    n, d = x.shape
    tm = min(tm, n)
    return pl.pallas_call(
        _norm_proj_body,
        grid=(n // tm, n_main // tn),
        in_specs=[pl.BlockSpec((tm, d), lambda i, j: (i, 0)),
                  pl.BlockSpec((1, d), lambda i, j: (0, 0)),
                  pl.BlockSpec((d, tn), lambda i, j: (0, j)),
                  pl.BlockSpec((d, LANES), lambda i, j: (0, 0))],
        out_specs=[pl.BlockSpec((tm, tn), lambda i, j: (i, j)),
                   pl.BlockSpec((tm, LANES), lambda i, j: (i, 0))],
        out_shape=[jax.ShapeDtypeStruct((n, n_main), BF16), jax.ShapeDtypeStruct((n, LANES), F32)],
        scratch_shapes=[] if prenormed else [pltpu.VMEM((tm, d), BF16)],
        compiler_params=_params("parallel", "arbitrary"),
        name="norm_proj",
    )(x, g.reshape(1, d), w, w_side)


ROPE_ROW_SPLIT = 4


def _rope_proj_body(xn_ref, w_ref, cos_ref, sin_ref, o_ref):
    j = pl.program_id(1)

    @pl.when(j < 2)
    def _():
        qscale = jnp.where(j == 0, DIFF_DH ** -0.5 * math.log2(math.e), 1.0).astype(F32)
        rows = xn_ref.shape[0] // ROPE_ROW_SPLIT
        for r in range(ROPE_ROW_SPLIT):
            slab = slice(r * rows, (r + 1) * rows)
            acc = jnp.dot(xn_ref[slab, :], w_ref[...], preferred_element_type=F32)
            cos = cos_ref[slab, :] * qscale
            sin = sin_ref[slab, :] * qscale
            for c in range(acc.shape[1] // LANES):
                xc = acc[:, c * LANES:(c + 1) * LANES]
                rot = pltpu.roll(xc, LANES // 2, 1)
                o_ref[slab, c * LANES:(c + 1) * LANES] = (xc * cos + rot * sin).astype(o_ref.dtype)

    @pl.when(j >= 2)
    def _():
        o_ref[...] = jnp.dot(xn_ref[...], w_ref[...], preferred_element_type=F32).astype(o_ref.dtype)


def _rope_proj(xn, w, cos, sin, seq, tm=1024):
    n, d = xn.shape
    nout = w.shape[1]
    tm = min(tm, seq)
    tn = d
    nseq = seq // tm
    return pl.pallas_call(
        _rope_proj_body,
        grid=(n // tm, nout // tn),
        in_specs=[pl.BlockSpec((tm, d), lambda i, j: (i, 0)),
                  pl.BlockSpec((d, tn), lambda i, j: (0, j)),
                  pl.BlockSpec((tm, LANES), lambda i, j: (i % nseq, 0)),
                  pl.BlockSpec((tm, LANES), lambda i, j: (i % nseq, 0))],
        out_specs=pl.BlockSpec((tm, tn), lambda i, j: (i, j)),
        out_shape=jax.ShapeDtypeStruct((n, nout), BF16),
        compiler_params=_params("parallel", "arbitrary"),
        name="rope_proj",
    )(xn, w, cos, sin)


def _mix_mlp_body(a_ref, wo_ref, h_ref, g_ref, wu_ref, wd_ref, gn_ref, o_ref, *rest, final_norm):
    xn_ref = rest[-1]
    j = pl.program_id(1)

    @pl.when(j == 0)
    def _():
        h1 = h_ref[...] + jnp.dot(a_ref[...], wo_ref[...], preferred_element_type=F32)
        o_ref[...] = h1
        xn_ref[...] = _rms(h1, g_ref[...]).astype(BF16)

    up = jnp.dot(xn_ref[...], wu_ref[...], preferred_element_type=F32)
    up = jnp.square(jnp.maximum(up, 0.0)).astype(BF16)
    o_ref[...] += jnp.dot(up, wd_ref[...], preferred_element_type=F32)

    @pl.when(j == pl.num_programs(1) - 1)
    def _():
        normed = _rms(o_ref[...], gn_ref[...])
        if final_norm:
            o_ref[...] = normed
        else:
            rest[0][...] = normed.astype(BF16)


def _mix_mlp(a, w_out, h, g, w_up, w_down, g_next, final_norm, tm=1024, tf=1024):
    n, d = h.shape
    k = a.shape[1]
    dff = w_up.shape[1]
    tm = min(tm, n)
    row = lambda i, j: (i, 0)
    const = lambda i, j: (0, 0)
    return pl.pallas_call(
        functools.partial(_mix_mlp_body, final_norm=final_norm),
        grid=(n // tm, dff // tf),
        in_specs=[pl.BlockSpec((tm, k), row),
                  pl.BlockSpec((k, d), const),
                  pl.BlockSpec((tm, d), row),
                  pl.BlockSpec((1, d), const),
                  pl.BlockSpec((d, tf), lambda i, j: (0, j)),
                  pl.BlockSpec((tf, d), lambda i, j: (j, 0)),
                  pl.BlockSpec((1, d), const)],
        out_specs=pl.BlockSpec((tm, d), row) if final_norm else [pl.BlockSpec((tm, d), row)] * 2,
        out_shape=(jax.ShapeDtypeStruct((n, d), F32) if final_norm else
                   [jax.ShapeDtypeStruct((n, d), F32), jax.ShapeDtypeStruct((n, d), BF16)]),
        scratch_shapes=[pltpu.VMEM((tm, d), BF16)],
        compiler_params=pltpu.CompilerParams(dimension_semantics=("parallel", "arbitrary"),
                                             vmem_limit_bytes=VMEM_LIMIT_MLP),
        name="mix_mlp",
    )(a, w_out, h, g.reshape(1, d), w_up, w_down, g_next.reshape(1, d))


MLSTM_QK = MLSTM_HEADS * MLSTM_DQK
MLSTM_MAIN = 2 * MLSTM_QK + 2 * MLSTM_HEADS * MLSTM_DV


def _mlstm_body(p_ref, gate_ref, bias_ref, ng_ref, out_ref, *state_refs):
    L = MLSTM_CHUNK
    nh = MLSTM_HEADS
    n_batch = p_ref.shape[0]
    n_chain = n_batch * nh
    c_refs = state_refs[:n_chain]
    n_refs = state_refs[n_chain:2 * n_chain]
    m_refs = state_refs[2 * n_chain:]

    @pl.when(pl.program_id(0) == 0)
    def _():
        for ref in state_refs:
            ref[...] = jnp.zeros_like(ref)

    row = lax.broadcasted_iota(jnp.int32, (L, L), 0)
    col = lax.broadcasted_iota(jnp.int32, (L, L), 1)
    causal = col <= row
    qscale = MLSTM_DQK ** -0.5

    for bi in range(n_batch):
        gates = gate_ref[bi] + bias_ref[...]
        capped = GATE_SOFTCAP * jnp.tanh(gates / GATE_SOFTCAP)
        log_f = -_softplus(-capped)
        log_i = pltpu.roll(capped, nh, 1)
        b = _cumsum_rows(log_f)
        g = log_i - b
        g_max = _cummax_rows(g)
        m_in = m_refs[bi][...]
        m_run = jnp.maximum(m_in, g_max)
        m_t = b + m_run
        scale_all = jnp.exp(m_in - m_run)
        floor_all = jnp.exp(-m_t)
        b_last = b[L - 1:L, :]
        g_max_last = g_max[L - 1:L, :]
        m_run_last = m_run[L - 1:L, :]
        w_all = jnp.exp(g - g_max_last)
        a_old_all = jnp.exp(m_in - m_run_last)
        a_loc_all = jnp.exp(g_max_last - m_run_last)
        m_refs[bi][...] = b_last + m_run_last
        g_t = g.T

        for h in range(nh):
            st = bi * nh + h
            ln = nh + h
            v_off = 2 * MLSTM_QK + h * MLSTM_DV
            o_off = v_off + nh * MLSTM_DV
            q = p_ref[bi, :, h * MLSTM_DQK:(h + 1) * MLSTM_DQK]
            k = p_ref[bi, :, MLSTM_QK + h * MLSTM_DQK:MLSTM_QK + (h + 1) * MLSTM_DQK]
            v = p_ref[bi, :, v_off:v_off + MLSTM_DV]
            c_in = c_refs[st][...]
            n_in = n_refs[st][...]

            decay = jnp.exp(jnp.where(causal, g_t[ln:ln + 1, :] - m_run[:, ln:ln + 1], -jnp.inf))
            qk = lax.dot_general(q, k, _NT, preferred_element_type=F32) * qscale
            p = decay * qk
            scale = scale_all[:, ln:ln + 1]
            q_c = jnp.dot(q, c_in.astype(BF16), preferred_element_type=F32) * qscale
            num = jnp.dot(p.astype(BF16), v, preferred_element_type=F32) + scale * q_c
            q_n = jnp.sum(q.astype(F32) * n_in, axis=1, keepdims=True) * qscale
            den = jnp.sum(p, axis=1, keepdims=True) + scale * q_n
            hs = num / jnp.maximum(jnp.abs(den), floor_all[:, ln:ln + 1])

            kw = k.astype(F32) * w_all[:, ln:ln + 1]
            c_loc = jnp.dot(kw.T.astype(BF16), v, preferred_element_type=F32)
            n_loc = jnp.sum(kw, axis=0, keepdims=True)
            a_old = a_old_all[:, ln:ln + 1]
            a_loc = a_loc_all[:, ln:ln + 1]
            c_refs[st][...] = a_old * c_in + a_loc * c_loc
            n_refs[st][...] = a_old * n_in + a_loc * n_loc

            hn = _rms(hs, ng_ref[:, h * MLSTM_DV:(h + 1) * MLSTM_DV])
            o_gate = _sigmoid(p_ref[bi, :, o_off:o_off + MLSTM_DV].astype(F32))
            out_ref[bi, :, h * MLSTM_DV:(h + 1) * MLSTM_DV] = (o_gate * hn).astype(out_ref.dtype)


def _mlstm_core(proj, gates, bias, norm_g, bsz, seq):
    L = MLSTM_CHUNK
    d = MLSTM_HEADS * MLSTM_DV
    n_chain = bsz * MLSTM_HEADS
    chunk = lambda c: (0, c, 0)
    return pl.pallas_call(
        _mlstm_body,
        grid=(seq // L,),
        in_specs=[pl.BlockSpec((bsz, L, MLSTM_MAIN), chunk),
                  pl.BlockSpec((bsz, L, LANES), chunk),
                  pl.BlockSpec((1, LANES), lambda c: (0, 0)),
                  pl.BlockSpec((1, d), lambda c: (0, 0))],
        out_specs=pl.BlockSpec((bsz, L, d), chunk),
        out_shape=jax.ShapeDtypeStruct((bsz, seq, d), BF16),
        scratch_shapes=([pltpu.VMEM((MLSTM_DQK, MLSTM_DV), F32)] * n_chain
                        + [pltpu.VMEM((1, MLSTM_DQK), F32)] * n_chain
                        + [pltpu.VMEM((1, LANES), F32)] * bsz),
        compiler_params=_params("arbitrary"),
        name="mlstm_core",
    )(proj.reshape(bsz, seq, MLSTM_MAIN), gates.reshape(bsz, seq, LANES), bias, norm_g.reshape(1, d))


def _mlstm_layer(x, prenormed, ln_g, w_in, b_gate, norm_g, bsz, seq):
    w_gate = _pad_cols(w_in[:, MLSTM_MAIN:], LANES).astype(BF16)
    bias = jnp.pad(b_gate.astype(F32), (0, LANES - b_gate.shape[0])).reshape(1, LANES)
    proj, gates = _norm_proj(x, ln_g, w_in.astype(BF16), w_gate, MLSTM_MAIN, prenormed)
    mixed = _mlstm_core(proj, gates, bias, norm_g, bsz, seq)
    return mixed.reshape(bsz * seq, -1)


def _diff_attn_body(lam_ref, g_ref, q_ref, k_ref, vt_ref, o_ref, sc_a, sc_b, m_ref, l_ref, acc_ref, *,
                    tq, lam_init):
    qi = pl.program_id(2)
    q = q_ref[...]
    lane = lax.broadcasted_iota(jnp.int32, q.shape, 1)
    zero = jnp.zeros_like(q)
    is_c1 = (lane & (DIFF_DH // 2)) != 0
    q_parts = (jnp.where(is_c1, zero, q), jnp.where(is_c1, q, zero))
    th = tq // 2
    chains = [(c, half) for c in range(2) for half in range(2)]
    q_ch = [q_parts[c][half * th:(half + 1) * th] for c, half in chains]
    kpos = lax.broadcasted_iota(jnp.int32, (tq, th), 0)
    qpos = lax.broadcasted_iota(jnp.int32, (tq, th), 1)
    causal = [kpos <= qpos + half * th for half in range(2)]
    ones_rows = jnp.ones((16, tq), BF16)

    def put_scores(sc_ref, ki):
        k = k_ref[pl.ds(pl.multiple_of(ki * tq, tq), tq), :]
        for x in range(4):
            sc_ref[x] = lax.dot_general(k, q_ch[x], _NT, preferred_element_type=F32)

    def softmax_pv(ki, sc_ref, masked):
        vt = vt_ref[:, pl.ds(pl.multiple_of(ki * tq, tq), tq)]
        vt_ones = jnp.concatenate([vt, ones_rows], axis=0)
        def load_scores(x):
            s = sc_ref[x]
            return jnp.where(causal[chains[x][1]], s, -jnp.inf) if masked else s

        m_old = [m_ref[x] for x in range(4)]
        m_new = [jnp.maximum(m_old[x], _col_reduce(load_scores(x), jnp.maximum, jnp.max)) for x in range(4)]
        probs = [jnp.exp2((load_scores(x) - m_new[x]).astype(BF16)) for x in range(4)]
        for x in range(4):
            alpha = jnp.exp2(m_old[x] - m_new[x])
            pv = jnp.dot(vt_ones, probs[x], preferred_element_type=F32)
            m_ref[x] = m_new[x]
            l_ref[x] = alpha * l_ref[x] + pv[DIFF_DV:DIFF_DV + 1]
            acc_ref[x] = alpha * acc_ref[x] + pv[:DIFF_DV]

    m_ref[...] = jnp.full(m_ref.shape, -jnp.inf, F32)
    l_ref[...] = jnp.zeros_like(l_ref)
    acc_ref[...] = jnp.zeros_like(acc_ref)

    odd = qi & 1

    @pl.when(odd == 0)
    def _():
        put_scores(sc_a, 0)

    @pl.when(odd == 1)
    def _():
        put_scores(sc_b, 0)
        put_scores(sc_a, 1)
        softmax_pv(0, sc_b, False)

    def pair(j, carry):
        b0 = odd + 2 * j
        put_scores(sc_b, b0 + 1)
        softmax_pv(b0, sc_a, False)
        put_scores(sc_a, b0 + 2)
        softmax_pv(b0 + 1, sc_b, False)
        return carry

    lax.fori_loop(0, lax.shift_right_logical(qi, 1), pair, 0)
    softmax_pv(qi, sc_a, True)
    a0, a1 = (jnp.concatenate([acc_ref[2 * c] / l_ref[2 * c], acc_ref[2 * c + 1] / l_ref[2 * c + 1]], axis=1)
              for c in range(2))

    lam = lam_ref[...]
    lam_val = (jnp.exp(jnp.sum(lam[0:1] * lam[1:2], axis=1, keepdims=True))
               - jnp.exp(jnp.sum(lam[2:3] * lam[3:4], axis=1, keepdims=True)) + lam_init)
    o = a0 - lam_val * a1
    ms = jnp.mean(o * o, axis=0, keepdims=True)
    y = o * lax.rsqrt(ms + EPS) * g_ref[...] * (1.0 - lam_init)
    o_ref[...] = y.T.astype(o_ref.dtype)


def _diff_attn(qkv, lam, norm_g, bsz, seq, layer_idx, tq=512):
    tq = min(tq, seq)
    nq = seq // tq
    nh = DIFF_HEADS
    lam_init = 0.8 - 0.6 * math.exp(-0.3 * layer_idx)
    v_t = qkv[:, 2 * nh * DIFF_DV:].reshape(bsz, seq, nh, DIFF_DV).transpose(0, 2, 3, 1)
    v_t = v_t.reshape(bsz * nh * DIFF_DV, seq)
    return pl.pallas_call(
        functools.partial(_diff_attn_body, tq=tq, lam_init=lam_init),
        grid=(bsz, nh, nq),
        in_specs=[pl.BlockSpec((4, DIFF_DH), lambda b, h, i: (0, 0)),
                  pl.BlockSpec((DIFF_DV, 1), lambda b, h, i: (0, 0)),
                  pl.BlockSpec((tq, DIFF_DV), lambda b, h, i: (b * nq + i, h)),
                  pl.BlockSpec((seq, DIFF_DV), lambda b, h, i: (b, nh + h)),
                  pl.BlockSpec((DIFF_DV, seq), lambda b, h, i: (b * nh + h, 0))],
        out_specs=pl.BlockSpec((tq, DIFF_DV), lambda b, h, i: (b * nq + i, h)),
        out_shape=jax.ShapeDtypeStruct((bsz * seq, nh * DIFF_DV), BF16),
        scratch_shapes=[pltpu.VMEM((4, tq, tq // 2), F32), pltpu.VMEM((4, tq, tq // 2), F32),
                        pltpu.VMEM((4, 1, tq // 2), F32), pltpu.VMEM((4, 1, tq // 2), F32),
                        pltpu.VMEM((4, DIFF_DV, tq // 2), F32)],
        compiler_params=_params("parallel", "parallel", "arbitrary"),
        name="diff_attn",
    )(lam.astype(F32), norm_g.reshape(DIFF_DV, 1).astype(F32), qkv, qkv, v_t)


def _rope_tables(seq):
    inv = ROPE_THETA ** (-jnp.arange(0, DIFF_DH, 2, dtype=F32) / DIFF_DH)
    ang = jnp.arange(seq, dtype=F32)[:, None] * inv[None, :]
    cos, sin = jnp.cos(ang), jnp.sin(ang)
    cos_t = jnp.tile(cos, (1, LANES // cos.shape[1]))
    sin_t = jnp.concatenate([-sin, -sin, sin, sin], axis=1)
    return cos_t, sin_t


def _rope_lane_order():
    half = DIFF_DH // 2
    return [c * DIFF_DH + part * half + i for part in range(2) for c in range(2) for i in range(half)]


def _diff_layer(xn, w_in, lam, norm_g, bsz, seq, layer_idx):
    cos, sin = _rope_tables(seq)
    order = jnp.asarray(_rope_lane_order(), jnp.int32)
    nqk = 2 * DIFF_HEADS * DIFF_DV
    qk_cols = (jnp.arange(nqk // LANES, dtype=jnp.int32) * LANES)[:, None] + order[None, :]
    w_in = jnp.concatenate([w_in[:, qk_cols.reshape(-1)], w_in[:, nqk:]], axis=1)
    qkv = _rope_proj(xn, w_in.astype(BF16), cos, sin, seq)
    return _diff_attn(qkv, lam, norm_g, bsz, seq, layer_idx)


SSD_BC = SSD_GROUPS * SSD_STATE
SSD_CONV_DIM = SSD_INNER + 2 * SSD_BC
SSD_MAIN = SSD_INNER + SSD_CONV_DIM


def _expand_heads(a, g, width):
    nr = SSD_HEADS_PER_GROUP
    lane = lax.broadcasted_iota(jnp.int32, (a.shape[0], width), 1)
    out = jnp.broadcast_to(a[:, nr * g + nr - 1:nr * g + nr], (a.shape[0], width))
    for r in range(nr - 2, -1, -1):
        out = jnp.where(lane < SSD_HEADDIM * (r + 1), a[:, nr * g + r:nr * g + r + 1], out)
    return out


def _ssd_body(z_ref, x_ref, bc_ref, dt_ref, cw_ref, cb_ref, dtb_ref, alog_ref, dsk_ref, ng_ref, e3_ref,
              out_ref, ext_ref, s_ref):
    L = SSD_CHUNK
    gw = SSD_GROUP_WIDTH
    ns = SSD_STATE

    @pl.when(pl.program_id(1) == 0)
    def _():
        ext_ref[0:CONV_HALO, :] = jnp.zeros((CONV_HALO, SSD_CONV_DIM), F32)
        s_ref[...] = jnp.zeros_like(s_ref)

    ext_ref[CONV_HALO:CONV_HALO + L, 0:SSD_INNER] = x_ref[...].astype(F32)
    ext_ref[CONV_HALO:CONV_HALO + L, SSD_INNER:SSD_CONV_DIM] = bc_ref[...].astype(F32)

    def conv_silu(lo, width):
        ext = ext_ref[:, lo:lo + width]
        acc = cb_ref[:, lo:lo + width] + cw_ref[SSD_CONV - 1:SSD_CONV, lo:lo + width] * ext[CONV_HALO:]
        for j in range(SSD_CONV - 1):
            shifted = pltpu.roll(ext, SSD_CONV - 1 - j, 0)[CONV_HALO:]
            acc = acc + cw_ref[j:j + 1, lo:lo + width] * shifted
        return _silu(acc)

    dt = _softplus(dt_ref[...] + dtb_ref[...])
    cum = _cumsum_rows(dt * (-jnp.exp(alog_ref[...])))
    cum_t = cum.T
    cum_last = cum[L - 1:L, :]
    exp_cum = jnp.exp(cum)
    decay_to_end = jnp.exp(cum_last - cum)
    chunk_decay = jnp.exp(cum_last)
    row = lax.broadcasted_iota(jnp.int32, (L, L), 0)
    col = lax.broadcasted_iota(jnp.int32, (L, L), 1)
    causal = col <= row
    lane_gw = lax.broadcasted_iota(jnp.int32, (1, gw), 1)
    head_rows = [jnp.where(lane_gw // SSD_HEADDIM == r, 1.0, 0.0).astype(BF16)
                 for r in range(SSD_HEADS_PER_GROUP)]

    def expand(a):
        hi = a.astype(BF16)
        rest = a - hi.astype(F32)
        mid = rest.astype(BF16)
        lo = (rest - mid.astype(F32)).astype(BF16)
        return jnp.dot(jnp.concatenate([hi, mid, lo], axis=1), e3_ref[...], preferred_element_type=F32)

    dt_x = expand(dt)
    exp_cum_x = expand(exp_cum)
    decay_to_end_x = expand(decay_to_end)

    for g in range(SSD_GROUPS):
        cols = slice(g * gw, (g + 1) * gw)
        xs = conv_silu(g * gw, gw)
        bm = conv_silu(SSD_INNER + g * ns, ns).astype(BF16)
        cm = conv_silu(SSD_INNER + SSD_BC + g * ns, ns).astype(BF16)
        xdt = xs * dt_x[:, cols]
        s_in = s_ref[g]
        cb = lax.dot_general(cm, bm, _NT, preferred_element_type=F32)
        y = exp_cum_x[:, cols] * jnp.dot(cm, s_in.astype(BF16), preferred_element_type=F32)
        weights, inputs = [], []
        xdt_lo = xdt.astype(BF16)
        for r in range(SSD_HEADS_PER_GROUP):
            hh = SSD_HEADS_PER_GROUP * g + r
            decay = jnp.exp(jnp.where(causal, cum[:, hh:hh + 1] - cum_t[hh:hh + 1, :], -jnp.inf))
            weights.append((cb * decay).astype(BF16))
            inputs.append(xdt_lo * head_rows[r])
        y = y + jnp.dot(jnp.concatenate(weights, axis=1), jnp.concatenate(inputs, axis=0),
                        preferred_element_type=F32)
        xdte = (xdt * decay_to_end_x[:, cols]).astype(BF16)
        states = jnp.dot(bm.astype(F32).T.astype(BF16), xdte, preferred_element_type=F32)
        s_ref[g] = _expand_heads(chunk_decay, g, gw) * s_in + states
        y = y + dsk_ref[:, g * gw:(g + 1) * gw] * xs
        y = y * _silu(z_ref[:, g * gw:(g + 1) * gw].astype(F32))
        out_ref[:, g * gw:(g + 1) * gw] = _rms(y, ng_ref[:, g * gw:(g + 1) * gw]).astype(out_ref.dtype)

    ext_ref[0:CONV_HALO, :] = ext_ref[L:L + CONV_HALO, :]


def _ssd_core(proj, dt, conv_w, conv_b, dt_bias, a_log, d_skip, norm_g, bsz, seq):
    L = SSD_CHUNK
    nc = seq // L
    rowmap = lambda b, c: (b * nc + c, 0)
    const = lambda b, c: (0, 0)
    head_of_channel = jnp.arange(SSD_INNER, dtype=jnp.int32) // SSD_HEADDIM
    expand1 = (jnp.arange(LANES, dtype=jnp.int32)[:, None] == head_of_channel[None, :]).astype(BF16)
    expand3 = jnp.concatenate([expand1] * 3, axis=0)
    return pl.pallas_call(
        _ssd_body,
        grid=(bsz, nc),
        in_specs=[pl.BlockSpec((L, SSD_INNER), rowmap),
                  pl.BlockSpec((L, SSD_INNER), lambda b, c: (b * nc + c, 1)),
                  pl.BlockSpec((L, 2 * SSD_BC), lambda b, c: (b * nc + c, 2)),
                  pl.BlockSpec((L, LANES), rowmap),
                  pl.BlockSpec((SSD_CONV, SSD_CONV_DIM), const),
                  pl.BlockSpec((1, SSD_CONV_DIM), const),
                  pl.BlockSpec((1, LANES), const),
                  pl.BlockSpec((1, LANES), const),
                  pl.BlockSpec((1, SSD_INNER), const),
                  pl.BlockSpec((1, SSD_INNER), const),
                  pl.BlockSpec((3 * LANES, SSD_INNER), const)],
        out_specs=pl.BlockSpec((L, SSD_INNER), rowmap),
        out_shape=jax.ShapeDtypeStruct((bsz * seq, SSD_INNER), BF16),
        scratch_shapes=[pltpu.VMEM((L + CONV_HALO, SSD_CONV_DIM), F32),
                        pltpu.VMEM((SSD_GROUPS, SSD_STATE, SSD_GROUP_WIDTH), F32)],
        compiler_params=_params("parallel", "arbitrary"),
        name="ssd_core",
    )(proj, proj, proj, dt, conv_w, conv_b, dt_bias, a_log, d_skip, norm_g, expand3)


def _ssd_layer(x, prenormed, ln_g, w_in, conv_w, conv_b, dt_bias, a_log, d_skip, norm_g, bsz, seq):
    w_dt = _pad_cols(w_in[:, SSD_MAIN:], LANES).astype(BF16)
    proj, dt = _norm_proj(x, ln_g, w_in.astype(BF16), w_dt, SSD_MAIN, prenormed)
    pad_heads = lambda a: jnp.pad(a.astype(F32), (0, LANES - SSD_HEADS)).reshape(1, LANES)
    mixed = _ssd_core(proj, dt, conv_w.astype(F32), conv_b.astype(F32).reshape(1, SSD_CONV_DIM),
                      pad_heads(dt_bias), pad_heads(a_log),
                      jnp.repeat(d_skip.astype(F32), SSD_HEADDIM).reshape(1, SSD_INNER),
                      norm_g.astype(F32).reshape(1, SSD_INNER), bsz, seq)
    return mixed


def kernel(x, ln_mix, ln_mlp, w_up, w_down, ln_f, mlstm_w_in, mlstm_b_gate, mlstm_norm, mlstm_w_out, diff_w_in, diff_lam, diff_norm, diff_w_out, ssd_w_in, ssd_conv_w, ssd_conv_b, ssd_dt_bias, ssd_A_log, ssd_D, ssd_norm, ssd_w_out):
    bsz, seq, d = x.shape
    h = x.reshape(bsz * seq, d)
    xn = None
    for i in range(DEPTH):
        kind, j = i % N_MIXERS, i // N_MIXERS
        prenormed = xn is not None
        mixer_in = xn if prenormed else h
        if kind == 0:
            mixed = _mlstm_layer(mixer_in, prenormed, ln_mix[i], mlstm_w_in[j], mlstm_b_gate[j],
                                 mlstm_norm[j], bsz, seq)
            w_out = mlstm_w_out[j]
        elif kind == 1:
            assert prenormed, "the attention projection expects a pre-normalised input"
            mixed = _diff_layer(mixer_in, diff_w_in[j], diff_lam[j], diff_norm[j], bsz, seq, i)
            w_out = diff_w_out[j]
        else:
            mixed = _ssd_layer(mixer_in, prenormed, ln_mix[i], ssd_w_in[j], ssd_conv_w[j], ssd_conv_b[j],
                               ssd_dt_bias[j], ssd_A_log[j], ssd_D[j], ssd_norm[j], bsz, seq)
            w_out = ssd_w_out[j]
        last = i == DEPTH - 1
        out = _mix_mlp(mixed, w_out.astype(BF16), h, ln_mlp[i], w_up[i].astype(BF16), w_down[i].astype(BF16),
                       ln_f if last else ln_mix[i + 1], final_norm=last)
        h, xn = (out, None) if last else out
    return h.reshape(bsz, seq, d)
```

```python
import functools
import math

import jax
import jax.numpy as jnp
from jax import lax
from jax.experimental import pallas as pl
from jax.experimental.pallas import tpu as pltpu

F32 = jnp.float32
BF16 = jnp.bfloat16

DEPTH = 4
N_MIXERS = 3
EPS = 1e-6

MLSTM_HEADS = 4
MLSTM_DV = 256
MLSTM_DQK = 128
MLSTM_CHUNK = 128
GATE_SOFTCAP = 15.0

DIFF_HEADS = 8
DIFF_DH = 64
DIFF_DV = 128
ROPE_THETA = 10000.0

SSD_INNER = 2048
SSD_HEADDIM = 64
SSD_HEADS = 32
SSD_GROUPS = 8
SSD_STATE = 128
SSD_CONV = 4
SSD_CHUNK = 128
SSD_GROUP_WIDTH = SSD_INNER // SSD_GROUPS
SSD_HEADS_PER_GROUP = SSD_HEADS // SSD_GROUPS

LANES = 128
CONV_HALO = 8
VMEM_LIMIT = 48 * 1024 * 1024
VMEM_LIMIT_MLP = 56 * 1024 * 1024

_NT = (((1,), (1,)), ((), ()))


def _params(*sem):
    return pltpu.CompilerParams(dimension_semantics=sem, vmem_limit_bytes=VMEM_LIMIT)


def _rms(x, g):
    ms = jnp.mean(x * x, axis=-1, keepdims=True)
    return x * lax.rsqrt(ms + EPS) * g


def _cumsum_rows(x):
    n = x.shape[0]
    row = lax.broadcasted_iota(jnp.int32, x.shape, 0)
    s = 1
    while s < n:
        x = x + jnp.where(row >= s, pltpu.roll(x, s, 0), 0.0)
        s *= 2
    return x


def _cummax_rows(x):
    n = x.shape[0]
    row = lax.broadcasted_iota(jnp.int32, x.shape, 0)
    s = 1
    while s < n:
        x = jnp.maximum(x, jnp.where(row >= s, pltpu.roll(x, s, 0), -jnp.inf))
        s *= 2
    return x


COL_REDUCE_SLAB = 64


def _col_reduce(x, pair_op, reduce_fn):
    slab = min(COL_REDUCE_SLAB, x.shape[0])
    acc = x[:slab]
    for r0 in range(slab, x.shape[0], slab):
        acc = pair_op(acc, x[r0:r0 + slab])
    while acc.shape[0] > 8:
        half = acc.shape[0] // 2
        acc = pair_op(acc[:half], acc[half:])
    return reduce_fn(acc, axis=0, keepdims=True)


def _sigmoid(x):
    return 0.5 + 0.5 * jnp.tanh(0.5 * x)


def _silu(x):
    half = 0.5 * x
    return half + half * jnp.tanh(half)


def _softplus(x):
    return jnp.maximum(x, 0.0) + jnp.log1p(jnp.exp(-jnp.abs(x)))


def _pad_cols(w, n):
    return jnp.pad(w, ((0, 0), (0, n - w.shape[1])))


def _norm_proj_body(x_ref, g_ref, w_ref, ws_ref, o_ref, os_ref, *scratch):
    first = pl.program_id(1) == 0
    if scratch:
        xn_ref, = scratch

        @pl.when(first)
        def _():
            xn_ref[...] = _rms(x_ref[...], g_ref[...]).astype(BF16)
    else:
        xn_ref = x_ref

    @pl.when(first)
    def _():
        os_ref[...] = jnp.dot(xn_ref[...], ws_ref[...], preferred_element_type=F32)

    o_ref[...] = jnp.dot(xn_ref[...], w_ref[...], preferred_element_type=F32).astype(o_ref.dtype)


def _norm_proj(x, g, w, w_side, n_main, prenormed, tn=1536, tm=1024):
    n, d = x.shape
    tm = min(tm, n)
    return pl.pallas_call(
        _norm_proj_body,
        grid=(n // tm, n_main // tn),
        in_specs=[pl.BlockSpec((tm, d), lambda i, j: (i, 0)),
                  pl.BlockSpec((1, d), lambda i, j: (0, 0)),
                  pl.BlockSpec((d, tn), lambda i, j: (0, j)),
                  pl.BlockSpec((d, LANES), lambda i, j: (0, 0))],
        out_specs=[pl.BlockSpec((tm, tn), lambda i, j: (i, j)),
                   pl.BlockSpec((tm, LANES), lambda i, j: (i, 0))],
        out_shape=[jax.ShapeDtypeStruct((n, n_main), BF16), jax.ShapeDtypeStruct((n, LANES), F32)],
        scratch_shapes=[] if prenormed else [pltpu.VMEM((tm, d), BF16)],
        compiler_params=_params("parallel", "arbitrary"),
        name="norm_proj",
    )(x, g.reshape(1, d), w, w_side)


ROPE_ROW_SPLIT = 4


def _rope_proj_body(xn_ref, w_ref, cos_ref, sin_ref, o_ref, vt_ref):
    j = pl.program_id(1)
    rows = xn_ref.shape[0] // ROPE_ROW_SPLIT

    @pl.when(j < 2)
    def _():
        qscale = jnp.where(j == 0, DIFF_DH ** -0.5 * math.log2(math.e), 1.0).astype(F32)
        for r in range(ROPE_ROW_SPLIT):
            slab = slice(r * rows, (r + 1) * rows)
            acc = jnp.dot(xn_ref[slab, :], w_ref[...], preferred_element_type=F32)
            cos = cos_ref[slab, :] * qscale
            sin = sin_ref[slab, :] * qscale
            for c in range(acc.shape[1] // LANES):
                xc = acc[:, c * LANES:(c + 1) * LANES]
                rot = pltpu.roll(xc, LANES // 2, 1)
                o_ref[slab, c * LANES:(c + 1) * LANES] = (xc * cos + rot * sin).astype(o_ref.dtype)

    @pl.when(j >= 2)
    def _():
        for r in range(ROPE_ROW_SPLIT):
            slab = slice(r * rows, (r + 1) * rows)
            acc = jnp.dot(xn_ref[slab, :], w_ref[...], preferred_element_type=F32)
            vt_ref[:, slab] = acc.T.astype(vt_ref.dtype)


def _rope_proj(xn, w, cos, sin, bsz, seq, tm=1024):
    n, d = xn.shape
    tm = min(tm, seq)
    tn = d
    nseq = seq // tm
    return pl.pallas_call(
        _rope_proj_body,
        grid=(n // tm, 3),
        in_specs=[pl.BlockSpec((tm, d), lambda i, j: (i, 0)),
                  pl.BlockSpec((d, tn), lambda i, j: (0, j)),
                  pl.BlockSpec((tm, LANES), lambda i, j: (i % nseq, 0)),
                  pl.BlockSpec((tm, LANES), lambda i, j: (i % nseq, 0))],
        out_specs=[pl.BlockSpec((tm, tn), lambda i, j: (i, jnp.minimum(j, 1))),
                   pl.BlockSpec((d, tm), lambda i, j: (i // nseq, i % nseq))],
        out_shape=[jax.ShapeDtypeStruct((n, 2 * d), BF16), jax.ShapeDtypeStruct((bsz * d, seq), BF16)],
        compiler_params=_params("parallel", "arbitrary"),
        name="rope_proj",
    )(xn, w, cos, sin)


def _mix_mlp_body(a_ref, wo_ref, h_ref, g_ref, wu_ref, wd_ref, gn_ref, o_ref, *rest, final_norm):
    xn_ref = rest[-1]
    j = pl.program_id(1)

    @pl.when(j == 0)
    def _():
        h1 = h_ref[...] + jnp.dot(a_ref[...], wo_ref[...], preferred_element_type=F32)
        o_ref[...] = h1
        xn_ref[...] = _rms(h1, g_ref[...]).astype(BF16)

    up = jnp.dot(xn_ref[...], wu_ref[...], preferred_element_type=F32)
    up = jnp.square(jnp.maximum(up, 0.0)).astype(BF16)
    o_ref[...] += jnp.dot(up, wd_ref[...], preferred_element_type=F32)

    @pl.when(j == pl.num_programs(1) - 1)
    def _():
        normed = _rms(o_ref[...], gn_ref[...])
        if final_norm:
            o_ref[...] = normed
        else:
            rest[0][...] = normed.astype(BF16)


def _mix_mlp(a, w_out, h, g, w_up, w_down, layer, g_next, final_norm, tm=1024, tf=1024):
    n, d = h.shape
    k = a.shape[1]
    dff = w_up.shape[2]
    tm = min(tm, n)
    row = lambda i, j: (i, 0)
    const = lambda i, j: (0, 0)
    return pl.pallas_call(
        functools.partial(_mix_mlp_body, final_norm=final_norm),
        grid=(n // tm, dff // tf),
        in_specs=[pl.BlockSpec((tm, k), row),
                  pl.BlockSpec((k, d), const),
                  pl.BlockSpec((tm, d), row),
                  pl.BlockSpec((1, d), const),
                  pl.BlockSpec((None, d, tf), lambda i, j: (layer, 0, j)),
                  pl.BlockSpec((None, tf, d), lambda i, j: (layer, j, 0)),
                  pl.BlockSpec((1, d), const)],
        out_specs=pl.BlockSpec((tm, d), row) if final_norm else [pl.BlockSpec((tm, d), row)] * 2,
        out_shape=(jax.ShapeDtypeStruct((n, d), F32) if final_norm else
                   [jax.ShapeDtypeStruct((n, d), F32), jax.ShapeDtypeStruct((n, d), BF16)]),
        scratch_shapes=[pltpu.VMEM((tm, d), BF16)],
        compiler_params=pltpu.CompilerParams(dimension_semantics=("parallel", "arbitrary"),
                                             vmem_limit_bytes=VMEM_LIMIT_MLP),
        name="mix_mlp",
    )(a, w_out, h, g.reshape(1, d), w_up, w_down, g_next.reshape(1, d))


MLSTM_QK = MLSTM_HEADS * MLSTM_DQK
MLSTM_MAIN = 2 * MLSTM_QK + 2 * MLSTM_HEADS * MLSTM_DV


def _mlstm_body(p_ref, gate_ref, bias_ref, ng_ref, out_ref, *state_refs):
    L = MLSTM_CHUNK
    nh = MLSTM_HEADS
    n_batch = p_ref.shape[0]
    n_chain = n_batch * nh
    c_refs = state_refs[:n_chain]
    n_refs = state_refs[n_chain:2 * n_chain]
    m_refs = state_refs[2 * n_chain:]

    @pl.when(pl.program_id(0) == 0)
    def _():
        for ref in state_refs:
            ref[...] = jnp.zeros_like(ref)

    row = lax.broadcasted_iota(jnp.int32, (L, L), 0)
    col = lax.broadcasted_iota(jnp.int32, (L, L), 1)
    causal = col <= row
    qscale = MLSTM_DQK ** -0.5

    for bi in range(n_batch):
        gates = gate_ref[bi] + bias_ref[...]
        capped = GATE_SOFTCAP * jnp.tanh(gates / GATE_SOFTCAP)
        log_f = -_softplus(-capped)
        log_i = pltpu.roll(capped, nh, 1)
        b = _cumsum_rows(log_f)
        g = log_i - b
        g_max = _cummax_rows(g)
        m_in = m_refs[bi][...]
        m_run = jnp.maximum(m_in, g_max)
        m_t = b + m_run
        scale_all = jnp.exp(m_in - m_run)
        floor_all = jnp.exp(-m_t)
        b_last = b[L - 1:L, :]
        g_max_last = g_max[L - 1:L, :]
        m_run_last = m_run[L - 1:L, :]
        w_all = jnp.exp(g - g_max_last)
        a_old_all = jnp.exp(m_in - m_run_last)
        a_loc_all = jnp.exp(g_max_last - m_run_last)
        m_refs[bi][...] = b_last + m_run_last
        g_t = g.T

        for h in range(nh):
            st = bi * nh + h
            ln = nh + h
            v_off = 2 * MLSTM_QK + h * MLSTM_DV
            o_off = v_off + nh * MLSTM_DV
            q = p_ref[bi, :, h * MLSTM_DQK:(h + 1) * MLSTM_DQK]
            k = p_ref[bi, :, MLSTM_QK + h * MLSTM_DQK:MLSTM_QK + (h + 1) * MLSTM_DQK]
            v = p_ref[bi, :, v_off:v_off + MLSTM_DV]
            c_in = c_refs[st][...]
            n_in = n_refs[st][...]

            decay = jnp.exp(jnp.where(causal, g_t[ln:ln + 1, :] - m_run[:, ln:ln + 1], -jnp.inf))
            qk = lax.dot_general(q, k, _NT, preferred_element_type=F32) * qscale
            p = decay * qk
            scale = scale_all[:, ln:ln + 1]
            q_c = jnp.dot(q, c_in.astype(BF16), preferred_element_type=F32) * qscale
            num = jnp.dot(p.astype(BF16), v, preferred_element_type=F32) + scale * q_c
            q_n = jnp.sum(q.astype(F32) * n_in, axis=1, keepdims=True) * qscale
            den = jnp.sum(p, axis=1, keepdims=True) + scale * q_n
            hs = num / jnp.maximum(jnp.abs(den), floor_all[:, ln:ln + 1])

            kw = k.astype(F32) * w_all[:, ln:ln + 1]
            c_loc = jnp.dot(kw.T.astype(BF16), v, preferred_element_type=F32)
            n_loc = jnp.sum(kw, axis=0, keepdims=True)
            a_old = a_old_all[:, ln:ln + 1]
            a_loc = a_loc_all[:, ln:ln + 1]
            c_refs[st][...] = a_old * c_in + a_loc * c_loc
            n_refs[st][...] = a_old * n_in + a_loc * n_loc

            hn = _rms(hs, ng_ref[:, h * MLSTM_DV:(h + 1) * MLSTM_DV])
            o_gate = _sigmoid(p_ref[bi, :, o_off:o_off + MLSTM_DV].astype(F32))
            out_ref[bi, :, h * MLSTM_DV:(h + 1) * MLSTM_DV] = (o_gate * hn).astype(out_ref.dtype)


def _mlstm_core(proj, gates, bias, norm_g, bsz, seq):
    L = MLSTM_CHUNK
    d = MLSTM_HEADS * MLSTM_DV
    n_chain = bsz * MLSTM_HEADS
    chunk = lambda c: (0, c, 0)
    return pl.pallas_call(
        _mlstm_body,
        grid=(seq // L,),
        in_specs=[pl.BlockSpec((bsz, L, MLSTM_MAIN), chunk),
                  pl.BlockSpec((bsz, L, LANES), chunk),
                  pl.BlockSpec((1, LANES), lambda c: (0, 0)),
                  pl.BlockSpec((1, d), lambda c: (0, 0))],
        out_specs=pl.BlockSpec((bsz, L, d), chunk),
        out_shape=jax.ShapeDtypeStruct((bsz, seq, d), BF16),
        scratch_shapes=([pltpu.VMEM((MLSTM_DQK, MLSTM_DV), F32)] * n_chain
                        + [pltpu.VMEM((1, MLSTM_DQK), F32)] * n_chain
                        + [pltpu.VMEM((1, LANES), F32)] * bsz),
        compiler_params=_params("arbitrary"),
        name="mlstm_core",
    )(proj.reshape(bsz, seq, MLSTM_MAIN), gates.reshape(bsz, seq, LANES), bias, norm_g.reshape(1, d))


def _mlstm_layer(x, prenormed, ln_g, w_in, b_gate, norm_g, bsz, seq):
    w_gate = _pad_cols(w_in[:, MLSTM_MAIN:], LANES).astype(BF16)
    bias = jnp.pad(b_gate.astype(F32), (0, LANES - b_gate.shape[0])).reshape(1, LANES)
    proj, gates = _norm_proj(x, ln_g, w_in.astype(BF16), w_gate, MLSTM_MAIN, prenormed)
    mixed = _mlstm_core(proj, gates, bias, norm_g, bsz, seq)
    return mixed.reshape(bsz * seq, -1)


def _diff_attn_body(lam_ref, g_ref, q_ref, qn_ref, k_ref, vt_ref, o_ref, sc_a, sc_b, sc_c, m_ref, l_ref,
                    acc_ref, *, tq, lam_init):
    qi = pl.program_id(2)
    q = q_ref[...]
    lane = lax.broadcasted_iota(jnp.int32, q.shape, 1)
    zero = jnp.zeros_like(q)
    is_c1 = (lane & (DIFF_DH // 2)) != 0
    th = tq // 2
    chains = [(c, half) for c in range(2) for half in range(2)]

    def chain_queries(qblk):
        parts = (jnp.where(is_c1, zero, qblk), jnp.where(is_c1, qblk, zero))
        return [parts[c][half * th:(half + 1) * th] for c, half in chains]

    q_ch = chain_queries(q)
    kpos = lax.broadcasted_iota(jnp.int32, (tq, th), 0)
    qpos = lax.broadcasted_iota(jnp.int32, (tq, th), 1)
    causal = [kpos <= qpos + half * th for half in range(2)]

    def put_scores(sc_ref, ki, queries=q_ch):
        k = k_ref[pl.ds(pl.multiple_of(ki * tq, tq), tq), :]
        for x in range(4):
            sc_ref[x] = lax.dot_general(k, queries[x], _NT, preferred_element_type=F32)

    def softmax_pv(ki, sc_ref, masked):
        vt = vt_ref[:, pl.ds(pl.multiple_of(ki * tq, tq), tq)]
        for x, (c, half) in enumerate(chains):
            def load_scores():
                s = sc_ref[x]
                return jnp.where(causal[half], s, -jnp.inf) if masked else s

            m = m_ref[x]
            m_new = jnp.maximum(m, _col_reduce(load_scores(), jnp.maximum, jnp.max))
            alpha = jnp.exp2(m - m_new)
            p = jnp.exp2(load_scores() - m_new)
            m_ref[x] = m_new
            l_ref[x] = alpha * l_ref[x] + _col_reduce(p, jnp.add, jnp.sum)
            acc_ref[x] = alpha * acc_ref[x] + jnp.dot(vt, p.astype(BF16), preferred_element_type=F32)

    m_ref[...] = jnp.full(m_ref.shape, -jnp.inf, F32)
    l_ref[...] = jnp.zeros_like(l_ref)
    acc_ref[...] = jnp.zeros_like(acc_ref)

    odd = qi & 1

    @pl.when(qi == 0)
    def _():
        put_scores(sc_a, 0)

    @pl.when(odd == 1)
    def _():
        put_scores(sc_a, 1)
        softmax_pv(0, sc_c, False)

    @pl.when((qi > 0) & (odd == 0))
    def _():
        put_scores(sc_b, 1)
        softmax_pv(0, sc_c, False)
        put_scores(sc_a, 2)
        softmax_pv(1, sc_b, False)

    first = 2 - odd

    def pair(j, carry):
        b0 = first + 2 * j
        put_scores(sc_b, b0 + 1)
        softmax_pv(b0, sc_a, False)
        put_scores(sc_a, b0 + 2)
        softmax_pv(b0 + 1, sc_b, False)
        return carry

    lax.fori_loop(0, lax.shift_right_logical(jnp.maximum(qi - first, 0), 1), pair, 0)

    put_scores(sc_c, 0, chain_queries(qn_ref[...]))
    softmax_pv(qi, sc_a, True)
    a0, a1 = (jnp.concatenate([acc_ref[2 * c] / l_ref[2 * c], acc_ref[2 * c + 1] / l_ref[2 * c + 1]], axis=1)
              for c in range(2))

    lam = lam_ref[...]
    lam_val = (jnp.exp(jnp.sum(lam[0:1] * lam[1:2], axis=1, keepdims=True))
               - jnp.exp(jnp.sum(lam[2:3] * lam[3:4], axis=1, keepdims=True)) + lam_init)
    o = a0 - lam_val * a1
    ms = jnp.mean(o * o, axis=0, keepdims=True)
    y = o * lax.rsqrt(ms + EPS) * g_ref[...] * (1.0 - lam_init)
    o_ref[...] = y.T.astype(o_ref.dtype)


def _diff_attn(qk, v_t, lam, norm_g, bsz, seq, layer_idx, tq=512):
    tq = min(tq, seq)
    nq = seq // tq
    nh = DIFF_HEADS
    lam_init = 0.8 - 0.6 * math.exp(-0.3 * layer_idx)
    return pl.pallas_call(
        functools.partial(_diff_attn_body, tq=tq, lam_init=lam_init),
        grid=(bsz, nh, nq),
        in_specs=[pl.BlockSpec((4, DIFF_DH), lambda b, h, i: (0, 0)),
                  pl.BlockSpec((DIFF_DV, 1), lambda b, h, i: (0, 0)),
                  pl.BlockSpec((tq, DIFF_DV), lambda b, h, i: (b * nq + i, h)),
                  pl.BlockSpec((tq, DIFF_DV), lambda b, h, i: (b * nq + jnp.minimum(i + 1, nq - 1), h)),
                  pl.BlockSpec((seq, DIFF_DV), lambda b, h, i: (b, nh + h)),
                  pl.BlockSpec((DIFF_DV, seq), lambda b, h, i: (b * nh + h, 0))],
        out_specs=pl.BlockSpec((tq, DIFF_DV), lambda b, h, i: (b * nq + i, h)),
        out_shape=jax.ShapeDtypeStruct((bsz * seq, nh * DIFF_DV), BF16),
        scratch_shapes=[pltpu.VMEM((4, tq, tq // 2), F32)] * 3 + [
                        pltpu.VMEM((4, 1, tq // 2), F32), pltpu.VMEM((4, 1, tq // 2), F32),
                        pltpu.VMEM((4, DIFF_DV, tq // 2), F32)],
        compiler_params=_params("arbitrary", "arbitrary", "arbitrary"),
        name="diff_attn",
    )(lam.astype(F32), norm_g.reshape(DIFF_DV, 1).astype(F32), qk, qk, qk, v_t)


def _rope_tables(seq):
    inv = ROPE_THETA ** (-jnp.arange(0, DIFF_DH, 2, dtype=F32) / DIFF_DH)
    ang = jnp.arange(seq, dtype=F32)[:, None] * inv[None, :]
    cos, sin = jnp.cos(ang), jnp.sin(ang)
    cos_t = jnp.tile(cos, (1, LANES // cos.shape[1]))
    sin_t = jnp.concatenate([-sin, -sin, sin, sin], axis=1)
    return cos_t, sin_t


def _rope_lane_order():
    half = DIFF_DH // 2
    return [c * DIFF_DH + part * half + i for part in range(2) for c in range(2) for i in range(half)]


def _diff_layer(xn, w_in, lam, norm_g, bsz, seq, layer_idx):
    cos, sin = _rope_tables(seq)
    order = jnp.asarray(_rope_lane_order(), jnp.int32)
    nqk = 2 * DIFF_HEADS * DIFF_DV
    qk_cols = (jnp.arange(nqk // LANES, dtype=jnp.int32) * LANES)[:, None] + order[None, :]
    w_in = jnp.concatenate([w_in[:, qk_cols.reshape(-1)], w_in[:, nqk:]], axis=1)
    qk, v_t = _rope_proj(xn, w_in.astype(BF16), cos, sin, bsz, seq)
    return _diff_attn(qk, v_t, lam, norm_g, bsz, seq, layer_idx)


SSD_BC = SSD_GROUPS * SSD_STATE
SSD_CONV_DIM = SSD_INNER + 2 * SSD_BC
SSD_MAIN = SSD_INNER + SSD_CONV_DIM


def _expand_heads(a, g, width):
    nr = SSD_HEADS_PER_GROUP
    lane = lax.broadcasted_iota(jnp.int32, (a.shape[0], width), 1)
    out = jnp.broadcast_to(a[:, nr * g + nr - 1:nr * g + nr], (a.shape[0], width))
    for r in range(nr - 2, -1, -1):
        out = jnp.where(lane < SSD_HEADDIM * (r + 1), a[:, nr * g + r:nr * g + r + 1], out)
    return out


def _ssd_body(z_ref, x_ref, bc_ref, dt_ref, cw_ref, cb_ref, dtb_ref, alog_ref, dsk_ref, ng_ref, e3_ref,
              out_ref, ext_ref, s_ref):
    L = SSD_CHUNK
    gw = SSD_GROUP_WIDTH
    ns = SSD_STATE

    @pl.when(pl.program_id(1) == 0)
    def _():
        ext_ref[0:CONV_HALO, :] = jnp.zeros((CONV_HALO, SSD_CONV_DIM), F32)
        s_ref[...] = jnp.zeros_like(s_ref)

    ext_ref[CONV_HALO:CONV_HALO + L, 0:SSD_INNER] = x_ref[...].astype(F32)
    ext_ref[CONV_HALO:CONV_HALO + L, SSD_INNER:SSD_CONV_DIM] = bc_ref[...].astype(F32)

    def conv_silu(lo, width):
        ext = ext_ref[:, lo:lo + width]
        acc = cb_ref[:, lo:lo + width] + cw_ref[SSD_CONV - 1:SSD_CONV, lo:lo + width] * ext[CONV_HALO:]
        for j in range(SSD_CONV - 1):
            shifted = pltpu.roll(ext, SSD_CONV - 1 - j, 0)[CONV_HALO:]
            acc = acc + cw_ref[j:j + 1, lo:lo + width] * shifted
        return _silu(acc)

    dt = _softplus(dt_ref[...] + dtb_ref[...])
    cum = _cumsum_rows(dt * (-jnp.exp(alog_ref[...])))
    cum_t = cum.T
    cum_last = cum[L - 1:L, :]
    exp_cum = jnp.exp(cum)
    decay_to_end = jnp.exp(cum_last - cum)
    chunk_decay = jnp.exp(cum_last)
    row = lax.broadcasted_iota(jnp.int32, (L, L), 0)
    col = lax.broadcasted_iota(jnp.int32, (L, L), 1)
    causal = col <= row
    lane_gw = lax.broadcasted_iota(jnp.int32, (1, gw), 1)
    head_rows = [jnp.where(lane_gw // SSD_HEADDIM == r, 1.0, 0.0).astype(BF16)
                 for r in range(SSD_HEADS_PER_GROUP)]

    def expand(a):
        hi = a.astype(BF16)
        rest = a - hi.astype(F32)
        mid = rest.astype(BF16)
        lo = (rest - mid.astype(F32)).astype(BF16)
        return jnp.dot(jnp.concatenate([hi, mid, lo], axis=1), e3_ref[...], preferred_element_type=F32)

    dt_x = expand(dt)
    exp_cum_x = expand(exp_cum)
    decay_to_end_x = expand(decay_to_end)

    for g in range(SSD_GROUPS):
        cols = slice(g * gw, (g + 1) * gw)
        xs = conv_silu(g * gw, gw)
        bm = conv_silu(SSD_INNER + g * ns, ns).astype(BF16)
        cm = conv_silu(SSD_INNER + SSD_BC + g * ns, ns).astype(BF16)
        xdt = xs * dt_x[:, cols]
        s_in = s_ref[g]
        cb = lax.dot_general(cm, bm, _NT, preferred_element_type=F32)
        y = exp_cum_x[:, cols] * jnp.dot(cm, s_in.astype(BF16), preferred_element_type=F32)
        weights, inputs = [], []
        xdt_lo = xdt.astype(BF16)
        for r in range(SSD_HEADS_PER_GROUP):
            hh = SSD_HEADS_PER_GROUP * g + r
            decay = jnp.exp(jnp.where(causal, cum[:, hh:hh + 1] - cum_t[hh:hh + 1, :], -jnp.inf))
            weights.append((cb * decay).astype(BF16))
            inputs.append(xdt_lo * head_rows[r])
        y = y + jnp.dot(jnp.concatenate(weights, axis=1), jnp.concatenate(inputs, axis=0),
                        preferred_element_type=F32)
        xdte = (xdt * decay_to_end_x[:, cols]).astype(BF16)
        states = jnp.dot(bm.astype(F32).T.astype(BF16), xdte, preferred_element_type=F32)
        s_ref[g] = _expand_heads(chunk_decay, g, gw) * s_in + states
        y = y + dsk_ref[:, g * gw:(g + 1) * gw] * xs
        y = y * _silu(z_ref[:, g * gw:(g + 1) * gw].astype(F32))
        out_ref[:, g * gw:(g + 1) * gw] = _rms(y, ng_ref[:, g * gw:(g + 1) * gw]).astype(out_ref.dtype)

    ext_ref[0:CONV_HALO, :] = ext_ref[L:L + CONV_HALO, :]


def _ssd_core(proj, dt, conv_w, conv_b, dt_bias, a_log, d_skip, norm_g, bsz, seq):
    L = SSD_CHUNK
    nc = seq // L
    rowmap = lambda b, c: (b * nc + c, 0)
    const = lambda b, c: (0, 0)
    head_of_channel = jnp.arange(SSD_INNER, dtype=jnp.int32) // SSD_HEADDIM
    expand1 = (jnp.arange(LANES, dtype=jnp.int32)[:, None] == head_of_channel[None, :]).astype(BF16)
    expand3 = jnp.concatenate([expand1] * 3, axis=0)
    return pl.pallas_call(
        _ssd_body,
        grid=(bsz, nc),
        in_specs=[pl.BlockSpec((L, SSD_INNER), rowmap),
                  pl.BlockSpec((L, SSD_INNER), lambda b, c: (b * nc + c, 1)),
                  pl.BlockSpec((L, 2 * SSD_BC), lambda b, c: (b * nc + c, 2)),
                  pl.BlockSpec((L, LANES), rowmap),
                  pl.BlockSpec((SSD_CONV, SSD_CONV_DIM), const),
                  pl.BlockSpec((1, SSD_CONV_DIM), const),
                  pl.BlockSpec((1, LANES), const),
                  pl.BlockSpec((1, LANES), const),
                  pl.BlockSpec((1, SSD_INNER), const),
                  pl.BlockSpec((1, SSD_INNER), const),
                  pl.BlockSpec((3 * LANES, SSD_INNER), const)],
        out_specs=pl.BlockSpec((L, SSD_INNER), rowmap),
        out_shape=jax.ShapeDtypeStruct((bsz * seq, SSD_INNER), BF16),
        scratch_shapes=[pltpu.VMEM((L + CONV_HALO, SSD_CONV_DIM), F32),
                        pltpu.VMEM((SSD_GROUPS, SSD_STATE, SSD_GROUP_WIDTH), F32)],
        compiler_params=_params("parallel", "arbitrary"),
        name="ssd_core",
    )(proj, proj, proj, dt, conv_w, conv_b, dt_bias, a_log, d_skip, norm_g, expand3)


def _ssd_layer(x, prenormed, ln_g, w_in, conv_w, conv_b, dt_bias, a_log, d_skip, norm_g, bsz, seq):
    w_dt = _pad_cols(w_in[:, SSD_MAIN:], LANES).astype(BF16)
    proj, dt = _norm_proj(x, ln_g, w_in.astype(BF16), w_dt, SSD_MAIN, prenormed)
    pad_heads = lambda a: jnp.pad(a.astype(F32), (0, LANES - SSD_HEADS)).reshape(1, LANES)
    mixed = _ssd_core(proj, dt, conv_w.astype(F32), conv_b.astype(F32).reshape(1, SSD_CONV_DIM),
                      pad_heads(dt_bias), pad_heads(a_log),
                      jnp.repeat(d_skip.astype(F32), SSD_HEADDIM).reshape(1, SSD_INNER),
                      norm_g.astype(F32).reshape(1, SSD_INNER), bsz, seq)
    return mixed


def kernel(x, ln_mix, ln_mlp, w_up, w_down, ln_f, mlstm_w_in, mlstm_b_gate, mlstm_norm, mlstm_w_out, diff_w_in, diff_lam, diff_norm, diff_w_out, ssd_w_in, ssd_conv_w, ssd_conv_b, ssd_dt_bias, ssd_A_log, ssd_D, ssd_norm, ssd_w_out):
    bsz, seq, d = x.shape
    h = x.reshape(bsz * seq, d)
    w_up_lo, w_down_lo = w_up.astype(BF16), w_down.astype(BF16)
    xn = None
    for i in range(DEPTH):
        kind, j = i % N_MIXERS, i // N_MIXERS
        prenormed = xn is not None
        mixer_in = xn if prenormed else h
        if kind == 0:
            mixed = _mlstm_layer(mixer_in, prenormed, ln_mix[i], mlstm_w_in[j], mlstm_b_gate[j],
                                 mlstm_norm[j], bsz, seq)
            w_out = mlstm_w_out[j]
        elif kind == 1:
            assert prenormed, "the attention projection expects a pre-normalised input"
            mixed = _diff_layer(mixer_in, diff_w_in[j], diff_lam[j], diff_norm[j], bsz, seq, i)
            w_out = diff_w_out[j]
        else:
            mixed = _ssd_layer(mixer_in, prenormed, ln_mix[i], ssd_w_in[j], ssd_conv_w[j], ssd_conv_b[j],
                               ssd_dt_bias[j], ssd_A_log[j], ssd_D[j], ssd_norm[j], bsz, seq)
            w_out = ssd_w_out[j]
        last = i == DEPTH - 1
        out = _mix_mlp(mixed, w_out.astype(BF16), h, ln_mlp[i], w_up_lo, w_down_lo, i,
                       ln_f if last else ln_mix[i + 1], final_norm=last)
        h, xn = (out, None) if last else out
    return h.reshape(bsz, seq, d)
```

```python
import functools
import math

import jax
import jax.numpy as jnp
from jax import lax
from jax.experimental import pallas as pl
from jax.experimental.pallas import tpu as pltpu

F32 = jnp.float32
BF16 = jnp.bfloat16

DEPTH = 4
N_MIXERS = 3
EPS = 1e-6

MLSTM_HEADS = 4
MLSTM_DV = 256
MLSTM_DQK = 128
MLSTM_CHUNK = 128
GATE_SOFTCAP = 15.0

DIFF_HEADS = 8
DIFF_DH = 64
DIFF_DV = 128
ROPE_THETA = 10000.0

SSD_INNER = 2048
SSD_HEADDIM = 64
SSD_HEADS = 32
SSD_GROUPS = 8
SSD_STATE = 128
SSD_CONV = 4
SSD_CHUNK = 128
SSD_GROUP_WIDTH = SSD_INNER // SSD_GROUPS
SSD_HEADS_PER_GROUP = SSD_HEADS // SSD_GROUPS

LANES = 128
CONV_HALO = 8
VMEM_LIMIT = 48 * 1024 * 1024
VMEM_LIMIT_MLP = 56 * 1024 * 1024

_NT = (((1,), (1,)), ((), ()))


def _params(*sem):
    return pltpu.CompilerParams(dimension_semantics=sem, vmem_limit_bytes=VMEM_LIMIT)


def _rms(x, g):
    ms = jnp.mean(x * x, axis=-1, keepdims=True)
    return x * lax.rsqrt(ms + EPS) * g


def _cumsum_rows(x):
    n = x.shape[0]
    row = lax.broadcasted_iota(jnp.int32, x.shape, 0)
    s = 1
    while s < n:
        x = x + jnp.where(row >= s, pltpu.roll(x, s, 0), 0.0)
        s *= 2
    return x


def _cummax_rows(x):
    n = x.shape[0]
    row = lax.broadcasted_iota(jnp.int32, x.shape, 0)
    s = 1
    while s < n:
        x = jnp.maximum(x, jnp.where(row >= s, pltpu.roll(x, s, 0), -jnp.inf))
        s *= 2
    return x


COL_REDUCE_SLAB = 64


def _col_reduce(x, pair_op, reduce_fn):
    slab = min(COL_REDUCE_SLAB, x.shape[0])
    acc = x[:slab]
    for r0 in range(slab, x.shape[0], slab):
        acc = pair_op(acc, x[r0:r0 + slab])
    while acc.shape[0] > 8:
        half = acc.shape[0] // 2
        acc = pair_op(acc[:half], acc[half:])
    return reduce_fn(acc, axis=0, keepdims=True)


def _sigmoid(x):
    return 0.5 + 0.5 * jnp.tanh(0.5 * x)


def _silu(x):
    half = 0.5 * x
    return half + half * jnp.tanh(half)


def _softplus(x):
    return jnp.maximum(x, 0.0) + jnp.log1p(jnp.exp(-jnp.abs(x)))


def _pad_cols(w, n):
    return jnp.pad(w, ((0, 0), (0, n - w.shape[1])))


def _norm_proj_body(x_ref, g_ref, w_ref, ws_ref, o_ref, os_ref, *scratch):
    first = pl.program_id(1) == 0
    if scratch:
        xn_ref, = scratch

        @pl.when(first)
        def _():
            xn_ref[...] = _rms(x_ref[...], g_ref[...]).astype(BF16)
    else:
        xn_ref = x_ref

    @pl.when(first)
    def _():
        os_ref[...] = jnp.dot(xn_ref[...], ws_ref[...], preferred_element_type=F32)

    o_ref[...] = jnp.dot(xn_ref[...], w_ref[...], preferred_element_type=F32).astype(o_ref.dtype)


def _norm_proj(x, g, w, w_side, n_main, prenormed, tn=1536, tm=1024):
    n, d = x.shape
    tm = min(tm, n)
    return pl.pallas_call(
        _norm_proj_body,
        grid=(n // tm, n_main // tn),
        in_specs=[pl.BlockSpec((tm, d), lambda i, j: (i, 0)),
                  pl.BlockSpec((1, d), lambda i, j: (0, 0)),
                  pl.BlockSpec((d, tn), lambda i, j: (0, j)),
                  pl.BlockSpec((d, LANES), lambda i, j: (0, 0))],
        out_specs=[pl.BlockSpec((tm, tn), lambda i, j: (i, j)),
                   pl.BlockSpec((tm, LANES), lambda i, j: (i, 0))],
        out_shape=[jax.ShapeDtypeStruct((n, n_main), BF16), jax.ShapeDtypeStruct((n, LANES), F32)],
        scratch_shapes=[] if prenormed else [pltpu.VMEM((tm, d), BF16)],
        compiler_params=_params("parallel", "arbitrary"),
        name="norm_proj",
    )(x, g.reshape(1, d), w, w_side)


ROPE_ROW_SPLIT = 4


def _rope_proj_body(xn_ref, w_ref, cos_ref, sin_ref, o_ref, vt_ref):
    j = pl.program_id(1)
    rows = xn_ref.shape[0] // ROPE_ROW_SPLIT

    @pl.when(j < 2)
    def _():
        qscale = jnp.where(j == 0, DIFF_DH ** -0.5 * math.log2(math.e), 1.0).astype(F32)
        for r in range(ROPE_ROW_SPLIT):
            slab = slice(r * rows, (r + 1) * rows)
            acc = jnp.dot(xn_ref[slab, :], w_ref[...], preferred_element_type=F32)
            cos = cos_ref[slab, :] * qscale
            sin = sin_ref[slab, :] * qscale
            for c in range(acc.shape[1] // LANES):
                xc = acc[:, c * LANES:(c + 1) * LANES]
                rot = pltpu.roll(xc, LANES // 2, 1)
                o_ref[slab, c * LANES:(c + 1) * LANES] = (xc * cos + rot * sin).astype(o_ref.dtype)

    @pl.when(j >= 2)
    def _():
        for r in range(ROPE_ROW_SPLIT):
            slab = slice(r * rows, (r + 1) * rows)
            acc = jnp.dot(xn_ref[slab, :], w_ref[...], preferred_element_type=F32)
            vt_ref[:, slab] = acc.T.astype(vt_ref.dtype)


def _rope_proj(xn, w, cos, sin, bsz, seq, tm=1024):
    n, d = xn.shape
    tm = min(tm, seq)
    tn = d
    nseq = seq // tm
    return pl.pallas_call(
        _rope_proj_body,
        grid=(n // tm, 3),
        in_specs=[pl.BlockSpec((tm, d), lambda i, j: (i, 0)),
                  pl.BlockSpec((d, tn), lambda i, j: (0, j)),
                  pl.BlockSpec((tm, LANES), lambda i, j: (i % nseq, 0)),
                  pl.BlockSpec((tm, LANES), lambda i, j: (i % nseq, 0))],
        out_specs=[pl.BlockSpec((tm, tn), lambda i, j: (i, jnp.minimum(j, 1))),
                   pl.BlockSpec((d, tm), lambda i, j: (i // nseq, i % nseq))],
        out_shape=[jax.ShapeDtypeStruct((n, 2 * d), BF16), jax.ShapeDtypeStruct((bsz * d, seq), BF16)],
        compiler_params=_params("parallel", "arbitrary"),
        name="rope_proj",
    )(xn, w, cos, sin)


def _mix_mlp_body(a_ref, wo_ref, h_ref, g_ref, wu_ref, wd_ref, gn_ref, o_ref, *rest, final_norm):
    xn_ref = rest[-1]
    j = pl.program_id(1)

    @pl.when(j == 0)
    def _():
        h1 = h_ref[...] + jnp.dot(a_ref[...], wo_ref[...], preferred_element_type=F32)
        o_ref[...] = h1
        xn_ref[...] = _rms(h1, g_ref[...]).astype(BF16)

    up = jnp.dot(xn_ref[...], wu_ref[...], preferred_element_type=F32)
    up = jnp.square(jnp.maximum(up, 0.0)).astype(BF16)
    o_ref[...] += jnp.dot(up, wd_ref[...], preferred_element_type=F32)

    @pl.when(j == pl.num_programs(1) - 1)
    def _():
        normed = _rms(o_ref[...], gn_ref[...])
        if final_norm:
            o_ref[...] = normed
        else:
            rest[0][...] = normed.astype(BF16)


def _mix_mlp(a, w_out, h, g, w_up, w_down, layer, g_next, final_norm, tm=1024, tf=1024):
    n, d = h.shape
    k = a.shape[1]
    dff = w_up.shape[2]
    tm = min(tm, n)
    row = lambda i, j: (i, 0)
    const = lambda i, j: (0, 0)
    return pl.pallas_call(
        functools.partial(_mix_mlp_body, final_norm=final_norm),
        grid=(n // tm, dff // tf),
        in_specs=[pl.BlockSpec((tm, k), row),
                  pl.BlockSpec((k, d), const),
                  pl.BlockSpec((tm, d), row),
                  pl.BlockSpec((1, d), const),
                  pl.BlockSpec((None, d, tf), lambda i, j: (layer, 0, j)),
                  pl.BlockSpec((None, tf, d), lambda i, j: (layer, j, 0)),
                  pl.BlockSpec((1, d), const)],
        out_specs=pl.BlockSpec((tm, d), row) if final_norm else [pl.BlockSpec((tm, d), row)] * 2,
        out_shape=(jax.ShapeDtypeStruct((n, d), F32) if final_norm else
                   [jax.ShapeDtypeStruct((n, d), F32), jax.ShapeDtypeStruct((n, d), BF16)]),
        scratch_shapes=[pltpu.VMEM((tm, d), BF16)],
        compiler_params=pltpu.CompilerParams(dimension_semantics=("parallel", "arbitrary"),
                                             vmem_limit_bytes=VMEM_LIMIT_MLP),
        name="mix_mlp",
    )(a, w_out, h, g.reshape(1, d), w_up, w_down, g_next.reshape(1, d))


MLSTM_QK = MLSTM_HEADS * MLSTM_DQK
MLSTM_MAIN = 2 * MLSTM_QK + 2 * MLSTM_HEADS * MLSTM_DV


def _mlstm_body(p_ref, gate_ref, bias_ref, ng_ref, out_ref, *state_refs):
    L = MLSTM_CHUNK
    nh = MLSTM_HEADS
    n_batch = p_ref.shape[0]
    n_chain = n_batch * nh
    c_refs = state_refs[:n_chain]
    n_refs = state_refs[n_chain:2 * n_chain]
    m_refs = state_refs[2 * n_chain:]

    @pl.when(pl.program_id(0) == 0)
    def _():
        for ref in state_refs:
            ref[...] = jnp.zeros_like(ref)

    row = lax.broadcasted_iota(jnp.int32, (L, L), 0)
    col = lax.broadcasted_iota(jnp.int32, (L, L), 1)
    causal = col <= row
    qscale = MLSTM_DQK ** -0.5

    for bi in range(n_batch):
        gates = gate_ref[bi] + bias_ref[...]
        capped = GATE_SOFTCAP * jnp.tanh(gates / GATE_SOFTCAP)
        log_f = -_softplus(-capped)
        log_i = pltpu.roll(capped, nh, 1)
        b = _cumsum_rows(log_f)
        g = log_i - b
        g_max = _cummax_rows(g)
        m_in = m_refs[bi][...]
        m_run = jnp.maximum(m_in, g_max)
        m_t = b + m_run
        scale_all = jnp.exp(m_in - m_run)
        floor_all = jnp.exp(-m_t)
        b_last = b[L - 1:L, :]
        g_max_last = g_max[L - 1:L, :]
        m_run_last = m_run[L - 1:L, :]
        w_all = jnp.exp(g - g_max_last)
        a_old_all = jnp.exp(m_in - m_run_last)
        a_loc_all = jnp.exp(g_max_last - m_run_last)
        m_refs[bi][...] = b_last + m_run_last
        g_t = g.T

        for h in range(nh):
            st = bi * nh + h
            ln = nh + h
            v_off = 2 * MLSTM_QK + h * MLSTM_DV
            o_off = v_off + nh * MLSTM_DV
            q = p_ref[bi, :, h * MLSTM_DQK:(h + 1) * MLSTM_DQK]
            k = p_ref[bi, :, MLSTM_QK + h * MLSTM_DQK:MLSTM_QK + (h + 1) * MLSTM_DQK]
            v = p_ref[bi, :, v_off:v_off + MLSTM_DV]
            c_in = c_refs[st][...]
            n_in = n_refs[st][...]

            decay = jnp.exp(jnp.where(causal, g_t[ln:ln + 1, :] - m_run[:, ln:ln + 1], -jnp.inf))
            qk = lax.dot_general(q, k, _NT, preferred_element_type=F32) * qscale
            p = decay * qk
            scale = scale_all[:, ln:ln + 1]
            q_c = jnp.dot(q, c_in.astype(BF16), preferred_element_type=F32) * qscale
            num = jnp.dot(p.astype(BF16), v, preferred_element_type=F32) + scale * q_c
            q_n = jnp.sum(q.astype(F32) * n_in, axis=1, keepdims=True) * qscale
            den = jnp.sum(p, axis=1, keepdims=True) + scale * q_n
            hs = num / jnp.maximum(jnp.abs(den), floor_all[:, ln:ln + 1])

            kw = k.astype(F32) * w_all[:, ln:ln + 1]
            c_loc = jnp.dot(kw.T.astype(BF16), v, preferred_element_type=F32)
            n_loc = jnp.sum(kw, axis=0, keepdims=True)
            a_old = a_old_all[:, ln:ln + 1]
            a_loc = a_loc_all[:, ln:ln + 1]
            c_refs[st][...] = a_old * c_in + a_loc * c_loc
            n_refs[st][...] = a_old * n_in + a_loc * n_loc

            hn = _rms(hs, ng_ref[:, h * MLSTM_DV:(h + 1) * MLSTM_DV])
            o_gate = _sigmoid(p_ref[bi, :, o_off:o_off + MLSTM_DV].astype(F32))
            out_ref[bi, :, h * MLSTM_DV:(h + 1) * MLSTM_DV] = (o_gate * hn).astype(out_ref.dtype)


def _mlstm_core(proj, gates, bias, norm_g, bsz, seq):
    L = MLSTM_CHUNK
    d = MLSTM_HEADS * MLSTM_DV
    n_chain = bsz * MLSTM_HEADS
    chunk = lambda c: (0, c, 0)
    return pl.pallas_call(
        _mlstm_body,
        grid=(seq // L,),
        in_specs=[pl.BlockSpec((bsz, L, MLSTM_MAIN), chunk),
                  pl.BlockSpec((bsz, L, LANES), chunk),
                  pl.BlockSpec((1, LANES), lambda c: (0, 0)),
                  pl.BlockSpec((1, d), lambda c: (0, 0))],
        out_specs=pl.BlockSpec((bsz, L, d), chunk),
        out_shape=jax.ShapeDtypeStruct((bsz, seq, d), BF16),
        scratch_shapes=([pltpu.VMEM((MLSTM_DQK, MLSTM_DV), F32)] * n_chain
                        + [pltpu.VMEM((1, MLSTM_DQK), F32)] * n_chain
                        + [pltpu.VMEM((1, LANES), F32)] * bsz),
        compiler_params=_params("arbitrary"),
        name="mlstm_core",
    )(proj.reshape(bsz, seq, MLSTM_MAIN), gates.reshape(bsz, seq, LANES), bias, norm_g.reshape(1, d))


def _mlstm_layer(x, prenormed, ln_g, w_in, b_gate, norm_g, bsz, seq):
    w_gate = _pad_cols(w_in[:, MLSTM_MAIN:], LANES).astype(BF16)
    bias = jnp.pad(b_gate.astype(F32), (0, LANES - b_gate.shape[0])).reshape(1, LANES)
    proj, gates = _norm_proj(x, ln_g, w_in.astype(BF16), w_gate, MLSTM_MAIN, prenormed)
    mixed = _mlstm_core(proj, gates, bias, norm_g, bsz, seq)
    return mixed.reshape(bsz * seq, -1)


def _diff_attn_body(lam_ref, g_ref, q_ref, qn_ref, k_ref, vt_ref, o_ref, sc_a, sc_b, sc_c, cm_a, cm_b, cm_c,
                    m_ref, l_ref, acc_ref, *, tq, lam_init):
    qi = pl.program_id(2)
    q = q_ref[...]
    lane = lax.broadcasted_iota(jnp.int32, q.shape, 1)
    zero = jnp.zeros_like(q)
    is_c1 = (lane & (DIFF_DH // 2)) != 0
    th = tq // 2
    chains = [(c, half) for c in range(2) for half in range(2)]

    def chain_queries(qblk):
        parts = (jnp.where(is_c1, zero, qblk), jnp.where(is_c1, qblk, zero))
        return [parts[c][half * th:(half + 1) * th] for c, half in chains]

    q_ch = chain_queries(q)
    kpos = lax.broadcasted_iota(jnp.int32, (tq, th), 0)
    qpos = lax.broadcasted_iota(jnp.int32, (tq, th), 1)
    causal = [kpos <= qpos + half * th for half in range(2)]

    buf_a, buf_b, buf_c = (sc_a, cm_a), (sc_b, cm_b), (sc_c, cm_c)

    def put_scores(buf, ki, queries=q_ch):
        sc_ref, cm_ref = buf
        k = k_ref[pl.ds(pl.multiple_of(ki * tq, tq), tq), :]
        for x in range(4):
            s = lax.dot_general(k, queries[x], _NT, preferred_element_type=F32)
            sc_ref[x] = s
            cm_ref[x] = _col_reduce(s, jnp.maximum, jnp.max)

    def softmax_pv(ki, buf, masked):
        sc_ref, cm_ref = buf
        vt = vt_ref[:, pl.ds(pl.multiple_of(ki * tq, tq), tq)]
        for x, (c, half) in enumerate(chains):
            def load_scores():
                s = sc_ref[x]
                return jnp.where(causal[half], s, -jnp.inf) if masked else s

            m = m_ref[x]
            col_max = _col_reduce(load_scores(), jnp.maximum, jnp.max) if masked else cm_ref[x]
            m_new = jnp.maximum(m, col_max)
            alpha = jnp.exp2(m - m_new)
            p = jnp.exp2(load_scores() - m_new)
            m_ref[x] = m_new
            l_ref[x] = alpha * l_ref[x] + _col_reduce(p, jnp.add, jnp.sum)
            acc_ref[x] = alpha * acc_ref[x] + jnp.dot(vt, p.astype(BF16), preferred_element_type=F32)

    m_ref[...] = jnp.full(m_ref.shape, -jnp.inf, F32)
    l_ref[...] = jnp.zeros_like(l_ref)
    acc_ref[...] = jnp.zeros_like(acc_ref)

    odd = qi & 1

    @pl.when(qi == 0)
    def _():
        put_scores(buf_a, 0)

    @pl.when(odd == 1)
    def _():
        put_scores(buf_a, 1)
        softmax_pv(0, buf_c, False)

    @pl.when((qi > 0) & (odd == 0))
    def _():
        put_scores(buf_b, 1)
        softmax_pv(0, buf_c, False)
        put_scores(buf_a, 2)
        softmax_pv(1, buf_b, False)

    first = 2 - odd

    def pair(b0):
        put_scores(buf_b, b0 + 1)
        softmax_pv(b0, buf_a, False)
        put_scores(buf_a, b0 + 2)
        softmax_pv(b0 + 1, buf_b, False)

    n_pairs = lax.shift_right_logical(jnp.maximum(qi - first, 0), 1)
    lone = n_pairs & 1

    @pl.when(lone == 1)
    def _():
        pair(first)

    def two_pairs(j, carry):
        b0 = first + 2 * lone + 4 * j
        pair(b0)
        pair(b0 + 2)
        return carry

    lax.fori_loop(0, lax.shift_right_logical(n_pairs, 1), two_pairs, 0)

    put_scores(buf_c, 0, chain_queries(qn_ref[...]))
    softmax_pv(qi, buf_a, True)
    a0, a1 = (jnp.concatenate([acc_ref[2 * c] / l_ref[2 * c], acc_ref[2 * c + 1] / l_ref[2 * c + 1]], axis=1)
              for c in range(2))

    lam = lam_ref[...]
    lam_val = (jnp.exp(jnp.sum(lam[0:1] * lam[1:2], axis=1, keepdims=True))
               - jnp.exp(jnp.sum(lam[2:3] * lam[3:4], axis=1, keepdims=True)) + lam_init)
    o = a0 - lam_val * a1
    ms = jnp.mean(o * o, axis=0, keepdims=True)
    y = o * lax.rsqrt(ms + EPS) * g_ref[...] * (1.0 - lam_init)
    o_ref[...] = y.T.astype(o_ref.dtype)


def _diff_attn(qk, v_t, lam, norm_g, bsz, seq, layer_idx, tq=512):
    tq = min(tq, seq)
    nq = seq // tq
    nh = DIFF_HEADS
    lam_init = 0.8 - 0.6 * math.exp(-0.3 * layer_idx)
    return pl.pallas_call(
        functools.partial(_diff_attn_body, tq=tq, lam_init=lam_init),
        grid=(bsz, nh, nq),
        in_specs=[pl.BlockSpec((4, DIFF_DH), lambda b, h, i: (0, 0)),
                  pl.BlockSpec((DIFF_DV, 1), lambda b, h, i: (0, 0)),
                  pl.BlockSpec((tq, DIFF_DV), lambda b, h, i: (b * nq + i, h)),
                  pl.BlockSpec((tq, DIFF_DV), lambda b, h, i: (b * nq + jnp.minimum(i + 1, nq - 1), h)),
                  pl.BlockSpec((seq, DIFF_DV), lambda b, h, i: (b, nh + h)),
                  pl.BlockSpec((DIFF_DV, seq), lambda b, h, i: (b * nh + h, 0))],
        out_specs=pl.BlockSpec((tq, DIFF_DV), lambda b, h, i: (b * nq + i, h)),
        out_shape=jax.ShapeDtypeStruct((bsz * seq, nh * DIFF_DV), BF16),
        scratch_shapes=[pltpu.VMEM((4, tq, tq // 2), F32)] * 3 + [pltpu.VMEM((4, 1, tq // 2), F32)] * 3 + [
                        pltpu.VMEM((4, 1, tq // 2), F32), pltpu.VMEM((4, 1, tq // 2), F32),
                        pltpu.VMEM((4, DIFF_DV, tq // 2), F32)],
        compiler_params=_params("arbitrary", "arbitrary", "arbitrary"),
        name="diff_attn",
    )(lam.astype(F32), norm_g.reshape(DIFF_DV, 1).astype(F32), qk, qk, qk, v_t)


def _rope_tables(seq):
    inv = ROPE_THETA ** (-jnp.arange(0, DIFF_DH, 2, dtype=F32) / DIFF_DH)
    ang = jnp.arange(seq, dtype=F32)[:, None] * inv[None, :]
    cos, sin = jnp.cos(ang), jnp.sin(ang)
    cos_t = jnp.tile(cos, (1, LANES // cos.shape[1]))
    sin_t = jnp.concatenate([-sin, -sin, sin, sin], axis=1)
    return cos_t, sin_t


def _rope_lane_order():
    half = DIFF_DH // 2
    return [c * DIFF_DH + part * half + i for part in range(2) for c in range(2) for i in range(half)]


def _diff_layer(xn, w_in, lam, norm_g, bsz, seq, layer_idx):
    cos, sin = _rope_tables(seq)
    order = jnp.asarray(_rope_lane_order(), jnp.int32)
    nqk = 2 * DIFF_HEADS * DIFF_DV
    qk_cols = (jnp.arange(nqk // LANES, dtype=jnp.int32) * LANES)[:, None] + order[None, :]
    w_in = jnp.concatenate([w_in[:, qk_cols.reshape(-1)], w_in[:, nqk:]], axis=1)
    qk, v_t = _rope_proj(xn, w_in.astype(BF16), cos, sin, bsz, seq)
    return _diff_attn(qk, v_t, lam, norm_g, bsz, seq, layer_idx)


SSD_BC = SSD_GROUPS * SSD_STATE
SSD_CONV_DIM = SSD_INNER + 2 * SSD_BC
SSD_MAIN = SSD_INNER + SSD_CONV_DIM


def _expand_heads(a, g, width):
    nr = SSD_HEADS_PER_GROUP
    lane = lax.broadcasted_iota(jnp.int32, (a.shape[0], width), 1)
    out = jnp.broadcast_to(a[:, nr * g + nr - 1:nr * g + nr], (a.shape[0], width))
    for r in range(nr - 2, -1, -1):
        out = jnp.where(lane < SSD_HEADDIM * (r + 1), a[:, nr * g + r:nr * g + r + 1], out)
    return out


def _ssd_body(z_ref, x_ref, bc_ref, dt_ref, cw_ref, cb_ref, dtb_ref, alog_ref, dsk_ref, ng_ref, e3_ref,
              out_ref, ext_ref, s_ref):
    L = SSD_CHUNK
    gw = SSD_GROUP_WIDTH
    ns = SSD_STATE

    @pl.when(pl.program_id(1) == 0)
    def _():
        ext_ref[0:CONV_HALO, :] = jnp.zeros((CONV_HALO, SSD_CONV_DIM), F32)
        s_ref[...] = jnp.zeros_like(s_ref)

    ext_ref[CONV_HALO:CONV_HALO + L, 0:SSD_INNER] = x_ref[...].astype(F32)
    ext_ref[CONV_HALO:CONV_HALO + L, SSD_INNER:SSD_CONV_DIM] = bc_ref[...].astype(F32)

    def conv_silu(lo, width):
        ext = ext_ref[:, lo:lo + width]
        acc = cb_ref[:, lo:lo + width] + cw_ref[SSD_CONV - 1:SSD_CONV, lo:lo + width] * ext[CONV_HALO:]
        for j in range(SSD_CONV - 1):
            shifted = pltpu.roll(ext, SSD_CONV - 1 - j, 0)[CONV_HALO:]
            acc = acc + cw_ref[j:j + 1, lo:lo + width] * shifted
        return _silu(acc)

    dt = _softplus(dt_ref[...] + dtb_ref[...])
    cum = _cumsum_rows(dt * (-jnp.exp(alog_ref[...])))
    cum_t = cum.T
    cum_last = cum[L - 1:L, :]
    exp_cum = jnp.exp(cum)
    decay_to_end = jnp.exp(cum_last - cum)
    chunk_decay = jnp.exp(cum_last)
    row = lax.broadcasted_iota(jnp.int32, (L, L), 0)
    col = lax.broadcasted_iota(jnp.int32, (L, L), 1)
    causal = col <= row
    lane_gw = lax.broadcasted_iota(jnp.int32, (1, gw), 1)
    head_rows = [jnp.where(lane_gw // SSD_HEADDIM == r, 1.0, 0.0).astype(BF16)
                 for r in range(SSD_HEADS_PER_GROUP)]

    def expand(a):
        hi = a.astype(BF16)
        rest = a - hi.astype(F32)
        mid = rest.astype(BF16)
        lo = (rest - mid.astype(F32)).astype(BF16)
        return jnp.dot(jnp.concatenate([hi, mid, lo], axis=1), e3_ref[...], preferred_element_type=F32)

    dt_x = expand(dt)
    exp_cum_x = expand(exp_cum)
    decay_to_end_x = expand(decay_to_end)

    for g in range(SSD_GROUPS):
        cols = slice(g * gw, (g + 1) * gw)
        xs = conv_silu(g * gw, gw)
        bm = conv_silu(SSD_INNER + g * ns, ns).astype(BF16)
        cm = conv_silu(SSD_INNER + SSD_BC + g * ns, ns).astype(BF16)
        xdt = xs * dt_x[:, cols]
        s_in = s_ref[g]
        cb = lax.dot_general(cm, bm, _NT, preferred_element_type=F32)
        y = exp_cum_x[:, cols] * jnp.dot(cm, s_in.astype(BF16), preferred_element_type=F32)
        weights, inputs = [], []
        xdt_lo = xdt.astype(BF16)
        for r in range(SSD_HEADS_PER_GROUP):
            hh = SSD_HEADS_PER_GROUP * g + r
            decay = jnp.exp(jnp.where(causal, cum[:, hh:hh + 1] - cum_t[hh:hh + 1, :], -jnp.inf))
            weights.append((cb * decay).astype(BF16))
            inputs.append(xdt_lo * head_rows[r])
        y = y + jnp.dot(jnp.concatenate(weights, axis=1), jnp.concatenate(inputs, axis=0),
                        preferred_element_type=F32)
        xdte = (xdt * decay_to_end_x[:, cols]).astype(BF16)
        states = jnp.dot(bm.astype(F32).T.astype(BF16), xdte, preferred_element_type=F32)
        s_ref[g] = _expand_heads(chunk_decay, g, gw) * s_in + states
        y = y + dsk_ref[:, g * gw:(g + 1) * gw] * xs
        y = y * _silu(z_ref[:, g * gw:(g + 1) * gw].astype(F32))
        out_ref[:, g * gw:(g + 1) * gw] = _rms(y, ng_ref[:, g * gw:(g + 1) * gw]).astype(out_ref.dtype)

    ext_ref[0:CONV_HALO, :] = ext_ref[L:L + CONV_HALO, :]


def _ssd_core(proj, dt, conv_w, conv_b, dt_bias, a_log, d_skip, norm_g, bsz, seq):
    L = SSD_CHUNK
    nc = seq // L
    rowmap = lambda b, c: (b * nc + c, 0)
    const = lambda b, c: (0, 0)
    head_of_channel = jnp.arange(SSD_INNER, dtype=jnp.int32) // SSD_HEADDIM
    expand1 = (jnp.arange(LANES, dtype=jnp.int32)[:, None] == head_of_channel[None, :]).astype(BF16)
    expand3 = jnp.concatenate([expand1] * 3, axis=0)
    return pl.pallas_call(
        _ssd_body,
        grid=(bsz, nc),
        in_specs=[pl.BlockSpec((L, SSD_INNER), rowmap),
                  pl.BlockSpec((L, SSD_INNER), lambda b, c: (b * nc + c, 1)),
                  pl.BlockSpec((L, 2 * SSD_BC), lambda b, c: (b * nc + c, 2)),
                  pl.BlockSpec((L, LANES), rowmap),
                  pl.BlockSpec((SSD_CONV, SSD_CONV_DIM), const),
                  pl.BlockSpec((1, SSD_CONV_DIM), const),
                  pl.BlockSpec((1, LANES), const),
                  pl.BlockSpec((1, LANES), const),
                  pl.BlockSpec((1, SSD_INNER), const),
                  pl.BlockSpec((1, SSD_INNER), const),
                  pl.BlockSpec((3 * LANES, SSD_INNER), const)],
        out_specs=pl.BlockSpec((L, SSD_INNER), rowmap),
        out_shape=jax.ShapeDtypeStruct((bsz * seq, SSD_INNER), BF16),
        scratch_shapes=[pltpu.VMEM((L + CONV_HALO, SSD_CONV_DIM), F32),
                        pltpu.VMEM((SSD_GROUPS, SSD_STATE, SSD_GROUP_WIDTH), F32)],
        compiler_params=_params("parallel", "arbitrary"),
        name="ssd_core",
    )(proj, proj, proj, dt, conv_w, conv_b, dt_bias, a_log, d_skip, norm_g, expand3)


def _ssd_layer(x, prenormed, ln_g, w_in, conv_w, conv_b, dt_bias, a_log, d_skip, norm_g, bsz, seq):
    w_dt = _pad_cols(w_in[:, SSD_MAIN:], LANES).astype(BF16)
    proj, dt = _norm_proj(x, ln_g, w_in.astype(BF16), w_dt, SSD_MAIN, prenormed)
    pad_heads = lambda a: jnp.pad(a.astype(F32), (0, LANES - SSD_HEADS)).reshape(1, LANES)
    mixed = _ssd_core(proj, dt, conv_w.astype(F32), conv_b.astype(F32).reshape(1, SSD_CONV_DIM),
                      pad_heads(dt_bias), pad_heads(a_log),
                      jnp.repeat(d_skip.astype(F32), SSD_HEADDIM).reshape(1, SSD_INNER),
                      norm_g.astype(F32).reshape(1, SSD_INNER), bsz, seq)
    return mixed


def kernel(x, ln_mix, ln_mlp, w_up, w_down, ln_f, mlstm_w_in, mlstm_b_gate, mlstm_norm, mlstm_w_out, diff_w_in, diff_lam, diff_norm, diff_w_out, ssd_w_in, ssd_conv_w, ssd_conv_b, ssd_dt_bias, ssd_A_log, ssd_D, ssd_norm, ssd_w_out):
    bsz, seq, d = x.shape
    h = x.reshape(bsz * seq, d)
    w_up_lo, w_down_lo = w_up.astype(BF16), w_down.astype(BF16)
    xn = None
    for i in range(DEPTH):
        kind, j = i % N_MIXERS, i // N_MIXERS
        prenormed = xn is not None
        mixer_in = xn if prenormed else h
        if kind == 0:
            mixed = _mlstm_layer(mixer_in, prenormed, ln_mix[i], mlstm_w_in[j], mlstm_b_gate[j],
                                 mlstm_norm[j], bsz, seq)
            w_out = mlstm_w_out[j]
        elif kind == 1:
            assert prenormed, "the attention projection expects a pre-normalised input"
            mixed = _diff_layer(mixer_in, diff_w_in[j], diff_lam[j], diff_norm[j], bsz, seq, i)
            w_out = diff_w_out[j]
        else:
            mixed = _ssd_layer(mixer_in, prenormed, ln_mix[i], ssd_w_in[j], ssd_conv_w[j], ssd_conv_b[j],
                               ssd_dt_bias[j], ssd_A_log[j], ssd_D[j], ssd_norm[j], bsz, seq)
            w_out = ssd_w_out[j]
        last = i == DEPTH - 1
        out = _mix_mlp(mixed, w_out.astype(BF16), h, ln_mlp[i], w_up_lo, w_down_lo, i,
                       ln_f if last else ln_mix[i + 1], final_norm=last)
        h, xn = (out, None) if last else out
    return h.reshape(bsz, seq, d)
```

```python
import functools
import math

import jax
import jax.numpy as jnp
from jax import lax
from jax.experimental import pallas as pl
from jax.experimental.pallas import tpu as pltpu

F32 = jnp.float32
BF16 = jnp.bfloat16

DEPTH = 4
N_MIXERS = 3
EPS = 1e-6

MLSTM_HEADS = 4
MLSTM_DV = 256
MLSTM_DQK = 128
MLSTM_CHUNK = 128
GATE_SOFTCAP = 15.0

DIFF_HEADS = 8
DIFF_DH = 64
DIFF_DV = 128
ROPE_THETA = 10000.0

SSD_INNER = 2048
SSD_HEADDIM = 64
SSD_HEADS = 32
SSD_GROUPS = 8
SSD_STATE = 128
SSD_CONV = 4
SSD_CHUNK = 128
SSD_GROUP_WIDTH = SSD_INNER // SSD_GROUPS
SSD_HEADS_PER_GROUP = SSD_HEADS // SSD_GROUPS

LANES = 128
CONV_HALO = 8
VMEM_LIMIT = 48 * 1024 * 1024
VMEM_LIMIT_MLP = 56 * 1024 * 1024

_NT = (((1,), (1,)), ((), ()))


def _params(*sem):
    return pltpu.CompilerParams(dimension_semantics=sem, vmem_limit_bytes=VMEM_LIMIT)


def _rms(x, g):
    ms = jnp.mean(x * x, axis=-1, keepdims=True)
    return x * lax.rsqrt(ms + EPS) * g


def _cumsum_rows(x):
    n = x.shape[0]
    row = lax.broadcasted_iota(jnp.int32, x.shape, 0)
    s = 1
    while s < n:
        x = x + jnp.where(row >= s, pltpu.roll(x, s, 0), 0.0)
        s *= 2
    return x


def _cummax_rows(x):
    n = x.shape[0]
    row = lax.broadcasted_iota(jnp.int32, x.shape, 0)
    s = 1
    while s < n:
        x = jnp.maximum(x, jnp.where(row >= s, pltpu.roll(x, s, 0), -jnp.inf))
        s *= 2
    return x


COL_REDUCE_SLAB = 64


def _col_reduce(x, pair_op, reduce_fn):
    slab = min(COL_REDUCE_SLAB, x.shape[0])
    acc = x[:slab]
    for r0 in range(slab, x.shape[0], slab):
        acc = pair_op(acc, x[r0:r0 + slab])
    while acc.shape[0] > 8:
        half = acc.shape[0] // 2
        acc = pair_op(acc[:half], acc[half:])
    return reduce_fn(acc, axis=0, keepdims=True)


def _sigmoid(x):
    return 0.5 + 0.5 * jnp.tanh(0.5 * x)


def _silu(x):
    half = 0.5 * x
    return half + half * jnp.tanh(half)


def _softplus(x):
    return jnp.maximum(x, 0.0) + jnp.log1p(jnp.exp(-jnp.abs(x)))


def _pad_cols(w, n):
    return jnp.pad(w, ((0, 0), (0, n - w.shape[1])))


def _norm_proj_body(x_ref, g_ref, w_ref, ws_ref, o_ref, os_ref, *scratch):
    first = pl.program_id(1) == 0
    if scratch:
        xn_ref, = scratch

        @pl.when(first)
        def _():
            xn_ref[...] = _rms(x_ref[...], g_ref[...]).astype(BF16)
    else:
        xn_ref = x_ref

    @pl.when(first)
    def _():
        os_ref[...] = jnp.dot(xn_ref[...], ws_ref[...], preferred_element_type=F32)

    o_ref[...] = jnp.dot(xn_ref[...], w_ref[...], preferred_element_type=F32).astype(o_ref.dtype)


def _norm_proj(x, g, w, w_side, n_main, prenormed, tn=1536, tm=1024):
    n, d = x.shape
    tm = min(tm, n)
    return pl.pallas_call(
        _norm_proj_body,
        grid=(n // tm, n_main // tn),
        in_specs=[pl.BlockSpec((tm, d), lambda i, j: (i, 0)),
                  pl.BlockSpec((1, d), lambda i, j: (0, 0)),
                  pl.BlockSpec((d, tn), lambda i, j: (0, j)),
                  pl.BlockSpec((d, LANES), lambda i, j: (0, 0))],
        out_specs=[pl.BlockSpec((tm, tn), lambda i, j: (i, j)),
                   pl.BlockSpec((tm, LANES), lambda i, j: (i, 0))],
        out_shape=[jax.ShapeDtypeStruct((n, n_main), BF16), jax.ShapeDtypeStruct((n, LANES), F32)],
        scratch_shapes=[] if prenormed else [pltpu.VMEM((tm, d), BF16)],
        compiler_params=_params("parallel", "arbitrary"),
        name="norm_proj",
    )(x, g.reshape(1, d), w, w_side)


ROPE_ROW_SPLIT = 4


def _rope_proj_body(xn_ref, w_ref, cos_ref, sin_ref, o_ref, vt_ref):
    j = pl.program_id(1)
    rows = xn_ref.shape[0] // ROPE_ROW_SPLIT

    @pl.when(j < 2)
    def _():
        qscale = jnp.where(j == 0, DIFF_DH ** -0.5 * math.log2(math.e), 1.0).astype(F32)
        for r in range(ROPE_ROW_SPLIT):
            slab = slice(r * rows, (r + 1) * rows)
            acc = jnp.dot(xn_ref[slab, :], w_ref[...], preferred_element_type=F32)
            cos = cos_ref[slab, :] * qscale
            sin = sin_ref[slab, :] * qscale
            for c in range(acc.shape[1] // LANES):
                xc = acc[:, c * LANES:(c + 1) * LANES]
                rot = pltpu.roll(xc, LANES // 2, 1)
                o_ref[slab, c * LANES:(c + 1) * LANES] = (xc * cos + rot * sin).astype(o_ref.dtype)

    @pl.when(j >= 2)
    def _():
        for r in range(ROPE_ROW_SPLIT):
            slab = slice(r * rows, (r + 1) * rows)
            acc = jnp.dot(xn_ref[slab, :], w_ref[...], preferred_element_type=F32)
            vt_ref[:, slab] = acc.T.astype(vt_ref.dtype)


def _rope_proj(xn, w, cos, sin, bsz, seq, tm=1024):
    n, d = xn.shape
    tm = min(tm, seq)
    tn = d
    nseq = seq // tm
    return pl.pallas_call(
        _rope_proj_body,
        grid=(n // tm, 3),
        in_specs=[pl.BlockSpec((tm, d), lambda i, j: (i, 0)),
                  pl.BlockSpec((d, tn), lambda i, j: (0, j)),
                  pl.BlockSpec((tm, LANES), lambda i, j: (i % nseq, 0)),
                  pl.BlockSpec((tm, LANES), lambda i, j: (i % nseq, 0))],
        out_specs=[pl.BlockSpec((tm, tn), lambda i, j: (i, jnp.minimum(j, 1))),
                   pl.BlockSpec((d, tm), lambda i, j: (i // nseq, i % nseq))],
        out_shape=[jax.ShapeDtypeStruct((n, 2 * d), BF16), jax.ShapeDtypeStruct((bsz * d, seq), BF16)],
        compiler_params=_params("parallel", "arbitrary"),
        name="rope_proj",
    )(xn, w, cos, sin)


MLP_ROW_SPLIT = 4


def _mix_mlp_body(a_ref, wo_ref, h_ref, g_ref, wu_ref, wd_ref, gn_ref, o_ref, *rest, final_norm):
    xn_ref = rest[-1]
    j = pl.program_id(1)

    rows = o_ref.shape[0] // MLP_ROW_SPLIT
    slabs = [slice(r * rows, (r + 1) * rows) for r in range(MLP_ROW_SPLIT)]

    @pl.when(j == 0)
    def _():
        for slab in slabs:
            h1 = h_ref[slab, :] + jnp.dot(a_ref[slab, :], wo_ref[...], preferred_element_type=F32)
            o_ref[slab, :] = h1
            xn_ref[slab, :] = _rms(h1, g_ref[...]).astype(BF16)

    def up_proj():
        up = jnp.dot(xn_ref[...], wu_ref[...], preferred_element_type=F32)
        return jnp.square(jnp.maximum(up, 0.0)).astype(BF16)

    last = j == pl.num_programs(1) - 1

    @pl.when(jnp.logical_not(last))
    def _():
        o_ref[...] += jnp.dot(up_proj(), wd_ref[...], preferred_element_type=F32)

    @pl.when(last)
    def _():
        up = up_proj()
        for slab in slabs:
            total = o_ref[slab, :] + jnp.dot(up[slab, :], wd_ref[...], preferred_element_type=F32)
            normed = _rms(total, gn_ref[...])
            if final_norm:
                o_ref[slab, :] = normed
            else:
                o_ref[slab, :] = total
                rest[0][slab, :] = normed.astype(BF16)


def _mix_mlp(a, w_out, h, g, w_up, w_down, layer, g_next, final_norm, tm=1024, tf=1024):
    n, d = h.shape
    k = a.shape[1]
    dff = w_up.shape[2]
    tm = min(tm, n)
    row = lambda i, j: (i, 0)
    const = lambda i, j: (0, 0)
    return pl.pallas_call(
        functools.partial(_mix_mlp_body, final_norm=final_norm),
        grid=(n // tm, dff // tf),
        in_specs=[pl.BlockSpec((tm, k), row),
                  pl.BlockSpec((k, d), const),
                  pl.BlockSpec((tm, d), row),
                  pl.BlockSpec((1, d), const),
                  pl.BlockSpec((None, d, tf), lambda i, j: (layer, 0, j)),
                  pl.BlockSpec((None, tf, d), lambda i, j: (layer, j, 0)),
                  pl.BlockSpec((1, d), const)],
        out_specs=pl.BlockSpec((tm, d), row) if final_norm else [pl.BlockSpec((tm, d), row)] * 2,
        out_shape=(jax.ShapeDtypeStruct((n, d), F32) if final_norm else
                   [jax.ShapeDtypeStruct((n, d), F32), jax.ShapeDtypeStruct((n, d), BF16)]),
        scratch_shapes=[pltpu.VMEM((tm, d), BF16)],
        compiler_params=pltpu.CompilerParams(dimension_semantics=("parallel", "arbitrary"),
                                             vmem_limit_bytes=VMEM_LIMIT_MLP),
        name="mix_mlp",
    )(a, w_out, h, g.reshape(1, d), w_up, w_down, g_next.reshape(1, d))


MLSTM_QK = MLSTM_HEADS * MLSTM_DQK
MLSTM_MAIN = 2 * MLSTM_QK + 2 * MLSTM_HEADS * MLSTM_DV


def _mlstm_body(p_ref, gate_ref, bias_ref, ng_ref, out_ref, *state_refs):
    L = MLSTM_CHUNK
    nh = MLSTM_HEADS
    n_batch = p_ref.shape[0]
    n_chain = n_batch * nh
    c_refs = state_refs[:n_chain]
    n_refs = state_refs[n_chain:2 * n_chain]
    m_refs = state_refs[2 * n_chain:]

    @pl.when(pl.program_id(0) == 0)
    def _():
        for ref in state_refs:
            ref[...] = jnp.zeros_like(ref)

    row = lax.broadcasted_iota(jnp.int32, (L, L), 0)
    col = lax.broadcasted_iota(jnp.int32, (L, L), 1)
    causal = col <= row
    qscale = MLSTM_DQK ** -0.5

    for bi in range(n_batch):
        gates = gate_ref[bi] + bias_ref[...]
        capped = GATE_SOFTCAP * jnp.tanh(gates / GATE_SOFTCAP)
        log_f = -_softplus(-capped)
        log_i = pltpu.roll(capped, nh, 1)
        b = _cumsum_rows(log_f)
        g = log_i - b
        g_max = _cummax_rows(g)
        m_in = m_refs[bi][...]
        m_run = jnp.maximum(m_in, g_max)
        m_t = b + m_run
        scale_all = jnp.exp(m_in - m_run)
        floor_all = jnp.exp(-m_t)
        b_last = b[L - 1:L, :]
        g_max_last = g_max[L - 1:L, :]
        m_run_last = m_run[L - 1:L, :]
        w_all = jnp.exp(g - g_max_last)
        a_old_all = jnp.exp(m_in - m_run_last)
        a_loc_all = jnp.exp(g_max_last - m_run_last)
        m_refs[bi][...] = b_last + m_run_last
        g_t = g.T

        for h in range(nh):
            st = bi * nh + h
            ln = nh + h
            v_off = 2 * MLSTM_QK + h * MLSTM_DV
            o_off = v_off + nh * MLSTM_DV
            q = p_ref[bi, :, h * MLSTM_DQK:(h + 1) * MLSTM_DQK]
            k = p_ref[bi, :, MLSTM_QK + h * MLSTM_DQK:MLSTM_QK + (h + 1) * MLSTM_DQK]
            v = p_ref[bi, :, v_off:v_off + MLSTM_DV]
            c_in = c_refs[st][...]
            n_in = n_refs[st][...]

            decay = jnp.exp(jnp.where(causal, g_t[ln:ln + 1, :] - m_run[:, ln:ln + 1], -jnp.inf))
            qk = lax.dot_general(q, k, _NT, preferred_element_type=F32) * qscale
            p = decay * qk
            scale = scale_all[:, ln:ln + 1]
            q_c = jnp.dot(q, c_in.astype(BF16), preferred_element_type=F32) * qscale
            num = jnp.dot(p.astype(BF16), v, preferred_element_type=F32) + scale * q_c
            q_n = jnp.sum(q.astype(F32) * n_in, axis=1, keepdims=True) * qscale
            den = jnp.sum(p, axis=1, keepdims=True) + scale * q_n
            hs = num / jnp.maximum(jnp.abs(den), floor_all[:, ln:ln + 1])

            kw = k.astype(F32) * w_all[:, ln:ln + 1]
            c_loc = jnp.dot(kw.T.astype(BF16), v, preferred_element_type=F32)
            n_loc = jnp.sum(kw, axis=0, keepdims=True)
            a_old = a_old_all[:, ln:ln + 1]
            a_loc = a_loc_all[:, ln:ln + 1]
            c_refs[st][...] = a_old * c_in + a_loc * c_loc
            n_refs[st][...] = a_old * n_in + a_loc * n_loc

            hn = _rms(hs, ng_ref[:, h * MLSTM_DV:(h + 1) * MLSTM_DV])
            o_gate = _sigmoid(p_ref[bi, :, o_off:o_off + MLSTM_DV].astype(F32))
            out_ref[bi, :, h * MLSTM_DV:(h + 1) * MLSTM_DV] = (o_gate * hn).astype(out_ref.dtype)


def _mlstm_core(proj, gates, bias, norm_g, bsz, seq):
    L = MLSTM_CHUNK
    d = MLSTM_HEADS * MLSTM_DV
    n_chain = bsz * MLSTM_HEADS
    chunk = lambda c: (0, c, 0)
    return pl.pallas_call(
        _mlstm_body,
        grid=(seq // L,),
        in_specs=[pl.BlockSpec((bsz, L, MLSTM_MAIN), chunk),
                  pl.BlockSpec((bsz, L, LANES), chunk),
                  pl.BlockSpec((1, LANES), lambda c: (0, 0)),
                  pl.BlockSpec((1, d), lambda c: (0, 0))],
        out_specs=pl.BlockSpec((bsz, L, d), chunk),
        out_shape=jax.ShapeDtypeStruct((bsz, seq, d), BF16),
        scratch_shapes=([pltpu.VMEM((MLSTM_DQK, MLSTM_DV), F32)] * n_chain
                        + [pltpu.VMEM((1, MLSTM_DQK), F32)] * n_chain
                        + [pltpu.VMEM((1, LANES), F32)] * bsz),
        compiler_params=_params("arbitrary"),
        name="mlstm_core",
    )(proj.reshape(bsz, seq, MLSTM_MAIN), gates.reshape(bsz, seq, LANES), bias, norm_g.reshape(1, d))


def _mlstm_layer(x, prenormed, ln_g, w_in, b_gate, norm_g, bsz, seq):
    w_gate = _pad_cols(w_in[:, MLSTM_MAIN:], LANES).astype(BF16)
    bias = jnp.pad(b_gate.astype(F32), (0, LANES - b_gate.shape[0])).reshape(1, LANES)
    proj, gates = _norm_proj(x, ln_g, w_in.astype(BF16), w_gate, MLSTM_MAIN, prenormed)
    mixed = _mlstm_core(proj, gates, bias, norm_g, bsz, seq)
    return mixed.reshape(bsz * seq, -1)


ATTN_PAIRS_PER_TRIP = 4


def _diff_attn_body(lam_ref, g_ref, q_ref, qn_ref, k_ref, vt_ref, o_ref, sc_a, sc_b, sc_c, cm_a, cm_b, cm_c,
                    m_ref, l_ref, acc_ref, *, tq, lam_init):
    qi = pl.program_id(2)
    q = q_ref[...]
    lane = lax.broadcasted_iota(jnp.int32, q.shape, 1)
    zero = jnp.zeros_like(q)
    is_c1 = (lane & (DIFF_DH // 2)) != 0
    th = tq // 2
    chains = [(c, half) for c in range(2) for half in range(2)]

    def chain_queries(qblk):
        parts = (jnp.where(is_c1, zero, qblk), jnp.where(is_c1, qblk, zero))
        return [parts[c][half * th:(half + 1) * th] for c, half in chains]

    q_ch = chain_queries(q)
    kpos = lax.broadcasted_iota(jnp.int32, (tq, th), 0)
    qpos = lax.broadcasted_iota(jnp.int32, (tq, th), 1)
    causal = [kpos <= qpos + half * th for half in range(2)]

    buf_a, buf_b, buf_c = (sc_a, cm_a), (sc_b, cm_b), (sc_c, cm_c)

    def put_scores(buf, ki, queries=q_ch):
        sc_ref, cm_ref = buf
        k = k_ref[pl.ds(pl.multiple_of(ki * tq, tq), tq), :]
        for x in range(4):
            s = lax.dot_general(k, queries[x], _NT, preferred_element_type=F32)
            sc_ref[x] = s
            cm_ref[x] = _col_reduce(s, jnp.maximum, jnp.max)

    def softmax_pv(ki, buf, masked):
        sc_ref, cm_ref = buf
        vt = vt_ref[:, pl.ds(pl.multiple_of(ki * tq, tq), tq)]
        for x, (c, half) in enumerate(chains):
            def load_scores():
                s = sc_ref[x]
                return jnp.where(causal[half], s, -jnp.inf) if masked else s

            m = m_ref[x]
            col_max = _col_reduce(load_scores(), jnp.maximum, jnp.max) if masked else cm_ref[x]
            m_new = jnp.maximum(m, col_max)
            alpha = jnp.exp2(m - m_new)
            p = jnp.exp2(load_scores() - m_new)
            m_ref[x] = m_new
            l_ref[x] = alpha * l_ref[x] + _col_reduce(p, jnp.add, jnp.sum)
            acc_ref[x] = alpha * acc_ref[x] + jnp.dot(vt, p.astype(BF16), preferred_element_type=F32)

    m_ref[...] = jnp.full(m_ref.shape, -jnp.inf, F32)
    l_ref[...] = jnp.zeros_like(l_ref)
    acc_ref[...] = jnp.zeros_like(acc_ref)

    odd = qi & 1

    @pl.when(qi == 0)
    def _():
        put_scores(buf_a, 0)

    @pl.when(odd == 1)
    def _():
        put_scores(buf_a, 1)
        softmax_pv(0, buf_c, False)

    @pl.when((qi > 0) & (odd == 0))
    def _():
        put_scores(buf_b, 1)
        softmax_pv(0, buf_c, False)
        put_scores(buf_a, 2)
        softmax_pv(1, buf_b, False)

    first = 2 - odd

    def pair(b0):
        put_scores(buf_b, b0 + 1)
        softmax_pv(b0, buf_a, False)
        put_scores(buf_a, b0 + 2)
        softmax_pv(b0 + 1, buf_b, False)

    n_pairs = lax.shift_right_logical(jnp.maximum(qi - first, 0), 1)
    done = 0
    group = 1
    while group < ATTN_PAIRS_PER_TRIP:
        start = first + 2 * done

        @pl.when((n_pairs & group) != 0)
        def _(start=start, group=group):
            for g in range(group):
                pair(start + 2 * g)

        done = done + (n_pairs & group)
        group *= 2

    def trip(j, carry):
        b0 = first + 2 * done + 2 * ATTN_PAIRS_PER_TRIP * j
        for g in range(ATTN_PAIRS_PER_TRIP):
            pair(b0 + 2 * g)
        return carry

    lax.fori_loop(0, lax.shift_right_logical(n_pairs, ATTN_PAIRS_PER_TRIP.bit_length() - 1), trip, 0)

    put_scores(buf_c, 0, chain_queries(qn_ref[...]))
    softmax_pv(qi, buf_a, True)
    a0, a1 = (jnp.concatenate([acc_ref[2 * c] / l_ref[2 * c], acc_ref[2 * c + 1] / l_ref[2 * c + 1]], axis=1)
              for c in range(2))

    lam = lam_ref[...]
    lam_val = (jnp.exp(jnp.sum(lam[0:1] * lam[1:2], axis=1, keepdims=True))
               - jnp.exp(jnp.sum(lam[2:3] * lam[3:4], axis=1, keepdims=True)) + lam_init)
    o = a0 - lam_val * a1
    ms = jnp.mean(o * o, axis=0, keepdims=True)
    y = o * lax.rsqrt(ms + EPS) * g_ref[...] * (1.0 - lam_init)
    o_ref[...] = y.T.astype(o_ref.dtype)


def _diff_attn(qk, v_t, lam, norm_g, bsz, seq, layer_idx, tq=512):
    tq = min(tq, seq)
    nq = seq // tq
    nh = DIFF_HEADS
    lam_init = 0.8 - 0.6 * math.exp(-0.3 * layer_idx)
    return pl.pallas_call(
        functools.partial(_diff_attn_body, tq=tq, lam_init=lam_init),
        grid=(bsz, nh, nq),
        in_specs=[pl.BlockSpec((4, DIFF_DH), lambda b, h, i: (0, 0)),
                  pl.BlockSpec((DIFF_DV, 1), lambda b, h, i: (0, 0)),
                  pl.BlockSpec((tq, DIFF_DV), lambda b, h, i: (b * nq + i, h)),
                  pl.BlockSpec((tq, DIFF_DV), lambda b, h, i: (b * nq + jnp.minimum(i + 1, nq - 1), h)),
                  pl.BlockSpec((seq, DIFF_DV), lambda b, h, i: (b, nh + h)),
                  pl.BlockSpec((DIFF_DV, seq), lambda b, h, i: (b * nh + h, 0))],
        out_specs=pl.BlockSpec((tq, DIFF_DV), lambda b, h, i: (b * nq + i, h)),
        out_shape=jax.ShapeDtypeStruct((bsz * seq, nh * DIFF_DV), BF16),
        scratch_shapes=[pltpu.VMEM((4, tq, tq // 2), F32)] * 3 + [pltpu.VMEM((4, 1, tq // 2), F32)] * 3 + [
                        pltpu.VMEM((4, 1, tq // 2), F32), pltpu.VMEM((4, 1, tq // 2), F32),
                        pltpu.VMEM((4, DIFF_DV, tq // 2), F32)],
        compiler_params=_params("arbitrary", "arbitrary", "arbitrary"),
        name="diff_attn",
    )(lam.astype(F32), norm_g.reshape(DIFF_DV, 1).astype(F32), qk, qk, qk, v_t)


def _rope_tables(seq):
    inv = ROPE_THETA ** (-jnp.arange(0, DIFF_DH, 2, dtype=F32) / DIFF_DH)
    ang = jnp.arange(seq, dtype=F32)[:, None] * inv[None, :]
    cos, sin = jnp.cos(ang), jnp.sin(ang)
    cos_t = jnp.tile(cos, (1, LANES // cos.shape[1]))
    sin_t = jnp.concatenate([-sin, -sin, sin, sin], axis=1)
    return cos_t, sin_t


def _rope_lane_order():
    half = DIFF_DH // 2
    return [c * DIFF_DH + part * half + i for part in range(2) for c in range(2) for i in range(half)]


def _diff_layer(xn, w_in, lam, norm_g, bsz, seq, layer_idx):
    cos, sin = _rope_tables(seq)
    order = jnp.asarray(_rope_lane_order(), jnp.int32)
    nqk = 2 * DIFF_HEADS * DIFF_DV
    qk_cols = (jnp.arange(nqk // LANES, dtype=jnp.int32) * LANES)[:, None] + order[None, :]
    cols = jnp.concatenate([qk_cols.reshape(-1), jnp.arange(nqk, w_in.shape[1], dtype=jnp.int32)])
    qk, v_t = _rope_proj(xn, w_in.astype(BF16)[:, cols], cos, sin, bsz, seq)
    return _diff_attn(qk, v_t, lam, norm_g, bsz, seq, layer_idx)


SSD_BC = SSD_GROUPS * SSD_STATE
SSD_CONV_DIM = SSD_INNER + 2 * SSD_BC
SSD_MAIN = SSD_INNER + SSD_CONV_DIM


def _expand_heads(a, g, width):
    nr = SSD_HEADS_PER_GROUP
    lane = lax.broadcasted_iota(jnp.int32, (a.shape[0], width), 1)
    out = jnp.broadcast_to(a[:, nr * g + nr - 1:nr * g + nr], (a.shape[0], width))
    for r in range(nr - 2, -1, -1):
        out = jnp.where(lane < SSD_HEADDIM * (r + 1), a[:, nr * g + r:nr * g + r + 1], out)
    return out


def _ssd_body(z_ref, x_ref, bc_ref, dt_ref, cw_ref, cb_ref, dtb_ref, alog_ref, dsk_ref, ng_ref, e3_ref,
              out_ref, ext_ref, s_ref):
    L = SSD_CHUNK
    gw = SSD_GROUP_WIDTH
    ns = SSD_STATE

    @pl.when(pl.program_id(1) == 0)
    def _():
        ext_ref[0:CONV_HALO, :] = jnp.zeros((CONV_HALO, SSD_CONV_DIM), F32)
        s_ref[...] = jnp.zeros_like(s_ref)

    ext_ref[CONV_HALO:CONV_HALO + L, 0:SSD_INNER] = x_ref[...].astype(F32)
    ext_ref[CONV_HALO:CONV_HALO + L, SSD_INNER:SSD_CONV_DIM] = bc_ref[...].astype(F32)

    def conv_silu(lo, width):
        ext = ext_ref[:, lo:lo + width]
        acc = cb_ref[:, lo:lo + width] + cw_ref[SSD_CONV - 1:SSD_CONV, lo:lo + width] * ext[CONV_HALO:]
        for j in range(SSD_CONV - 1):
            shifted = pltpu.roll(ext, SSD_CONV - 1 - j, 0)[CONV_HALO:]
            acc = acc + cw_ref[j:j + 1, lo:lo + width] * shifted
        return _silu(acc)

    dt = _softplus(dt_ref[...] + dtb_ref[...])
    cum = _cumsum_rows(dt * (-jnp.exp(alog_ref[...])))
    cum_t = cum.T
    cum_last = cum[L - 1:L, :]
    exp_cum = jnp.exp(cum)
    decay_to_end = jnp.exp(cum_last - cum)
    chunk_decay = jnp.exp(cum_last)
    row = lax.broadcasted_iota(jnp.int32, (L, L), 0)
    col = lax.broadcasted_iota(jnp.int32, (L, L), 1)
    causal = col <= row
    lane_gw = lax.broadcasted_iota(jnp.int32, (1, gw), 1)
    head_rows = [jnp.where(lane_gw // SSD_HEADDIM == r, 1.0, 0.0).astype(BF16)
                 for r in range(SSD_HEADS_PER_GROUP)]

    def expand(a):
        hi = a.astype(BF16)
        rest = a - hi.astype(F32)
        mid = rest.astype(BF16)
        lo = (rest - mid.astype(F32)).astype(BF16)
        return jnp.dot(jnp.concatenate([hi, mid, lo], axis=1), e3_ref[...], preferred_element_type=F32)

    dt_x = expand(dt)
    exp_cum_x = expand(exp_cum)
    decay_to_end_x = expand(decay_to_end)

    for g in range(SSD_GROUPS):
        cols = slice(g * gw, (g + 1) * gw)
        xs = conv_silu(g * gw, gw)
        bm = conv_silu(SSD_INNER + g * ns, ns).astype(BF16)
        cm = conv_silu(SSD_INNER + SSD_BC + g * ns, ns).astype(BF16)
        xdt = xs * dt_x[:, cols]
        s_in = s_ref[g]
        cb = lax.dot_general(cm, bm, _NT, preferred_element_type=F32)
        y = exp_cum_x[:, cols] * jnp.dot(cm, s_in.astype(BF16), preferred_element_type=F32)
        weights, inputs = [], []
        xdt_lo = xdt.astype(BF16)
        for r in range(SSD_HEADS_PER_GROUP):
            hh = SSD_HEADS_PER_GROUP * g + r
            decay = jnp.exp(jnp.where(causal, cum[:, hh:hh + 1] - cum_t[hh:hh + 1, :], -jnp.inf))
            weights.append((cb * decay).astype(BF16))
            inputs.append(xdt_lo * head_rows[r])
        y = y + jnp.dot(jnp.concatenate(weights, axis=1), jnp.concatenate(inputs, axis=0),
                        preferred_element_type=F32)
        xdte = (xdt * decay_to_end_x[:, cols]).astype(BF16)
        states = jnp.dot(bm.astype(F32).T.astype(BF16), xdte, preferred_element_type=F32)
        s_ref[g] = _expand_heads(chunk_decay, g, gw) * s_in + states
        y = y + dsk_ref[:, g * gw:(g + 1) * gw] * xs
        y = y * _silu(z_ref[:, g * gw:(g + 1) * gw].astype(F32))
        out_ref[:, g * gw:(g + 1) * gw] = _rms(y, ng_ref[:, g * gw:(g + 1) * gw]).astype(out_ref.dtype)

    ext_ref[0:CONV_HALO, :] = ext_ref[L:L + CONV_HALO, :]


def _ssd_core(proj, dt, conv_w, conv_b, dt_bias, a_log, d_skip, norm_g, bsz, seq):
    L = SSD_CHUNK
    nc = seq // L
    rowmap = lambda b, c: (b * nc + c, 0)
    const = lambda b, c: (0, 0)
    head_of_channel = jnp.arange(SSD_INNER, dtype=jnp.int32) // SSD_HEADDIM
    expand1 = (jnp.arange(LANES, dtype=jnp.int32)[:, None] == head_of_channel[None, :]).astype(BF16)
    expand3 = jnp.concatenate([expand1] * 3, axis=0)
    return pl.pallas_call(
        _ssd_body,
        grid=(bsz, nc),
        in_specs=[pl.BlockSpec((L, SSD_INNER), rowmap),
                  pl.BlockSpec((L, SSD_INNER), lambda b, c: (b * nc + c, 1)),
                  pl.BlockSpec((L, 2 * SSD_BC), lambda b, c: (b * nc + c, 2)),
                  pl.BlockSpec((L, LANES), rowmap),
                  pl.BlockSpec((SSD_CONV, SSD_CONV_DIM), const),
                  pl.BlockSpec((1, SSD_CONV_DIM), const),
                  pl.BlockSpec((1, LANES), const),
                  pl.BlockSpec((1, LANES), const),
                  pl.BlockSpec((1, SSD_INNER), const),
                  pl.BlockSpec((1, SSD_INNER), const),
                  pl.BlockSpec((3 * LANES, SSD_INNER), const)],
        out_specs=pl.BlockSpec((L, SSD_INNER), rowmap),
        out_shape=jax.ShapeDtypeStruct((bsz * seq, SSD_INNER), BF16),
        scratch_shapes=[pltpu.VMEM((L + CONV_HALO, SSD_CONV_DIM), F32),
                        pltpu.VMEM((SSD_GROUPS, SSD_STATE, SSD_GROUP_WIDTH), F32)],
        compiler_params=_params("parallel", "arbitrary"),
        name="ssd_core",
    )(proj, proj, proj, dt, conv_w, conv_b, dt_bias, a_log, d_skip, norm_g, expand3)


def _ssd_layer(x, prenormed, ln_g, w_in, conv_w, conv_b, dt_bias, a_log, d_skip, norm_g, bsz, seq):
    w_dt = _pad_cols(w_in[:, SSD_MAIN:], LANES).astype(BF16)
    proj, dt = _norm_proj(x, ln_g, w_in.astype(BF16), w_dt, SSD_MAIN, prenormed)
    pad_heads = lambda a: jnp.pad(a.astype(F32), (0, LANES - SSD_HEADS)).reshape(1, LANES)
    mixed = _ssd_core(proj, dt, conv_w.astype(F32), conv_b.astype(F32).reshape(1, SSD_CONV_DIM),
                      pad_heads(dt_bias), pad_heads(a_log),
                      jnp.repeat(d_skip.astype(F32), SSD_HEADDIM).reshape(1, SSD_INNER),
                      norm_g.astype(F32).reshape(1, SSD_INNER), bsz, seq)
    return mixed


def kernel(x, ln_mix, ln_mlp, w_up, w_down, ln_f, mlstm_w_in, mlstm_b_gate, mlstm_norm, mlstm_w_out, diff_w_in, diff_lam, diff_norm, diff_w_out, ssd_w_in, ssd_conv_w, ssd_conv_b, ssd_dt_bias, ssd_A_log, ssd_D, ssd_norm, ssd_w_out):
    bsz, seq, d = x.shape
    h = x.reshape(bsz * seq, d)
    w_up_lo, w_down_lo = w_up.astype(BF16), w_down.astype(BF16)
    xn = None
    for i in range(DEPTH):
        kind, j = i % N_MIXERS, i // N_MIXERS
        prenormed = xn is not None
        mixer_in = xn if prenormed else h
        if kind == 0:
            mixed = _mlstm_layer(mixer_in, prenormed, ln_mix[i], mlstm_w_in[j], mlstm_b_gate[j],
                                 mlstm_norm[j], bsz, seq)
            w_out = mlstm_w_out[j]
        elif kind == 1:
            assert prenormed, "the attention projection expects a pre-normalised input"
            mixed = _diff_layer(mixer_in, diff_w_in[j], diff_lam[j], diff_norm[j], bsz, seq, i)
            w_out = diff_w_out[j]
        else:
            mixed = _ssd_layer(mixer_in, prenormed, ln_mix[i], ssd_w_in[j], ssd_conv_w[j], ssd_conv_b[j],
                               ssd_dt_bias[j], ssd_A_log[j], ssd_D[j], ssd_norm[j], bsz, seq)
            w_out = ssd_w_out[j]
        last = i == DEPTH - 1
        out = _mix_mlp(mixed, w_out.astype(BF16), h, ln_mlp[i], w_up_lo, w_down_lo, i,
                       ln_f if last else ln_mix[i + 1], final_norm=last)
        h, xn = (out, None) if last else out
    return h.reshape(bsz, seq, d)
```

```python
import functools
import math

import jax
import jax.numpy as jnp
from jax import lax
from jax.experimental import pallas as pl
from jax.experimental.pallas import tpu as pltpu

F32 = jnp.float32
BF16 = jnp.bfloat16

DEPTH = 4
N_MIXERS = 3
EPS = 1e-6

MLSTM_HEADS = 4
MLSTM_DV = 256
MLSTM_DQK = 128
MLSTM_CHUNK = 128
GATE_SOFTCAP = 15.0

DIFF_HEADS = 8
DIFF_DH = 64
DIFF_DV = 128
ROPE_THETA = 10000.0

SSD_INNER = 2048
SSD_HEADDIM = 64
SSD_HEADS = 32
SSD_GROUPS = 8
SSD_STATE = 128
SSD_CONV = 4
SSD_CHUNK = 128
SSD_GROUP_WIDTH = SSD_INNER // SSD_GROUPS
SSD_HEADS_PER_GROUP = SSD_HEADS // SSD_GROUPS

LANES = 128
CONV_HALO = 8
VMEM_LIMIT = 48 * 1024 * 1024
VMEM_LIMIT_MLP = 56 * 1024 * 1024

_NT = (((1,), (1,)), ((), ()))


def _params(*sem):
    return pltpu.CompilerParams(dimension_semantics=sem, vmem_limit_bytes=VMEM_LIMIT)


def _rms(x, g):
    ms = jnp.mean(x * x, axis=-1, keepdims=True)
    return x * lax.rsqrt(ms + EPS) * g


def _cumsum_rows(x):
    n = x.shape[0]
    row = lax.broadcasted_iota(jnp.int32, x.shape, 0)
    s = 1
    while s < n:
        x = x + jnp.where(row >= s, pltpu.roll(x, s, 0), 0.0)
        s *= 2
    return x


def _cummax_rows(x):
    n = x.shape[0]
    row = lax.broadcasted_iota(jnp.int32, x.shape, 0)
    s = 1
    while s < n:
        x = jnp.maximum(x, jnp.where(row >= s, pltpu.roll(x, s, 0), -jnp.inf))
        s *= 2
    return x


COL_REDUCE_SLAB = 64


def _col_reduce(x, pair_op, reduce_fn):
    slab = min(COL_REDUCE_SLAB, x.shape[0])
    acc = x[:slab]
    for r0 in range(slab, x.shape[0], slab):
        acc = pair_op(acc, x[r0:r0 + slab])
    while acc.shape[0] > 8:
        half = acc.shape[0] // 2
        acc = pair_op(acc[:half], acc[half:])
    return reduce_fn(acc, axis=0, keepdims=True)


def _sigmoid(x):
    return 0.5 + 0.5 * jnp.tanh(0.5 * x)


def _silu(x):
    half = 0.5 * x
    return half + half * jnp.tanh(half)


def _softplus(x):
    return jnp.maximum(x, 0.0) + jnp.log1p(jnp.exp(-jnp.abs(x)))


def _pad_cols(w, n):
    return jnp.pad(w, ((0, 0), (0, n - w.shape[1])))


def _norm_proj_body(x_ref, g_ref, w_ref, ws_ref, o_ref, os_ref, *scratch):
    first = pl.program_id(1) == 0
    if scratch:
        xn_ref, = scratch

        @pl.when(first)
        def _():
            xn_ref[...] = _rms(x_ref[...], g_ref[...]).astype(BF16)
    else:
        xn_ref = x_ref

    @pl.when(first)
    def _():
        os_ref[...] = jnp.dot(xn_ref[...], ws_ref[...], preferred_element_type=F32)

    o_ref[...] = jnp.dot(xn_ref[...], w_ref[...], preferred_element_type=F32).astype(o_ref.dtype)


def _norm_proj(x, g, w, w_side, n_main, prenormed, tn=1536, tm=1024):
    n, d = x.shape
    tm = min(tm, n)
    return pl.pallas_call(
        _norm_proj_body,
        grid=(n // tm, n_main // tn),
        in_specs=[pl.BlockSpec((tm, d), lambda i, j: (i, 0)),
                  pl.BlockSpec((1, d), lambda i, j: (0, 0)),
                  pl.BlockSpec((d, tn), lambda i, j: (0, j)),
                  pl.BlockSpec((d, LANES), lambda i, j: (0, 0))],
        out_specs=[pl.BlockSpec((tm, tn), lambda i, j: (i, j)),
                   pl.BlockSpec((tm, LANES), lambda i, j: (i, 0))],
        out_shape=[jax.ShapeDtypeStruct((n, n_main), BF16), jax.ShapeDtypeStruct((n, LANES), F32)],
        scratch_shapes=[] if prenormed else [pltpu.VMEM((tm, d), BF16)],
        compiler_params=_params("parallel", "arbitrary"),
        name="norm_proj",
    )(x, g.reshape(1, d), w, w_side)


ROPE_ROW_SPLIT = 4


def _rope_proj_body(xn_ref, w_ref, cos_ref, sin_ref, o_ref, vt_ref):
    j = pl.program_id(1)
    rows = xn_ref.shape[0] // ROPE_ROW_SPLIT

    @pl.when(j < 2)
    def _():
        qscale = jnp.where(j == 0, DIFF_DH ** -0.5 * math.log2(math.e), 1.0).astype(F32)
        for r in range(ROPE_ROW_SPLIT):
            slab = slice(r * rows, (r + 1) * rows)
            acc = jnp.dot(xn_ref[slab, :], w_ref[...], preferred_element_type=F32)
            cos = cos_ref[slab, :] * qscale
            sin = sin_ref[slab, :] * qscale
            for c in range(acc.shape[1] // LANES):
                xc = acc[:, c * LANES:(c + 1) * LANES]
                rot = pltpu.roll(xc, LANES // 2, 1)
                o_ref[slab, c * LANES:(c + 1) * LANES] = (xc * cos + rot * sin).astype(o_ref.dtype)

    @pl.when(j >= 2)
    def _():
        for r in range(ROPE_ROW_SPLIT):
            slab = slice(r * rows, (r + 1) * rows)
            acc = jnp.dot(xn_ref[slab, :], w_ref[...], preferred_element_type=F32)
            vt_ref[:, slab] = acc.T.astype(vt_ref.dtype)


def _rope_proj(xn, w, cos, sin, bsz, seq, tm=1024):
    n, d = xn.shape
    tm = min(tm, seq)
    tn = d
    nseq = seq // tm
    return pl.pallas_call(
        _rope_proj_body,
        grid=(n // tm, 3),
        in_specs=[pl.BlockSpec((tm, d), lambda i, j: (i, 0)),
                  pl.BlockSpec((d, tn), lambda i, j: (0, j)),
                  pl.BlockSpec((tm, LANES), lambda i, j: (i % nseq, 0)),
                  pl.BlockSpec((tm, LANES), lambda i, j: (i % nseq, 0))],
        out_specs=[pl.BlockSpec((tm, tn), lambda i, j: (i, jnp.minimum(j, 1))),
                   pl.BlockSpec((d, tm), lambda i, j: (i // nseq, i % nseq))],
        out_shape=[jax.ShapeDtypeStruct((n, 2 * d), BF16), jax.ShapeDtypeStruct((bsz * d, seq), BF16)],
        compiler_params=_params("parallel", "arbitrary"),
        name="rope_proj",
    )(xn, w, cos, sin)


MLP_ROW_SPLIT = 4


def _mix_mlp_body(a_ref, wo_ref, h_ref, g_ref, wu_ref, wd_ref, gn_ref, o_ref, *rest, final_norm):
    xn_ref = rest[-1]
    j = pl.program_id(1)

    rows = o_ref.shape[0] // MLP_ROW_SPLIT
    slabs = [slice(r * rows, (r + 1) * rows) for r in range(MLP_ROW_SPLIT)]

    @pl.when(j == 0)
    def _():
        for slab in slabs:
            h1 = h_ref[slab, :] + jnp.dot(a_ref[slab, :], wo_ref[...], preferred_element_type=F32)
            o_ref[slab, :] = h1
            xn_ref[slab, :] = _rms(h1, g_ref[...]).astype(BF16)

    def up_proj():
        up = jnp.dot(xn_ref[...], wu_ref[...], preferred_element_type=F32)
        return jnp.square(jnp.maximum(up, 0.0)).astype(BF16)

    last = j == pl.num_programs(1) - 1

    @pl.when(jnp.logical_not(last))
    def _():
        o_ref[...] += jnp.dot(up_proj(), wd_ref[...], preferred_element_type=F32)

    @pl.when(last)
    def _():
        up = up_proj()
        for slab in slabs:
            total = o_ref[slab, :] + jnp.dot(up[slab, :], wd_ref[...], preferred_element_type=F32)
            normed = _rms(total, gn_ref[...])
            if final_norm:
                o_ref[slab, :] = normed
            else:
                o_ref[slab, :] = total
                rest[0][slab, :] = normed.astype(BF16)


def _mix_mlp(a, w_out, h, g, w_up, w_down, layer, g_next, final_norm, tm=1024, tf=1024):
    n, d = h.shape
    k = a.shape[1]
    dff = w_up.shape[2]
    tm = min(tm, n)
    row = lambda i, j: (i, 0)
    const = lambda i, j: (0, 0)
    return pl.pallas_call(
        functools.partial(_mix_mlp_body, final_norm=final_norm),
        grid=(n // tm, dff // tf),
        in_specs=[pl.BlockSpec((tm, k), row),
                  pl.BlockSpec((k, d), const),
                  pl.BlockSpec((tm, d), row),
                  pl.BlockSpec((1, d), const),
                  pl.BlockSpec((None, d, tf), lambda i, j: (layer, 0, j)),
                  pl.BlockSpec((None, tf, d), lambda i, j: (layer, j, 0)),
                  pl.BlockSpec((1, d), const)],
        out_specs=pl.BlockSpec((tm, d), row) if final_norm else [pl.BlockSpec((tm, d), row)] * 2,
        out_shape=(jax.ShapeDtypeStruct((n, d), F32) if final_norm else
                   [jax.ShapeDtypeStruct((n, d), F32), jax.ShapeDtypeStruct((n, d), BF16)]),
        scratch_shapes=[pltpu.VMEM((tm, d), BF16)],
        compiler_params=pltpu.CompilerParams(dimension_semantics=("parallel", "arbitrary"),
                                             vmem_limit_bytes=VMEM_LIMIT_MLP),
        name="mix_mlp",
    )(a, w_out, h, g.reshape(1, d), w_up, w_down, g_next.reshape(1, d))


MLSTM_QK = MLSTM_HEADS * MLSTM_DQK
MLSTM_MAIN = 2 * MLSTM_QK + 2 * MLSTM_HEADS * MLSTM_DV


def _mlstm_body(p_ref, gate_ref, bias_ref, ng_ref, out_ref, *state_refs):
    L = MLSTM_CHUNK
    nh = MLSTM_HEADS
    n_batch = p_ref.shape[0]
    n_chain = n_batch * nh
    c_refs = state_refs[:n_chain]
    n_refs = state_refs[n_chain:2 * n_chain]
    m_refs = state_refs[2 * n_chain:]

    @pl.when(pl.program_id(0) == 0)
    def _():
        for ref in state_refs:
            ref[...] = jnp.zeros_like(ref)

    row = lax.broadcasted_iota(jnp.int32, (L, L), 0)
    col = lax.broadcasted_iota(jnp.int32, (L, L), 1)
    causal = col <= row
    qscale = MLSTM_DQK ** -0.5

    for bi in range(n_batch):
        gates = gate_ref[bi] + bias_ref[...]
        capped = GATE_SOFTCAP * jnp.tanh(gates / GATE_SOFTCAP)
        log_f = -_softplus(-capped)
        log_i = pltpu.roll(capped, nh, 1)
        b = _cumsum_rows(log_f)
        g = log_i - b
        g_max = _cummax_rows(g)
        m_in = m_refs[bi][...]
        m_run = jnp.maximum(m_in, g_max)
        m_t = b + m_run
        scale_all = jnp.exp(m_in - m_run)
        floor_all = jnp.exp(-m_t)
        b_last = b[L - 1:L, :]
        g_max_last = g_max[L - 1:L, :]
        m_run_last = m_run[L - 1:L, :]
        w_all = jnp.exp(g - g_max_last)
        a_old_all = jnp.exp(m_in - m_run_last)
        a_loc_all = jnp.exp(g_max_last - m_run_last)
        m_refs[bi][...] = b_last + m_run_last
        g_t = g.T

        for h in range(nh):
            st = bi * nh + h
            ln = nh + h
            v_off = 2 * MLSTM_QK + h * MLSTM_DV
            o_off = v_off + nh * MLSTM_DV
            q = p_ref[bi, :, h * MLSTM_DQK:(h + 1) * MLSTM_DQK]
            k = p_ref[bi, :, MLSTM_QK + h * MLSTM_DQK:MLSTM_QK + (h + 1) * MLSTM_DQK]
            v = p_ref[bi, :, v_off:v_off + MLSTM_DV]
            c_in = c_refs[st][...]
            n_in = n_refs[st][...]

            decay = jnp.exp(jnp.where(causal, g_t[ln:ln + 1, :] - m_run[:, ln:ln + 1], -jnp.inf))
            qk = lax.dot_general(q, k, _NT, preferred_element_type=F32) * qscale
            p = decay * qk
            scale = scale_all[:, ln:ln + 1]
            q_c = jnp.dot(q, c_in.astype(BF16), preferred_element_type=F32) * qscale
            num = jnp.dot(p.astype(BF16), v, preferred_element_type=F32) + scale * q_c
            q_n = jnp.sum(q.astype(F32) * n_in, axis=1, keepdims=True) * qscale
            den = jnp.sum(p, axis=1, keepdims=True) + scale * q_n
            hs = num / jnp.maximum(jnp.abs(den), floor_all[:, ln:ln + 1])

            kw = k.astype(F32) * w_all[:, ln:ln + 1]
            c_loc = jnp.dot(kw.T.astype(BF16), v, preferred_element_type=F32)
            n_loc = jnp.sum(kw, axis=0, keepdims=True)
            a_old = a_old_all[:, ln:ln + 1]
            a_loc = a_loc_all[:, ln:ln + 1]
            c_refs[st][...] = a_old * c_in + a_loc * c_loc
            n_refs[st][...] = a_old * n_in + a_loc * n_loc

            hn = _rms(hs, ng_ref[:, h * MLSTM_DV:(h + 1) * MLSTM_DV])
            o_gate = _sigmoid(p_ref[bi, :, o_off:o_off + MLSTM_DV].astype(F32))
            out_ref[bi, :, h * MLSTM_DV:(h + 1) * MLSTM_DV] = (o_gate * hn).astype(out_ref.dtype)


def _mlstm_core(proj, gates, bias, norm_g, bsz, seq):
    L = MLSTM_CHUNK
    d = MLSTM_HEADS * MLSTM_DV
    n_chain = bsz * MLSTM_HEADS
    chunk = lambda c: (0, c, 0)
    return pl.pallas_call(
        _mlstm_body,
        grid=(seq // L,),
        in_specs=[pl.BlockSpec((bsz, L, MLSTM_MAIN), chunk),
                  pl.BlockSpec((bsz, L, LANES), chunk),
                  pl.BlockSpec((1, LANES), lambda c: (0, 0)),
                  pl.BlockSpec((1, d), lambda c: (0, 0))],
        out_specs=pl.BlockSpec((bsz, L, d), chunk),
        out_shape=jax.ShapeDtypeStruct((bsz, seq, d), BF16),
        scratch_shapes=([pltpu.VMEM((MLSTM_DQK, MLSTM_DV), F32)] * n_chain
                        + [pltpu.VMEM((1, MLSTM_DQK), F32)] * n_chain
                        + [pltpu.VMEM((1, LANES), F32)] * bsz),
        compiler_params=_params("arbitrary"),
        name="mlstm_core",
    )(proj.reshape(bsz, seq, MLSTM_MAIN), gates.reshape(bsz, seq, LANES), bias, norm_g.reshape(1, d))


def _mlstm_layer(x, prenormed, ln_g, w_in, b_gate, norm_g, bsz, seq):
    w_gate = _pad_cols(w_in[:, MLSTM_MAIN:], LANES).astype(BF16)
    bias = jnp.pad(b_gate.astype(F32), (0, LANES - b_gate.shape[0])).reshape(1, LANES)
    proj, gates = _norm_proj(x, ln_g, w_in.astype(BF16), w_gate, MLSTM_MAIN, prenormed)
    mixed = _mlstm_core(proj, gates, bias, norm_g, bsz, seq)
    return mixed.reshape(bsz * seq, -1)


ATTN_PAIRS_PER_TRIP = 4


def _diff_attn_body(lam_ref, g_ref, q_ref, qn_ref, k_ref, vt_ref, o_ref, sc_a, sc_b, sc_c, cm_a, cm_b, cm_c,
                    m_ref, l_ref, acc_ref, *, tq, lam_init):
    qi = pl.program_id(2)
    q = q_ref[...]
    lane = lax.broadcasted_iota(jnp.int32, q.shape, 1)
    zero = jnp.zeros_like(q)
    is_c1 = (lane & (DIFF_DH // 2)) != 0
    th = tq // 2
    chains = [(c, half) for c in range(2) for half in range(2)]

    def chain_queries(qblk):
        parts = (jnp.where(is_c1, zero, qblk), jnp.where(is_c1, qblk, zero))
        return [parts[c][half * th:(half + 1) * th] for c, half in chains]

    q_ch = chain_queries(q)
    kpos = lax.broadcasted_iota(jnp.int32, (tq, th), 0)
    qpos = lax.broadcasted_iota(jnp.int32, (tq, th), 1)
    causal = [kpos <= qpos + half * th for half in range(2)]
    ones_rows = jnp.ones((16, tq), BF16)

    buf_a, buf_b, buf_c = (sc_a, cm_a), (sc_b, cm_b), (sc_c, cm_c)

    def put_scores(buf, ki, queries=q_ch):
        sc_ref, cm_ref = buf
        k = k_ref[pl.ds(pl.multiple_of(ki * tq, tq), tq), :]
        for x in range(4):
            s = lax.dot_general(k, queries[x], _NT, preferred_element_type=F32)
            sc_ref[x] = s
            cm_ref[x] = _col_reduce(s, jnp.maximum, jnp.max)

    def softmax_pv(ki, buf, masked):
        sc_ref, cm_ref = buf
        vt = vt_ref[:, pl.ds(pl.multiple_of(ki * tq, tq), tq)]
        vt_ones = jnp.concatenate([vt, ones_rows], axis=0)
        for x, (c, half) in enumerate(chains):
            def load_scores():
                s = sc_ref[x]
                return jnp.where(causal[half], s, -jnp.inf) if masked else s

            m = m_ref[x]
            col_max = _col_reduce(load_scores(), jnp.maximum, jnp.max) if masked else cm_ref[x]
            m_new = jnp.maximum(m, col_max)
            alpha = jnp.exp2(m - m_new)
            p = jnp.exp2(load_scores() - m_new)
            pv = jnp.dot(vt_ones, p.astype(BF16), preferred_element_type=F32)
            m_ref[x] = m_new
            l_ref[x] = alpha * l_ref[x] + pv[DIFF_DV:DIFF_DV + 1]
            acc_ref[x] = alpha * acc_ref[x] + pv[:DIFF_DV]

    m_ref[...] = jnp.full(m_ref.shape, -jnp.inf, F32)
    l_ref[...] = jnp.zeros_like(l_ref)
    acc_ref[...] = jnp.zeros_like(acc_ref)

    odd = qi & 1

    @pl.when(qi == 0)
    def _():
        put_scores(buf_a, 0)

    @pl.when(odd == 1)
    def _():
        put_scores(buf_a, 1)
        softmax_pv(0, buf_c, False)

    @pl.when((qi > 0) & (odd == 0))
    def _():
        put_scores(buf_b, 1)
        softmax_pv(0, buf_c, False)
        put_scores(buf_a, 2)
        softmax_pv(1, buf_b, False)

    first = 2 - odd

    def pair(b0):
        put_scores(buf_b, b0 + 1)
        softmax_pv(b0, buf_a, False)
        put_scores(buf_a, b0 + 2)
        softmax_pv(b0 + 1, buf_b, False)

    n_pairs = lax.shift_right_logical(jnp.maximum(qi - first, 0), 1)
    done = 0
    group = 1
    while group < ATTN_PAIRS_PER_TRIP:
        start = first + 2 * done

        @pl.when((n_pairs & group) != 0)
        def _(start=start, group=group):
            for g in range(group):
                pair(start + 2 * g)

        done = done + (n_pairs & group)
        group *= 2

    def trip(j, carry):
        b0 = first + 2 * done + 2 * ATTN_PAIRS_PER_TRIP * j
        for g in range(ATTN_PAIRS_PER_TRIP):
            pair(b0 + 2 * g)
        return carry

    lax.fori_loop(0, lax.shift_right_logical(n_pairs, ATTN_PAIRS_PER_TRIP.bit_length() - 1), trip, 0)

    put_scores(buf_c, 0, chain_queries(qn_ref[...]))
    softmax_pv(qi, buf_a, True)
    a0, a1 = (jnp.concatenate([acc_ref[2 * c] / l_ref[2 * c], acc_ref[2 * c + 1] / l_ref[2 * c + 1]], axis=1)
              for c in range(2))

    lam = lam_ref[...]
    lam_val = (jnp.exp(jnp.sum(lam[0:1] * lam[1:2], axis=1, keepdims=True))
               - jnp.exp(jnp.sum(lam[2:3] * lam[3:4], axis=1, keepdims=True)) + lam_init)
    o = a0 - lam_val * a1
    ms = jnp.mean(o * o, axis=0, keepdims=True)
    y = o * lax.rsqrt(ms + EPS) * g_ref[...] * (1.0 - lam_init)
    o_ref[...] = y.T.astype(o_ref.dtype)


def _diff_attn(qk, v_t, lam, norm_g, bsz, seq, layer_idx, tq=512):
    tq = min(tq, seq)
    nq = seq // tq
    nh = DIFF_HEADS
    lam_init = 0.8 - 0.6 * math.exp(-0.3 * layer_idx)
    return pl.pallas_call(
        functools.partial(_diff_attn_body, tq=tq, lam_init=lam_init),
        grid=(bsz, nh, nq),
        in_specs=[pl.BlockSpec((4, DIFF_DH), lambda b, h, i: (0, 0)),
                  pl.BlockSpec((DIFF_DV, 1), lambda b, h, i: (0, 0)),
                  pl.BlockSpec((tq, DIFF_DV), lambda b, h, i: (b * nq + i, h)),
                  pl.BlockSpec((tq, DIFF_DV), lambda b, h, i: (b * nq + jnp.minimum(i + 1, nq - 1), h)),
                  pl.BlockSpec((seq, DIFF_DV), lambda b, h, i: (b, nh + h)),
                  pl.BlockSpec((DIFF_DV, seq), lambda b, h, i: (b * nh + h, 0))],
        out_specs=pl.BlockSpec((tq, DIFF_DV), lambda b, h, i: (b * nq + i, h)),
        out_shape=jax.ShapeDtypeStruct((bsz * seq, nh * DIFF_DV), BF16),
        scratch_shapes=[pltpu.VMEM((4, tq, tq // 2), F32)] * 3 + [pltpu.VMEM((4, 1, tq // 2), F32)] * 3 + [
                        pltpu.VMEM((4, 1, tq // 2), F32), pltpu.VMEM((4, 1, tq // 2), F32),
                        pltpu.VMEM((4, DIFF_DV, tq // 2), F32)],
        compiler_params=_params("arbitrary", "arbitrary", "arbitrary"),
        name="diff_attn",
    )(lam.astype(F32), norm_g.reshape(DIFF_DV, 1).astype(F32), qk, qk, qk, v_t)


def _rope_tables(seq):
    inv = ROPE_THETA ** (-jnp.arange(0, DIFF_DH, 2, dtype=F32) / DIFF_DH)
    ang = jnp.arange(seq, dtype=F32)[:, None] * inv[None, :]
    cos, sin = jnp.cos(ang), jnp.sin(ang)
    cos_t = jnp.tile(cos, (1, LANES // cos.shape[1]))
    sin_t = jnp.concatenate([-sin, -sin, sin, sin], axis=1)
    return cos_t, sin_t


def _rope_lane_order():
    half = DIFF_DH // 2
    return [c * DIFF_DH + part * half + i for part in range(2) for c in range(2) for i in range(half)]


def _diff_layer(xn, w_in, lam, norm_g, bsz, seq, layer_idx):
    cos, sin = _rope_tables(seq)
    order = jnp.asarray(_rope_lane_order(), jnp.int32)
    nqk = 2 * DIFF_HEADS * DIFF_DV
    qk_cols = (jnp.arange(nqk // LANES, dtype=jnp.int32) * LANES)[:, None] + order[None, :]
    cols = jnp.concatenate([qk_cols.reshape(-1), jnp.arange(nqk, w_in.shape[1], dtype=jnp.int32)])
    qk, v_t = _rope_proj(xn, w_in.astype(BF16)[:, cols], cos, sin, bsz, seq)
    return _diff_attn(qk, v_t, lam, norm_g, bsz, seq, layer_idx)


SSD_BC = SSD_GROUPS * SSD_STATE
SSD_CONV_DIM = SSD_INNER + 2 * SSD_BC
SSD_MAIN = SSD_INNER + SSD_CONV_DIM


def _expand_heads(a, g, width):
    nr = SSD_HEADS_PER_GROUP
    lane = lax.broadcasted_iota(jnp.int32, (a.shape[0], width), 1)
    out = jnp.broadcast_to(a[:, nr * g + nr - 1:nr * g + nr], (a.shape[0], width))
    for r in range(nr - 2, -1, -1):
        out = jnp.where(lane < SSD_HEADDIM * (r + 1), a[:, nr * g + r:nr * g + r + 1], out)
    return out


def _ssd_body(z_ref, x_ref, bc_ref, dt_ref, cw_ref, cb_ref, dtb_ref, alog_ref, dsk_ref, ng_ref, e3_ref,
              out_ref, ext_ref, s_ref):
    L = SSD_CHUNK
    gw = SSD_GROUP_WIDTH
    ns = SSD_STATE

    @pl.when(pl.program_id(1) == 0)
    def _():
        ext_ref[0:CONV_HALO, :] = jnp.zeros((CONV_HALO, SSD_CONV_DIM), F32)
        s_ref[...] = jnp.zeros_like(s_ref)

    ext_ref[CONV_HALO:CONV_HALO + L, 0:SSD_INNER] = x_ref[...].astype(F32)
    ext_ref[CONV_HALO:CONV_HALO + L, SSD_INNER:SSD_CONV_DIM] = bc_ref[...].astype(F32)

    def conv_silu(lo, width):
        ext = ext_ref[:, lo:lo + width]
        acc = cb_ref[:, lo:lo + width] + cw_ref[SSD_CONV - 1:SSD_CONV, lo:lo + width] * ext[CONV_HALO:]
        for j in range(SSD_CONV - 1):
            shifted = pltpu.roll(ext, SSD_CONV - 1 - j, 0)[CONV_HALO:]
            acc = acc + cw_ref[j:j + 1, lo:lo + width] * shifted
        return _silu(acc)

    dt = _softplus(dt_ref[...] + dtb_ref[...])
    cum = _cumsum_rows(dt * (-jnp.exp(alog_ref[...])))
    cum_t = cum.T
    cum_last = cum[L - 1:L, :]
    exp_cum = jnp.exp(cum)
    decay_to_end = jnp.exp(cum_last - cum)
    chunk_decay = jnp.exp(cum_last)
    row = lax.broadcasted_iota(jnp.int32, (L, L), 0)
    col = lax.broadcasted_iota(jnp.int32, (L, L), 1)
    causal = col <= row
    lane_gw = lax.broadcasted_iota(jnp.int32, (1, gw), 1)
    head_rows = [jnp.where(lane_gw // SSD_HEADDIM == r, 1.0, 0.0).astype(BF16)
                 for r in range(SSD_HEADS_PER_GROUP)]

    def expand(a):
        hi = a.astype(BF16)
        rest = a - hi.astype(F32)
        mid = rest.astype(BF16)
        lo = (rest - mid.astype(F32)).astype(BF16)
        return jnp.dot(jnp.concatenate([hi, mid, lo], axis=1), e3_ref[...], preferred_element_type=F32)

    dt_x = expand(dt)
    exp_cum_x = expand(exp_cum)
    decay_to_end_x = expand(decay_to_end)

    for g in range(SSD_GROUPS):
        cols = slice(g * gw, (g + 1) * gw)
        xs = conv_silu(g * gw, gw)
        bm = conv_silu(SSD_INNER + g * ns, ns).astype(BF16)
        cm = conv_silu(SSD_INNER + SSD_BC + g * ns, ns).astype(BF16)
        xdt = xs * dt_x[:, cols]
        s_in = s_ref[g]
        cb = lax.dot_general(cm, bm, _NT, preferred_element_type=F32)
        y = exp_cum_x[:, cols] * jnp.dot(cm, s_in.astype(BF16), preferred_element_type=F32)
        weights, inputs = [], []
        xdt_lo = xdt.astype(BF16)
        for r in range(SSD_HEADS_PER_GROUP):
            hh = SSD_HEADS_PER_GROUP * g + r
            decay = jnp.exp(jnp.where(causal, cum[:, hh:hh + 1] - cum_t[hh:hh + 1, :], -jnp.inf))
            weights.append((cb * decay).astype(BF16))
            inputs.append(xdt_lo * head_rows[r])
        y = y + jnp.dot(jnp.concatenate(weights, axis=1), jnp.concatenate(inputs, axis=0),
                        preferred_element_type=F32)
        xdte = (xdt * decay_to_end_x[:, cols]).astype(BF16)
        states = jnp.dot(bm.astype(F32).T.astype(BF16), xdte, preferred_element_type=F32)
        s_ref[g] = _expand_heads(chunk_decay, g, gw) * s_in + states
        y = y + dsk_ref[:, g * gw:(g + 1) * gw] * xs
        y = y * _silu(z_ref[:, g * gw:(g + 1) * gw].astype(F32))
        out_ref[:, g * gw:(g + 1) * gw] = _rms(y, ng_ref[:, g * gw:(g + 1) * gw]).astype(out_ref.dtype)

    ext_ref[0:CONV_HALO, :] = ext_ref[L:L + CONV_HALO, :]


def _ssd_core(proj, dt, conv_w, conv_b, dt_bias, a_log, d_skip, norm_g, bsz, seq):
    L = SSD_CHUNK
    nc = seq // L
    rowmap = lambda b, c: (b * nc + c, 0)
    const = lambda b, c: (0, 0)
    head_of_channel = jnp.arange(SSD_INNER, dtype=jnp.int32) // SSD_HEADDIM
    expand1 = (jnp.arange(LANES, dtype=jnp.int32)[:, None] == head_of_channel[None, :]).astype(BF16)
    expand3 = jnp.concatenate([expand1] * 3, axis=0)
    return pl.pallas_call(
        _ssd_body,
        grid=(bsz, nc),
        in_specs=[pl.BlockSpec((L, SSD_INNER), rowmap),
                  pl.BlockSpec((L, SSD_INNER), lambda b, c: (b * nc + c, 1)),
                  pl.BlockSpec((L, 2 * SSD_BC), lambda b, c: (b * nc + c, 2)),
                  pl.BlockSpec((L, LANES), rowmap),
                  pl.BlockSpec((SSD_CONV, SSD_CONV_DIM), const),
                  pl.BlockSpec((1, SSD_CONV_DIM), const),
                  pl.BlockSpec((1, LANES), const),
                  pl.BlockSpec((1, LANES), const),
                  pl.BlockSpec((1, SSD_INNER), const),
                  pl.BlockSpec((1, SSD_INNER), const),
                  pl.BlockSpec((3 * LANES, SSD_INNER), const)],
        out_specs=pl.BlockSpec((L, SSD_INNER), rowmap),
        out_shape=jax.ShapeDtypeStruct((bsz * seq, SSD_INNER), BF16),
        scratch_shapes=[pltpu.VMEM((L + CONV_HALO, SSD_CONV_DIM), F32),
                        pltpu.VMEM((SSD_GROUPS, SSD_STATE, SSD_GROUP_WIDTH), F32)],
        compiler_params=_params("parallel", "arbitrary"),
        name="ssd_core",
    )(proj, proj, proj, dt, conv_w, conv_b, dt_bias, a_log, d_skip, norm_g, expand3)


def _ssd_layer(x, prenormed, ln_g, w_in, conv_w, conv_b, dt_bias, a_log, d_skip, norm_g, bsz, seq):
    w_dt = _pad_cols(w_in[:, SSD_MAIN:], LANES).astype(BF16)
    proj, dt = _norm_proj(x, ln_g, w_in.astype(BF16), w_dt, SSD_MAIN, prenormed)
    pad_heads = lambda a: jnp.pad(a.astype(F32), (0, LANES - SSD_HEADS)).reshape(1, LANES)
    mixed = _ssd_core(proj, dt, conv_w.astype(F32), conv_b.astype(F32).reshape(1, SSD_CONV_DIM),
                      pad_heads(dt_bias), pad_heads(a_log),
                      jnp.repeat(d_skip.astype(F32), SSD_HEADDIM).reshape(1, SSD_INNER),
                      norm_g.astype(F32).reshape(1, SSD_INNER), bsz, seq)
    return mixed


def kernel(x, ln_mix, ln_mlp, w_up, w_down, ln_f, mlstm_w_in, mlstm_b_gate, mlstm_norm, mlstm_w_out, diff_w_in, diff_lam, diff_norm, diff_w_out, ssd_w_in, ssd_conv_w, ssd_conv_b, ssd_dt_bias, ssd_A_log, ssd_D, ssd_norm, ssd_w_out):
    bsz, seq, d = x.shape
    h = x.reshape(bsz * seq, d)
    w_up_lo, w_down_lo = w_up.astype(BF16), w_down.astype(BF16)
    xn = None
    for i in range(DEPTH):
        kind, j = i % N_MIXERS, i // N_MIXERS
        prenormed = xn is not None
        mixer_in = xn if prenormed else h
        if kind == 0:
            mixed = _mlstm_layer(mixer_in, prenormed, ln_mix[i], mlstm_w_in[j], mlstm_b_gate[j],
                                 mlstm_norm[j], bsz, seq)
            w_out = mlstm_w_out[j]
        elif kind == 1:
            assert prenormed, "the attention projection expects a pre-normalised input"
            mixed = _diff_layer(mixer_in, diff_w_in[j], diff_lam[j], diff_norm[j], bsz, seq, i)
            w_out = diff_w_out[j]
        else:
            mixed = _ssd_layer(mixer_in, prenormed, ln_mix[i], ssd_w_in[j], ssd_conv_w[j], ssd_conv_b[j],
                               ssd_dt_bias[j], ssd_A_log[j], ssd_D[j], ssd_norm[j], bsz, seq)
            w_out = ssd_w_out[j]
        last = i == DEPTH - 1
        out = _mix_mlp(mixed, w_out.astype(BF16), h, ln_mlp[i], w_up_lo, w_down_lo, i,
                       ln_f if last else ln_mix[i + 1], final_norm=last)
        h, xn = (out, None) if last else out
    return h.reshape(bsz, seq, d)
```

```python
import functools
import math

import jax
import jax.numpy as jnp
from jax import lax
from jax.experimental import pallas as pl
from jax.experimental.pallas import tpu as pltpu

F32 = jnp.float32
BF16 = jnp.bfloat16

DEPTH = 4
N_MIXERS = 3
EPS = 1e-6

MLSTM_HEADS = 4
MLSTM_DV = 256
MLSTM_DQK = 128
MLSTM_CHUNK = 128
GATE_SOFTCAP = 15.0

DIFF_HEADS = 8
DIFF_DH = 64
DIFF_DV = 128
ROPE_THETA = 10000.0

SSD_INNER = 2048
SSD_HEADDIM = 64
SSD_HEADS = 32
SSD_GROUPS = 8
SSD_STATE = 128
SSD_CONV = 4
SSD_CHUNK = 128
SSD_GROUP_WIDTH = SSD_INNER // SSD_GROUPS
SSD_HEADS_PER_GROUP = SSD_HEADS // SSD_GROUPS

LANES = 128
CONV_HALO = 8
VMEM_LIMIT = 48 * 1024 * 1024
VMEM_LIMIT_MLP = 56 * 1024 * 1024

_NT = (((1,), (1,)), ((), ()))


def _params(*sem):
    return pltpu.CompilerParams(dimension_semantics=sem, vmem_limit_bytes=VMEM_LIMIT)


def _rms(x, g):
    ms = jnp.mean(x * x, axis=-1, keepdims=True)
    return x * lax.rsqrt(ms + EPS) * g


def _cumsum_rows(x):
    n = x.shape[0]
    row = lax.broadcasted_iota(jnp.int32, x.shape, 0)
    s = 1
    while s < n:
        x = x + jnp.where(row >= s, pltpu.roll(x, s, 0), 0.0)
        s *= 2
    return x


def _cummax_rows(x):
    n = x.shape[0]
    row = lax.broadcasted_iota(jnp.int32, x.shape, 0)
    s = 1
    while s < n:
        x = jnp.maximum(x, jnp.where(row >= s, pltpu.roll(x, s, 0), -jnp.inf))
        s *= 2
    return x


COL_REDUCE_SLAB = 64


def _col_reduce(x, pair_op, reduce_fn):
    slab = min(COL_REDUCE_SLAB, x.shape[0])
    acc = x[:slab]
    for r0 in range(slab, x.shape[0], slab):
        acc = pair_op(acc, x[r0:r0 + slab])
    while acc.shape[0] > 8:
        half = acc.shape[0] // 2
        acc = pair_op(acc[:half], acc[half:])
    return reduce_fn(acc, axis=0, keepdims=True)


def _sigmoid(x):
    return 0.5 + 0.5 * jnp.tanh(0.5 * x)


def _silu(x):
    half = 0.5 * x
    return half + half * jnp.tanh(half)


def _softplus(x):
    return jnp.maximum(x, 0.0) + jnp.log1p(jnp.exp(-jnp.abs(x)))


def _pad_cols(w, n):
    return jnp.pad(w, ((0, 0), (0, n - w.shape[1])))


def _norm_proj_body(x_ref, g_ref, w_ref, ws_ref, o_ref, os_ref, *scratch):
    first = pl.program_id(1) == 0
    if scratch:
        xn_ref, = scratch

        @pl.when(first)
        def _():
            xn_ref[...] = _rms(x_ref[...], g_ref[...]).astype(BF16)
    else:
        xn_ref = x_ref

    @pl.when(first)
    def _():
        os_ref[...] = jnp.dot(xn_ref[...], ws_ref[...], preferred_element_type=F32)

    o_ref[...] = jnp.dot(xn_ref[...], w_ref[...], preferred_element_type=F32).astype(o_ref.dtype)


def _norm_proj(x, g, w, w_side, n_main, prenormed, tn=1536, tm=1024):
    n, d = x.shape
    tm = min(tm, n)
    return pl.pallas_call(
        _norm_proj_body,
        grid=(n // tm, n_main // tn),
        in_specs=[pl.BlockSpec((tm, d), lambda i, j: (i, 0)),
                  pl.BlockSpec((1, d), lambda i, j: (0, 0)),
                  pl.BlockSpec((d, tn), lambda i, j: (0, j)),
                  pl.BlockSpec((d, LANES), lambda i, j: (0, 0))],
        out_specs=[pl.BlockSpec((tm, tn), lambda i, j: (i, j)),
                   pl.BlockSpec((tm, LANES), lambda i, j: (i, 0))],
        out_shape=[jax.ShapeDtypeStruct((n, n_main), BF16), jax.ShapeDtypeStruct((n, LANES), F32)],
        scratch_shapes=[] if prenormed else [pltpu.VMEM((tm, d), BF16)],
        compiler_params=_params("parallel", "arbitrary"),
        name="norm_proj",
    )(x, g.reshape(1, d), w, w_side)


ROPE_ROW_SPLIT = 4


def _rope_proj_body(xn_ref, w_ref, cos_ref, sin_ref, o_ref, vt_ref):
    j = pl.program_id(1)
    rows = xn_ref.shape[0] // ROPE_ROW_SPLIT

    @pl.when(j < 2)
    def _():
        qscale = jnp.where(j == 0, DIFF_DH ** -0.5 * math.log2(math.e), 1.0).astype(F32)
        for r in range(ROPE_ROW_SPLIT):
            slab = slice(r * rows, (r + 1) * rows)
            acc = jnp.dot(xn_ref[slab, :], w_ref[...], preferred_element_type=F32)
            cos = cos_ref[slab, :] * qscale
            sin = sin_ref[slab, :] * qscale
            for c in range(acc.shape[1] // LANES):
                xc = acc[:, c * LANES:(c + 1) * LANES]
                rot = pltpu.roll(xc, LANES // 2, 1)
                o_ref[slab, c * LANES:(c + 1) * LANES] = (xc * cos + rot * sin).astype(o_ref.dtype)

    @pl.when(j >= 2)
    def _():
        for r in range(ROPE_ROW_SPLIT):
            slab = slice(r * rows, (r + 1) * rows)
            acc = jnp.dot(xn_ref[slab, :], w_ref[...], preferred_element_type=F32)
            vt_ref[:, slab] = acc.T.astype(vt_ref.dtype)


def _rope_proj(xn, w, cos, sin, bsz, seq, tm=1024):
    n, d = xn.shape
    tm = min(tm, seq)
    tn = d
    nseq = seq // tm
    return pl.pallas_call(
        _rope_proj_body,
        grid=(n // tm, 3),
        in_specs=[pl.BlockSpec((tm, d), lambda i, j: (i, 0)),
                  pl.BlockSpec((d, tn), lambda i, j: (0, j)),
                  pl.BlockSpec((tm, LANES), lambda i, j: (i % nseq, 0)),
                  pl.BlockSpec((tm, LANES), lambda i, j: (i % nseq, 0))],
        out_specs=[pl.BlockSpec((tm, tn), lambda i, j: (i, jnp.minimum(j, 1))),
                   pl.BlockSpec((d, tm), lambda i, j: (i // nseq, i % nseq))],
        out_shape=[jax.ShapeDtypeStruct((n, 2 * d), BF16), jax.ShapeDtypeStruct((bsz * d, seq), BF16)],
        compiler_params=_params("parallel", "arbitrary"),
        name="rope_proj",
    )(xn, w, cos, sin)


MLP_ROW_SPLIT = 4


def _mix_mlp_body(a_ref, wo_ref, h_ref, g_ref, wu_ref, wd_ref, gn_ref, o_ref, *rest, final_norm):
    xn_ref = rest[-1]
    j = pl.program_id(1)

    rows = o_ref.shape[0] // MLP_ROW_SPLIT
    slabs = [slice(r * rows, (r + 1) * rows) for r in range(MLP_ROW_SPLIT)]

    @pl.when(j == 0)
    def _():
        for slab in slabs:
            h1 = h_ref[slab, :] + jnp.dot(a_ref[slab, :], wo_ref[...], preferred_element_type=F32)
            o_ref[slab, :] = h1
            xn_ref[slab, :] = _rms(h1, g_ref[...]).astype(BF16)

    def up_proj():
        up = jnp.dot(xn_ref[...], wu_ref[...], preferred_element_type=F32)
        return jnp.square(jnp.maximum(up, 0.0)).astype(BF16)

    last = j == pl.num_programs(1) - 1

    @pl.when(jnp.logical_not(last))
    def _():
        o_ref[...] += jnp.dot(up_proj(), wd_ref[...], preferred_element_type=F32)

    @pl.when(last)
    def _():
        up = up_proj()
        for slab in slabs:
            total = o_ref[slab, :] + jnp.dot(up[slab, :], wd_ref[...], preferred_element_type=F32)
            normed = _rms(total, gn_ref[...])
            if final_norm:
                o_ref[slab, :] = normed
            else:
                o_ref[slab, :] = total
                rest[0][slab, :] = normed.astype(BF16)


def _mix_mlp(a, w_out, h, g, w_up, w_down, layer, g_next, final_norm, tm=1024, tf=1024):
    n, d = h.shape
    k = a.shape[1]
    dff = w_up.shape[2]
    tm = min(tm, n)
    row = lambda i, j: (i, 0)
    const = lambda i, j: (0, 0)
    return pl.pallas_call(
        functools.partial(_mix_mlp_body, final_norm=final_norm),
        grid=(n // tm, dff // tf),
        in_specs=[pl.BlockSpec((tm, k), row),
                  pl.BlockSpec((k, d), const),
                  pl.BlockSpec((tm, d), row),
                  pl.BlockSpec((1, d), const),
                  pl.BlockSpec((None, d, tf), lambda i, j: (layer, 0, j)),
                  pl.BlockSpec((None, tf, d), lambda i, j: (layer, j, 0)),
                  pl.BlockSpec((1, d), const)],
        out_specs=pl.BlockSpec((tm, d), row) if final_norm else [pl.BlockSpec((tm, d), row)] * 2,
        out_shape=(jax.ShapeDtypeStruct((n, d), F32) if final_norm else
                   [jax.ShapeDtypeStruct((n, d), F32), jax.ShapeDtypeStruct((n, d), BF16)]),
        scratch_shapes=[pltpu.VMEM((tm, d), BF16)],
        compiler_params=pltpu.CompilerParams(dimension_semantics=("parallel", "arbitrary"),
                                             vmem_limit_bytes=VMEM_LIMIT_MLP),
        name="mix_mlp",
    )(a, w_out, h, g.reshape(1, d), w_up, w_down, g_next.reshape(1, d))


MLSTM_QK = MLSTM_HEADS * MLSTM_DQK
MLSTM_MAIN = 2 * MLSTM_QK + 2 * MLSTM_HEADS * MLSTM_DV


def _mlstm_body(p_ref, gate_ref, bias_ref, ng_ref, out_ref, *state_refs):
    L = MLSTM_CHUNK
    nh = MLSTM_HEADS
    n_batch = p_ref.shape[0]
    n_chain = n_batch * nh
    cn_refs = state_refs[:n_chain]
    m_refs = state_refs[n_chain:]
    ones_cols = jnp.ones((L, LANES), BF16)

    @pl.when(pl.program_id(0) == 0)
    def _():
        for ref in state_refs:
            ref[...] = jnp.zeros_like(ref)

    row = lax.broadcasted_iota(jnp.int32, (L, L), 0)
    col = lax.broadcasted_iota(jnp.int32, (L, L), 1)
    causal = col <= row
    qscale = MLSTM_DQK ** -0.5

    for bi in range(n_batch):
        gates = gate_ref[bi] + bias_ref[...]
        capped = GATE_SOFTCAP * jnp.tanh(gates / GATE_SOFTCAP)
        log_f = -_softplus(-capped)
        log_i = pltpu.roll(capped, nh, 1)
        b = _cumsum_rows(log_f)
        g = log_i - b
        g_max = _cummax_rows(g)
        m_in = m_refs[bi][...]
        m_run = jnp.maximum(m_in, g_max)
        m_t = b + m_run
        scale_all = jnp.exp(m_in - m_run)
        floor_all = jnp.exp(-m_t)
        b_last = b[L - 1:L, :]
        g_max_last = g_max[L - 1:L, :]
        m_run_last = m_run[L - 1:L, :]
        w_all = jnp.exp(g - g_max_last)
        a_old_all = jnp.exp(m_in - m_run_last)
        a_loc_all = jnp.exp(g_max_last - m_run_last)
        m_refs[bi][...] = b_last + m_run_last
        g_t = g.T

        for h in range(nh):
            st = bi * nh + h
            ln = nh + h
            v_off = 2 * MLSTM_QK + h * MLSTM_DV
            o_off = v_off + nh * MLSTM_DV
            q = p_ref[bi, :, h * MLSTM_DQK:(h + 1) * MLSTM_DQK]
            k = p_ref[bi, :, MLSTM_QK + h * MLSTM_DQK:MLSTM_QK + (h + 1) * MLSTM_DQK]
            v_ones = jnp.concatenate([p_ref[bi, :, v_off:v_off + MLSTM_DV], ones_cols], axis=1)
            cn_in = cn_refs[st][...]

            decay = jnp.exp(jnp.where(causal, g_t[ln:ln + 1, :] - m_run[:, ln:ln + 1], -jnp.inf))
            qk = lax.dot_general(q, k, _NT, preferred_element_type=F32) * qscale
            p = decay * qk
            scale = scale_all[:, ln:ln + 1]
            inter = jnp.dot(q, cn_in.astype(BF16), preferred_element_type=F32) * qscale
            both = jnp.dot(p.astype(BF16), v_ones, preferred_element_type=F32) + scale * inter
            den = both[:, MLSTM_DV:MLSTM_DV + 1]
            hs = both[:, :MLSTM_DV] / jnp.maximum(jnp.abs(den), floor_all[:, ln:ln + 1])

            kw = k.astype(F32) * w_all[:, ln:ln + 1]
            cn_loc = jnp.dot(kw.T.astype(BF16), v_ones, preferred_element_type=F32)
            cn_refs[st][...] = a_old_all[:, ln:ln + 1] * cn_in + a_loc_all[:, ln:ln + 1] * cn_loc

            hn = _rms(hs, ng_ref[:, h * MLSTM_DV:(h + 1) * MLSTM_DV])
            o_gate = _sigmoid(p_ref[bi, :, o_off:o_off + MLSTM_DV].astype(F32))
            out_ref[bi, :, h * MLSTM_DV:(h + 1) * MLSTM_DV] = (o_gate * hn).astype(out_ref.dtype)


def _mlstm_core(proj, gates, bias, norm_g, bsz, seq):
    L = MLSTM_CHUNK
    d = MLSTM_HEADS * MLSTM_DV
    n_chain = bsz * MLSTM_HEADS
    chunk = lambda c: (0, c, 0)
    return pl.pallas_call(
        _mlstm_body,
        grid=(seq // L,),
        in_specs=[pl.BlockSpec((bsz, L, MLSTM_MAIN), chunk),
                  pl.BlockSpec((bsz, L, LANES), chunk),
                  pl.BlockSpec((1, LANES), lambda c: (0, 0)),
                  pl.BlockSpec((1, d), lambda c: (0, 0))],
        out_specs=pl.BlockSpec((bsz, L, d), chunk),
        out_shape=jax.ShapeDtypeStruct((bsz, seq, d), BF16),
        scratch_shapes=([pltpu.VMEM((MLSTM_DQK, MLSTM_DV + LANES), F32)] * n_chain
                        + [pltpu.VMEM((1, LANES), F32)] * bsz),
        compiler_params=_params("arbitrary"),
        name="mlstm_core",
    )(proj.reshape(bsz, seq, MLSTM_MAIN), gates.reshape(bsz, seq, LANES), bias, norm_g.reshape(1, d))


def _mlstm_layer(x, prenormed, ln_g, w_in, b_gate, norm_g, bsz, seq):
    w_gate = _pad_cols(w_in[:, MLSTM_MAIN:], LANES).astype(BF16)
    bias = jnp.pad(b_gate.astype(F32), (0, LANES - b_gate.shape[0])).reshape(1, LANES)
    proj, gates = _norm_proj(x, ln_g, w_in.astype(BF16), w_gate, MLSTM_MAIN, prenormed)
    mixed = _mlstm_core(proj, gates, bias, norm_g, bsz, seq)
    return mixed.reshape(bsz * seq, -1)


ATTN_PAIRS_PER_TRIP = 4


def _diff_attn_body(lam_ref, g_ref, q_ref, qn_ref, k_ref, vt_ref, o_ref, sc_a, sc_b, sc_c, cm_a, cm_b, cm_c,
                    m_ref, l_ref, acc_ref, *, tq, lam_init):
    qi = pl.program_id(2)
    q = q_ref[...]
    lane = lax.broadcasted_iota(jnp.int32, q.shape, 1)
    zero = jnp.zeros_like(q)
    is_c1 = (lane & (DIFF_DH // 2)) != 0
    th = tq // 2
    chains = [(c, half) for c in range(2) for half in range(2)]

    def chain_queries(qblk):
        parts = (jnp.where(is_c1, zero, qblk), jnp.where(is_c1, qblk, zero))
        return [parts[c][half * th:(half + 1) * th] for c, half in chains]

    q_ch = chain_queries(q)
    kpos = lax.broadcasted_iota(jnp.int32, (tq, th), 0)
    qpos = lax.broadcasted_iota(jnp.int32, (tq, th), 1)
    causal = [kpos <= qpos + half * th for half in range(2)]
    ones_rows = jnp.ones((16, tq), BF16)

    buf_a, buf_b, buf_c = (sc_a, cm_a), (sc_b, cm_b), (sc_c, cm_c)

    def put_scores(buf, ki, queries=q_ch):
        sc_ref, cm_ref = buf
        k = k_ref[pl.ds(pl.multiple_of(ki * tq, tq), tq), :]
        for x in range(4):
            s = lax.dot_general(k, queries[x], _NT, preferred_element_type=F32)
            sc_ref[x] = s
            cm_ref[x] = _col_reduce(s, jnp.maximum, jnp.max)

    def softmax_pv(ki, buf, masked):
        sc_ref, cm_ref = buf
        vt = vt_ref[:, pl.ds(pl.multiple_of(ki * tq, tq), tq)]
        vt_ones = jnp.concatenate([vt, ones_rows], axis=0)
        for x, (c, half) in enumerate(chains):
            nkeys = th if masked and half == 0 else tq

            def load_scores():
                s = sc_ref[x, :nkeys, :]
                return jnp.where(causal[half][:nkeys], s, -jnp.inf) if masked else s

            m = m_ref[x]
            col_max = _col_reduce(load_scores(), jnp.maximum, jnp.max) if masked else cm_ref[x]
            m_new = jnp.maximum(m, col_max)
            alpha = jnp.exp2(m - m_new)
            p = jnp.exp2(load_scores() - m_new)
            pv = jnp.dot(vt_ones[:, :nkeys], p.astype(BF16), preferred_element_type=F32)
            m_ref[x] = m_new
            l_ref[x] = alpha * l_ref[x] + pv[DIFF_DV:DIFF_DV + 1]
            acc_ref[x] = alpha * acc_ref[x] + pv[:DIFF_DV]

    m_ref[...] = jnp.full(m_ref.shape, -jnp.inf, F32)
    l_ref[...] = jnp.zeros_like(l_ref)
    acc_ref[...] = jnp.zeros_like(acc_ref)

    odd = qi & 1

    @pl.when(qi == 0)
    def _():
        put_scores(buf_a, 0)

    @pl.when(odd == 1)
    def _():
        put_scores(buf_a, 1)
        softmax_pv(0, buf_c, False)

    @pl.when((qi > 0) & (odd == 0))
    def _():
        put_scores(buf_b, 1)
        softmax_pv(0, buf_c, False)
        put_scores(buf_a, 2)
        softmax_pv(1, buf_b, False)

    first = 2 - odd

    def pair(b0):
        put_scores(buf_b, b0 + 1)
        softmax_pv(b0, buf_a, False)
        put_scores(buf_a, b0 + 2)
        softmax_pv(b0 + 1, buf_b, False)

    n_pairs = lax.shift_right_logical(jnp.maximum(qi - first, 0), 1)
    done = 0
    group = 1
    while group < ATTN_PAIRS_PER_TRIP:
        start = first + 2 * done

        @pl.when((n_pairs & group) != 0)
        def _(start=start, group=group):
            for g in range(group):
                pair(start + 2 * g)

        done = done + (n_pairs & group)
        group *= 2

    def trip(j, carry):
        b0 = first + 2 * done + 2 * ATTN_PAIRS_PER_TRIP * j
        for g in range(ATTN_PAIRS_PER_TRIP):
            pair(b0 + 2 * g)
        return carry

    lax.fori_loop(0, lax.shift_right_logical(n_pairs, ATTN_PAIRS_PER_TRIP.bit_length() - 1), trip, 0)

    put_scores(buf_c, 0, chain_queries(qn_ref[...]))
    softmax_pv(qi, buf_a, True)
    a0, a1 = (jnp.concatenate([acc_ref[2 * c] / l_ref[2 * c], acc_ref[2 * c + 1] / l_ref[2 * c + 1]], axis=1)
              for c in range(2))

    lam = lam_ref[...]
    lam_val = (jnp.exp(jnp.sum(lam[0:1] * lam[1:2], axis=1, keepdims=True))
               - jnp.exp(jnp.sum(lam[2:3] * lam[3:4], axis=1, keepdims=True)) + lam_init)
    o = a0 - lam_val * a1
    ms = jnp.mean(o * o, axis=0, keepdims=True)
    y = o * lax.rsqrt(ms + EPS) * g_ref[...] * (1.0 - lam_init)
    o_ref[...] = y.T.astype(o_ref.dtype)


def _diff_attn(qk, v_t, lam, norm_g, bsz, seq, layer_idx, tq=512):
    tq = min(tq, seq)
    nq = seq // tq
    nh = DIFF_HEADS
    lam_init = 0.8 - 0.6 * math.exp(-0.3 * layer_idx)
    return pl.pallas_call(
        functools.partial(_diff_attn_body, tq=tq, lam_init=lam_init),
        grid=(bsz, nh, nq),
        in_specs=[pl.BlockSpec((4, DIFF_DH), lambda b, h, i: (0, 0)),
                  pl.BlockSpec((DIFF_DV, 1), lambda b, h, i: (0, 0)),
                  pl.BlockSpec((tq, DIFF_DV), lambda b, h, i: (b * nq + i, h)),
                  pl.BlockSpec((tq, DIFF_DV), lambda b, h, i: (b * nq + jnp.minimum(i + 1, nq - 1), h)),
                  pl.BlockSpec((seq, DIFF_DV), lambda b, h, i: (b, nh + h)),
                  pl.BlockSpec((DIFF_DV, seq), lambda b, h, i: (b * nh + h, 0))],
        out_specs=pl.BlockSpec((tq, DIFF_DV), lambda b, h, i: (b * nq + i, h)),
        out_shape=jax.ShapeDtypeStruct((bsz * seq, nh * DIFF_DV), BF16),
        scratch_shapes=[pltpu.VMEM((4, tq, tq // 2), F32)] * 3 + [pltpu.VMEM((4, 1, tq // 2), F32)] * 3 + [
                        pltpu.VMEM((4, 1, tq // 2), F32), pltpu.VMEM((4, 1, tq // 2), F32),
                        pltpu.VMEM((4, DIFF_DV, tq // 2), F32)],
        compiler_params=_params("arbitrary", "arbitrary", "arbitrary"),
        name="diff_attn",
    )(lam.astype(F32), norm_g.reshape(DIFF_DV, 1).astype(F32), qk, qk, qk, v_t)


def _rope_tables(seq):
    inv = ROPE_THETA ** (-jnp.arange(0, DIFF_DH, 2, dtype=F32) / DIFF_DH)
    ang = jnp.arange(seq, dtype=F32)[:, None] * inv[None, :]
    cos, sin = jnp.cos(ang), jnp.sin(ang)
    cos_t = jnp.tile(cos, (1, LANES // cos.shape[1]))
    sin_t = jnp.concatenate([-sin, -sin, sin, sin], axis=1)
    return cos_t, sin_t


def _rope_lane_order():
    half = DIFF_DH // 2
    return [c * DIFF_DH + part * half + i for part in range(2) for c in range(2) for i in range(half)]


def _diff_layer(xn, w_in, lam, norm_g, bsz, seq, layer_idx):
    cos, sin = _rope_tables(seq)
    order = jnp.asarray(_rope_lane_order(), jnp.int32)
    nqk = 2 * DIFF_HEADS * DIFF_DV
    qk_cols = (jnp.arange(nqk // LANES, dtype=jnp.int32) * LANES)[:, None] + order[None, :]
    cols = jnp.concatenate([qk_cols.reshape(-1), jnp.arange(nqk, w_in.shape[1], dtype=jnp.int32)])
    qk, v_t = _rope_proj(xn, w_in.astype(BF16)[:, cols], cos, sin, bsz, seq)
    return _diff_attn(qk, v_t, lam, norm_g, bsz, seq, layer_idx)


SSD_BC = SSD_GROUPS * SSD_STATE
SSD_CONV_DIM = SSD_INNER + 2 * SSD_BC
SSD_MAIN = SSD_INNER + SSD_CONV_DIM


def _expand_heads(a, g, width):
    nr = SSD_HEADS_PER_GROUP
    lane = lax.broadcasted_iota(jnp.int32, (a.shape[0], width), 1)
    out = jnp.broadcast_to(a[:, nr * g + nr - 1:nr * g + nr], (a.shape[0], width))
    for r in range(nr - 2, -1, -1):
        out = jnp.where(lane < SSD_HEADDIM * (r + 1), a[:, nr * g + r:nr * g + r + 1], out)
    return out


def _ssd_body(z_ref, x_ref, bc_ref, dt_ref, cw_ref, cb_ref, dtb_ref, alog_ref, dsk_ref, ng_ref, e3_ref,
              out_ref, ext_ref, s_ref):
    L = SSD_CHUNK
    gw = SSD_GROUP_WIDTH
    ns = SSD_STATE

    @pl.when(pl.program_id(1) == 0)
    def _():
        ext_ref[0:CONV_HALO, :] = jnp.zeros((CONV_HALO, SSD_CONV_DIM), F32)
        s_ref[...] = jnp.zeros_like(s_ref)

    ext_ref[CONV_HALO:CONV_HALO + L, 0:SSD_INNER] = x_ref[...].astype(F32)
    ext_ref[CONV_HALO:CONV_HALO + L, SSD_INNER:SSD_CONV_DIM] = bc_ref[...].astype(F32)

    def conv_silu(lo, width):
        ext = ext_ref[:, lo:lo + width]
        acc = cb_ref[:, lo:lo + width] + cw_ref[SSD_CONV - 1:SSD_CONV, lo:lo + width] * ext[CONV_HALO:]
        for j in range(SSD_CONV - 1):
            shifted = pltpu.roll(ext, SSD_CONV - 1 - j, 0)[CONV_HALO:]
            acc = acc + cw_ref[j:j + 1, lo:lo + width] * shifted
        return _silu(acc)

    dt = _softplus(dt_ref[...] + dtb_ref[...])
    cum = _cumsum_rows(dt * (-jnp.exp(alog_ref[...])))
    cum_t = cum.T
    cum_last = cum[L - 1:L, :]
    exp_cum = jnp.exp(cum)
    decay_to_end = jnp.exp(cum_last - cum)
    chunk_decay = jnp.exp(cum_last)
    row = lax.broadcasted_iota(jnp.int32, (L, L), 0)
    col = lax.broadcasted_iota(jnp.int32, (L, L), 1)
    causal = col <= row
    lane_gw = lax.broadcasted_iota(jnp.int32, (1, gw), 1)
    head_rows = [jnp.where(lane_gw // SSD_HEADDIM == r, 1.0, 0.0).astype(BF16)
                 for r in range(SSD_HEADS_PER_GROUP)]

    def expand(a):
        hi = a.astype(BF16)
        rest = a - hi.astype(F32)
        mid = rest.astype(BF16)
        lo = (rest - mid.astype(F32)).astype(BF16)
        return jnp.dot(jnp.concatenate([hi, mid, lo], axis=1), e3_ref[...], preferred_element_type=F32)

    dt_x = expand(dt)
    exp_cum_x = expand(exp_cum)
    decay_to_end_x = expand(decay_to_end)

    for g in range(SSD_GROUPS):
        cols = slice(g * gw, (g + 1) * gw)
        xs = conv_silu(g * gw, gw)
        bm = conv_silu(SSD_INNER + g * ns, ns).astype(BF16)
        cm = conv_silu(SSD_INNER + SSD_BC + g * ns, ns).astype(BF16)
        xdt = xs * dt_x[:, cols]
        s_in = s_ref[g]
        cb = lax.dot_general(cm, bm, _NT, preferred_element_type=F32)
        y = exp_cum_x[:, cols] * jnp.dot(cm, s_in.astype(BF16), preferred_element_type=F32)
        weights, inputs = [], []
        xdt_lo = xdt.astype(BF16)
        for r in range(SSD_HEADS_PER_GROUP):
            hh = SSD_HEADS_PER_GROUP * g + r
            decay = jnp.exp(jnp.where(causal, cum[:, hh:hh + 1] - cum_t[hh:hh + 1, :], -jnp.inf))
            weights.append((cb * decay).astype(BF16))
            inputs.append(xdt_lo * head_rows[r])
        y = y + jnp.dot(jnp.concatenate(weights, axis=1), jnp.concatenate(inputs, axis=0),
                        preferred_element_type=F32)
        xdte = (xdt * decay_to_end_x[:, cols]).astype(BF16)
        states = jnp.dot(bm.astype(F32).T.astype(BF16), xdte, preferred_element_type=F32)
        s_ref[g] = _expand_heads(chunk_decay, g, gw) * s_in + states
        y = y + dsk_ref[:, g * gw:(g + 1) * gw] * xs
        y = y * _silu(z_ref[:, g * gw:(g + 1) * gw].astype(F32))
        out_ref[:, g * gw:(g + 1) * gw] = _rms(y, ng_ref[:, g * gw:(g + 1) * gw]).astype(out_ref.dtype)

    ext_ref[0:CONV_HALO, :] = ext_ref[L:L + CONV_HALO, :]


def _ssd_core(proj, dt, conv_w, conv_b, dt_bias, a_log, d_skip, norm_g, bsz, seq):
    L = SSD_CHUNK
    nc = seq // L
    rowmap = lambda b, c: (b * nc + c, 0)
    const = lambda b, c: (0, 0)
    head_of_channel = jnp.arange(SSD_INNER, dtype=jnp.int32) // SSD_HEADDIM
    expand1 = (jnp.arange(LANES, dtype=jnp.int32)[:, None] == head_of_channel[None, :]).astype(BF16)
    expand3 = jnp.concatenate([expand1] * 3, axis=0)
    return pl.pallas_call(
        _ssd_body,
        grid=(bsz, nc),
        in_specs=[pl.BlockSpec((L, SSD_INNER), rowmap),
                  pl.BlockSpec((L, SSD_INNER), lambda b, c: (b * nc + c, 1)),
                  pl.BlockSpec((L, 2 * SSD_BC), lambda b, c: (b * nc + c, 2)),
                  pl.BlockSpec((L, LANES), rowmap),
                  pl.BlockSpec((SSD_CONV, SSD_CONV_DIM), const),
                  pl.BlockSpec((1, SSD_CONV_DIM), const),
                  pl.BlockSpec((1, LANES), const),
                  pl.BlockSpec((1, LANES), const),
                  pl.BlockSpec((1, SSD_INNER), const),
                  pl.BlockSpec((1, SSD_INNER), const),
                  pl.BlockSpec((3 * LANES, SSD_INNER), const)],
        out_specs=pl.BlockSpec((L, SSD_INNER), rowmap),
        out_shape=jax.ShapeDtypeStruct((bsz * seq, SSD_INNER), BF16),
        scratch_shapes=[pltpu.VMEM((L + CONV_HALO, SSD_CONV_DIM), F32),
                        pltpu.VMEM((SSD_GROUPS, SSD_STATE, SSD_GROUP_WIDTH), F32)],
        compiler_params=_params("parallel", "arbitrary"),
        name="ssd_core",
    )(proj, proj, proj, dt, conv_w, conv_b, dt_bias, a_log, d_skip, norm_g, expand3)


def _ssd_layer(x, prenormed, ln_g, w_in, conv_w, conv_b, dt_bias, a_log, d_skip, norm_g, bsz, seq):
    w_dt = _pad_cols(w_in[:, SSD_MAIN:], LANES).astype(BF16)
    proj, dt = _norm_proj(x, ln_g, w_in.astype(BF16), w_dt, SSD_MAIN, prenormed)
    pad_heads = lambda a: jnp.pad(a.astype(F32), (0, LANES - SSD_HEADS)).reshape(1, LANES)
    mixed = _ssd_core(proj, dt, conv_w.astype(F32), conv_b.astype(F32).reshape(1, SSD_CONV_DIM),
                      pad_heads(dt_bias), pad_heads(a_log),
                      jnp.repeat(d_skip.astype(F32), SSD_HEADDIM).reshape(1, SSD_INNER),
                      norm_g.astype(F32).reshape(1, SSD_INNER), bsz, seq)
    return mixed


def kernel(x, ln_mix, ln_mlp, w_up, w_down, ln_f, mlstm_w_in, mlstm_b_gate, mlstm_norm, mlstm_w_out, diff_w_in, diff_lam, diff_norm, diff_w_out, ssd_w_in, ssd_conv_w, ssd_conv_b, ssd_dt_bias, ssd_A_log, ssd_D, ssd_norm, ssd_w_out):
    bsz, seq, d = x.shape
    h = x.reshape(bsz * seq, d)
    w_up_lo, w_down_lo = w_up.astype(BF16), w_down.astype(BF16)
    xn = None
    for i in range(DEPTH):
        kind, j = i % N_MIXERS, i // N_MIXERS
        prenormed = xn is not None
        mixer_in = xn if prenormed else h
        if kind == 0:
            mixed = _mlstm_layer(mixer_in, prenormed, ln_mix[i], mlstm_w_in[j], mlstm_b_gate[j],
                                 mlstm_norm[j], bsz, seq)
            w_out = mlstm_w_out[j]
        elif kind == 1:
            assert prenormed, "the attention projection expects a pre-normalised input"
            mixed = _diff_layer(mixer_in, diff_w_in[j], diff_lam[j], diff_norm[j], bsz, seq, i)
            w_out = diff_w_out[j]
        else:
            mixed = _ssd_layer(mixer_in, prenormed, ln_mix[i], ssd_w_in[j], ssd_conv_w[j], ssd_conv_b[j],
                               ssd_dt_bias[j], ssd_A_log[j], ssd_D[j], ssd_norm[j], bsz, seq)
            w_out = ssd_w_out[j]
        last = i == DEPTH - 1
        out = _mix_mlp(mixed, w_out.astype(BF16), h, ln_mlp[i], w_up_lo, w_down_lo, i,
                       ln_f if last else ln_mix[i + 1], final_norm=last)
        h, xn = (out, None) if last else out
    return h.reshape(bsz, seq, d)
```

```python
import functools
import math

import jax
import jax.numpy as jnp
from jax import lax
from jax.experimental import pallas as pl
from jax.experimental.pallas import tpu as pltpu

F32 = jnp.float32
BF16 = jnp.bfloat16

DEPTH = 4
N_MIXERS = 3
EPS = 1e-6

MLSTM_HEADS = 4
MLSTM_DV = 256
MLSTM_DQK = 128
MLSTM_CHUNK = 128
GATE_SOFTCAP = 15.0

DIFF_HEADS = 8
DIFF_DH = 64
DIFF_DV = 128
ROPE_THETA = 10000.0

SSD_INNER = 2048
SSD_HEADDIM = 64
SSD_HEADS = 32
SSD_GROUPS = 8
SSD_STATE = 128
SSD_CONV = 4
SSD_CHUNK = 128
SSD_GROUP_WIDTH = SSD_INNER // SSD_GROUPS
SSD_HEADS_PER_GROUP = SSD_HEADS // SSD_GROUPS

LANES = 128
CONV_HALO = 8
VMEM_LIMIT = 48 * 1024 * 1024
VMEM_LIMIT_MLP = 56 * 1024 * 1024

_NT = (((1,), (1,)), ((), ()))


def _params(*sem):
    return pltpu.CompilerParams(dimension_semantics=sem, vmem_limit_bytes=VMEM_LIMIT)


def _rms(x, g):
    ms = jnp.mean(x * x, axis=-1, keepdims=True)
    return x * lax.rsqrt(ms + EPS) * g


def _cumsum_rows(x):
    n = x.shape[0]
    row = lax.broadcasted_iota(jnp.int32, x.shape, 0)
    s = 1
    while s < n:
        x = x + jnp.where(row >= s, pltpu.roll(x, s, 0), 0.0)
        s *= 2
    return x


def _cummax_rows(x):
    n = x.shape[0]
    row = lax.broadcasted_iota(jnp.int32, x.shape, 0)
    s = 1
    while s < n:
        x = jnp.maximum(x, jnp.where(row >= s, pltpu.roll(x, s, 0), -jnp.inf))
        s *= 2
    return x


COL_REDUCE_SLAB = 64


def _col_reduce(x, pair_op, reduce_fn):
    slab = min(COL_REDUCE_SLAB, x.shape[0])
    acc = x[:slab]
    for r0 in range(slab, x.shape[0], slab):
        acc = pair_op(acc, x[r0:r0 + slab])
    while acc.shape[0] > 8:
        half = acc.shape[0] // 2
        acc = pair_op(acc[:half], acc[half:])
    return reduce_fn(acc, axis=0, keepdims=True)


def _sigmoid(x):
    return 0.5 + 0.5 * jnp.tanh(0.5 * x)


def _silu(x):
    half = 0.5 * x
    return half + half * jnp.tanh(half)


def _softplus(x):
    return jnp.maximum(x, 0.0) + jnp.log1p(jnp.exp(-jnp.abs(x)))


def _pad_cols(w, n):
    return jnp.pad(w, ((0, 0), (0, n - w.shape[1])))


def _norm_proj_body(x_ref, g_ref, w_ref, ws_ref, o_ref, os_ref, *scratch):
    first = pl.program_id(1) == 0
    if scratch:
        xn_ref, = scratch

        @pl.when(first)
        def _():
            xn_ref[...] = _rms(x_ref[...], g_ref[...]).astype(BF16)
    else:
        xn_ref = x_ref

    @pl.when(first)
    def _():
        os_ref[...] = jnp.dot(xn_ref[...], ws_ref[...], preferred_element_type=F32)

    o_ref[...] = jnp.dot(xn_ref[...], w_ref[...], preferred_element_type=F32).astype(o_ref.dtype)


def _norm_proj(x, g, w, w_side, n_main, prenormed, tn=3072, tm=1024):
    n, d = x.shape
    tm = min(tm, n)
    return pl.pallas_call(
        _norm_proj_body,
        grid=(n // tm, n_main // tn),
        in_specs=[pl.BlockSpec((tm, d), lambda i, j: (i, 0)),
                  pl.BlockSpec((1, d), lambda i, j: (0, 0)),
                  pl.BlockSpec((d, tn), lambda i, j: (0, j)),
                  pl.BlockSpec((d, LANES), lambda i, j: (0, 0))],
        out_specs=[pl.BlockSpec((tm, tn), lambda i, j: (i, j)),
                   pl.BlockSpec((tm, LANES), lambda i, j: (i, 0))],
        out_shape=[jax.ShapeDtypeStruct((n, n_main), BF16), jax.ShapeDtypeStruct((n, LANES), F32)],
        scratch_shapes=[] if prenormed else [pltpu.VMEM((tm, d), BF16)],
        compiler_params=_params("parallel", "arbitrary"),
        name="norm_proj",
    )(x, g.reshape(1, d), w, w_side)


ROPE_ROW_SPLIT = 4


def _rope_proj_body(xn_ref, w_ref, cos_ref, sin_ref, o_ref, vt_ref):
    j = pl.program_id(1)
    rows = xn_ref.shape[0] // ROPE_ROW_SPLIT

    @pl.when(j < 2)
    def _():
        qscale = jnp.where(j == 0, DIFF_DH ** -0.5 * math.log2(math.e), 1.0).astype(F32)
        for r in range(ROPE_ROW_SPLIT):
            slab = slice(r * rows, (r + 1) * rows)
            acc = jnp.dot(xn_ref[slab, :], w_ref[...], preferred_element_type=F32)
            cos = cos_ref[slab, :] * qscale
            sin = sin_ref[slab, :] * qscale
            for c in range(acc.shape[1] // LANES):
                xc = acc[:, c * LANES:(c + 1) * LANES]
                rot = pltpu.roll(xc, LANES // 2, 1)
                o_ref[slab, c * LANES:(c + 1) * LANES] = (xc * cos + rot * sin).astype(o_ref.dtype)

    @pl.when(j >= 2)
    def _():
        for r in range(ROPE_ROW_SPLIT):
            slab = slice(r * rows, (r + 1) * rows)
            acc = jnp.dot(xn_ref[slab, :], w_ref[...], preferred_element_type=F32)
            vt_ref[:, slab] = acc.T.astype(vt_ref.dtype)


def _rope_proj(xn, w, cos, sin, bsz, seq, tm=1024):
    n, d = xn.shape
    tm = min(tm, seq)
    tn = d
    nseq = seq // tm
    return pl.pallas_call(
        _rope_proj_body,
        grid=(n // tm, 3),
        in_specs=[pl.BlockSpec((tm, d), lambda i, j: (i, 0)),
                  pl.BlockSpec((d, tn), lambda i, j: (0, j)),
                  pl.BlockSpec((tm, LANES), lambda i, j: (i % nseq, 0)),
                  pl.BlockSpec((tm, LANES), lambda i, j: (i % nseq, 0))],
        out_specs=[pl.BlockSpec((tm, tn), lambda i, j: (i, jnp.minimum(j, 1))),
                   pl.BlockSpec((d, tm), lambda i, j: (i // nseq, i % nseq))],
        out_shape=[jax.ShapeDtypeStruct((n, 2 * d), BF16), jax.ShapeDtypeStruct((bsz * d, seq), BF16)],
        compiler_params=_params("parallel", "arbitrary"),
        name="rope_proj",
    )(xn, w, cos, sin)


MLP_ROW_SPLIT = 4


def _mix_mlp_body(a_ref, wo_ref, h_ref, g_ref, wu_ref, wd_ref, gn_ref, o_ref, *rest, final_norm):
    xn_ref = rest[-1]
    j = pl.program_id(1)

    rows = o_ref.shape[0] // MLP_ROW_SPLIT
    slabs = [slice(r * rows, (r + 1) * rows) for r in range(MLP_ROW_SPLIT)]

    @pl.when(j == 0)
    def _():
        for slab in slabs:
            h1 = h_ref[slab, :] + jnp.dot(a_ref[slab, :], wo_ref[...], preferred_element_type=F32)
            o_ref[slab, :] = h1
            xn_ref[slab, :] = _rms(h1, g_ref[...]).astype(BF16)

    def up_proj():
        up = jnp.dot(xn_ref[...], wu_ref[...], preferred_element_type=F32)
        return jnp.square(jnp.maximum(up, 0.0)).astype(BF16)

    last = j == pl.num_programs(1) - 1

    @pl.when(jnp.logical_not(last))
    def _():
        o_ref[...] += jnp.dot(up_proj(), wd_ref[...], preferred_element_type=F32)

    @pl.when(last)
    def _():
        up = up_proj()
        for slab in slabs:
            total = o_ref[slab, :] + jnp.dot(up[slab, :], wd_ref[...], preferred_element_type=F32)
            normed = _rms(total, gn_ref[...])
            if final_norm:
                o_ref[slab, :] = normed
            else:
                o_ref[slab, :] = total
                rest[0][slab, :] = normed.astype(BF16)


def _mix_mlp(a, w_out, h, g, w_up, w_down, layer, g_next, final_norm, tm=1024, tf=1024):
    n, d = h.shape
    k = a.shape[1]
    dff = w_up.shape[2]
    tm = min(tm, n)
    row = lambda i, j: (i, 0)
    const = lambda i, j: (0, 0)
    return pl.pallas_call(
        functools.partial(_mix_mlp_body, final_norm=final_norm),
        grid=(n // tm, dff // tf),
        in_specs=[pl.BlockSpec((tm, k), row),
                  pl.BlockSpec((k, d), const),
                  pl.BlockSpec((tm, d), row),
                  pl.BlockSpec((1, d), const),
                  pl.BlockSpec((None, d, tf), lambda i, j: (layer, 0, j)),
                  pl.BlockSpec((None, tf, d), lambda i, j: (layer, j, 0)),
                  pl.BlockSpec((1, d), const)],
        out_specs=pl.BlockSpec((tm, d), row) if final_norm else [pl.BlockSpec((tm, d), row)] * 2,
        out_shape=(jax.ShapeDtypeStruct((n, d), F32) if final_norm else
                   [jax.ShapeDtypeStruct((n, d), F32), jax.ShapeDtypeStruct((n, d), BF16)]),
        scratch_shapes=[pltpu.VMEM((tm, d), BF16)],
        compiler_params=pltpu.CompilerParams(dimension_semantics=("parallel", "arbitrary"),
                                             vmem_limit_bytes=VMEM_LIMIT_MLP),
        name="mix_mlp",
    )(a, w_out, h, g.reshape(1, d), w_up, w_down, g_next.reshape(1, d))


MLSTM_QK = MLSTM_HEADS * MLSTM_DQK
MLSTM_MAIN = 2 * MLSTM_QK + 2 * MLSTM_HEADS * MLSTM_DV


def _mlstm_body(p_ref, gate_ref, bias_ref, ng_ref, out_ref, *state_refs):
    L = MLSTM_CHUNK
    nh = MLSTM_HEADS
    n_batch = p_ref.shape[0]
    n_chain = n_batch * nh
    cn_refs = state_refs[:n_chain]
    m_refs = state_refs[n_chain:]
    ones_cols = jnp.ones((L, LANES), BF16)

    @pl.when(pl.program_id(0) == 0)
    def _():
        for ref in state_refs:
            ref[...] = jnp.zeros_like(ref)

    row = lax.broadcasted_iota(jnp.int32, (L, L), 0)
    col = lax.broadcasted_iota(jnp.int32, (L, L), 1)
    causal = col <= row
    qscale = MLSTM_DQK ** -0.5

    for bi in range(n_batch):
        gates = gate_ref[bi] + bias_ref[...]
        capped = GATE_SOFTCAP * jnp.tanh(gates / GATE_SOFTCAP)
        log_f = -_softplus(-capped)
        log_i = pltpu.roll(capped, nh, 1)
        b = _cumsum_rows(log_f)
        g = log_i - b
        g_max = _cummax_rows(g)
        m_in = m_refs[bi][...]
        m_run = jnp.maximum(m_in, g_max)
        m_t = b + m_run
        scale_all = jnp.exp(m_in - m_run)
        floor_all = jnp.exp(-m_t)
        b_last = b[L - 1:L, :]
        g_max_last = g_max[L - 1:L, :]
        m_run_last = m_run[L - 1:L, :]
        w_all = jnp.exp(g - g_max_last)
        a_old_all = jnp.exp(m_in - m_run_last)
        a_loc_all = jnp.exp(g_max_last - m_run_last)
        m_refs[bi][...] = b_last + m_run_last
        g_t = g.T

        for h in range(nh):
            st = bi * nh + h
            ln = nh + h
            v_off = 2 * MLSTM_QK + h * MLSTM_DV
            o_off = v_off + nh * MLSTM_DV
            q = p_ref[bi, :, h * MLSTM_DQK:(h + 1) * MLSTM_DQK]
            k = p_ref[bi, :, MLSTM_QK + h * MLSTM_DQK:MLSTM_QK + (h + 1) * MLSTM_DQK]
            v_ones = jnp.concatenate([p_ref[bi, :, v_off:v_off + MLSTM_DV], ones_cols], axis=1)
            cn_in = cn_refs[st][...]

            decay = jnp.exp(jnp.where(causal, g_t[ln:ln + 1, :] - m_run[:, ln:ln + 1], -jnp.inf))
            qk = lax.dot_general(q, k, _NT, preferred_element_type=F32) * qscale
            p = decay * qk
            scale = scale_all[:, ln:ln + 1]
            inter = jnp.dot(q, cn_in.astype(BF16), preferred_element_type=F32) * qscale
            both = jnp.dot(p.astype(BF16), v_ones, preferred_element_type=F32) + scale * inter
            den = both[:, MLSTM_DV:MLSTM_DV + 1]
            hs = both[:, :MLSTM_DV] / jnp.maximum(jnp.abs(den), floor_all[:, ln:ln + 1])

            kw = k.astype(F32) * w_all[:, ln:ln + 1]
            cn_loc = jnp.dot(kw.T.astype(BF16), v_ones, preferred_element_type=F32)
            cn_refs[st][...] = a_old_all[:, ln:ln + 1] * cn_in + a_loc_all[:, ln:ln + 1] * cn_loc

            hn = _rms(hs, ng_ref[:, h * MLSTM_DV:(h + 1) * MLSTM_DV])
            o_gate = _sigmoid(p_ref[bi, :, o_off:o_off + MLSTM_DV].astype(F32))
            out_ref[bi, :, h * MLSTM_DV:(h + 1) * MLSTM_DV] = (o_gate * hn).astype(out_ref.dtype)


def _mlstm_core(proj, gates, bias, norm_g, bsz, seq):
    L = MLSTM_CHUNK
    d = MLSTM_HEADS * MLSTM_DV
    n_chain = bsz * MLSTM_HEADS
    chunk = lambda c: (0, c, 0)
    return pl.pallas_call(
        _mlstm_body,
        grid=(seq // L,),
        in_specs=[pl.BlockSpec((bsz, L, MLSTM_MAIN), chunk),
                  pl.BlockSpec((bsz, L, LANES), chunk),
                  pl.BlockSpec((1, LANES), lambda c: (0, 0)),
                  pl.BlockSpec((1, d), lambda c: (0, 0))],
        out_specs=pl.BlockSpec((bsz, L, d), chunk),
        out_shape=jax.ShapeDtypeStruct((bsz, seq, d), BF16),
        scratch_shapes=([pltpu.VMEM((MLSTM_DQK, MLSTM_DV + LANES), F32)] * n_chain
                        + [pltpu.VMEM((1, LANES), F32)] * bsz),
        compiler_params=_params("arbitrary"),
        name="mlstm_core",
    )(proj.reshape(bsz, seq, MLSTM_MAIN), gates.reshape(bsz, seq, LANES), bias, norm_g.reshape(1, d))


def _mlstm_layer(x, prenormed, ln_g, w_in, b_gate, norm_g, bsz, seq):
    w_gate = _pad_cols(w_in[:, MLSTM_MAIN:], LANES).astype(BF16)
    bias = jnp.pad(b_gate.astype(F32), (0, LANES - b_gate.shape[0])).reshape(1, LANES)
    proj, gates = _norm_proj(x, ln_g, w_in.astype(BF16), w_gate, MLSTM_MAIN, prenormed)
    mixed = _mlstm_core(proj, gates, bias, norm_g, bsz, seq)
    return mixed.reshape(bsz * seq, -1)


ATTN_PAIRS_PER_TRIP = 4


def _diff_attn_body(lam_ref, g_ref, q_ref, qn_ref, k_ref, vt_ref, o_ref, sc_a, sc_b, sc_c, cm_a, cm_b, cm_c,
                    m_ref, l_ref, acc_ref, *, tq, lam_init):
    qi = pl.program_id(2)
    q = q_ref[...]
    lane = lax.broadcasted_iota(jnp.int32, q.shape, 1)
    zero = jnp.zeros_like(q)
    is_c1 = (lane & (DIFF_DH // 2)) != 0
    th = tq // 2
    chains = [(c, half) for c in range(2) for half in range(2)]

    def chain_queries(qblk):
        parts = (jnp.where(is_c1, zero, qblk), jnp.where(is_c1, qblk, zero))
        return [parts[c][half * th:(half + 1) * th] for c, half in chains]

    q_ch = chain_queries(q)
    kpos = lax.broadcasted_iota(jnp.int32, (tq, th), 0)
    qpos = lax.broadcasted_iota(jnp.int32, (tq, th), 1)
    causal = [kpos <= qpos + half * th for half in range(2)]
    ones_rows = jnp.ones((16, tq), BF16)

    buf_a, buf_b, buf_c = (sc_a, cm_a), (sc_b, cm_b), (sc_c, cm_c)

    def put_scores(buf, ki, queries=q_ch):
        sc_ref, cm_ref = buf
        k = k_ref[pl.ds(pl.multiple_of(ki * tq, tq), tq), :]
        for x in range(4):
            s = lax.dot_general(k, queries[x], _NT, preferred_element_type=F32)
            sc_ref[x] = s
            cm_ref[x] = _col_reduce(s, jnp.maximum, jnp.max)

    def softmax_pv(ki, buf, masked):
        sc_ref, cm_ref = buf
        vt = vt_ref[:, pl.ds(pl.multiple_of(ki * tq, tq), tq)]
        vt_ones = jnp.concatenate([vt, ones_rows], axis=0)
        for x, (c, half) in enumerate(chains):
            nkeys = th if masked and half == 0 else tq

            def load_scores():
                s = sc_ref[x, :nkeys, :]
                return jnp.where(causal[half][:nkeys], s, -jnp.inf) if masked else s

            m = m_ref[x]
            col_max = _col_reduce(load_scores(), jnp.maximum, jnp.max) if masked else cm_ref[x]
            m_new = jnp.maximum(m, col_max)
            alpha = jnp.exp2(m - m_new)
            p = jnp.exp2(load_scores() - m_new)
            pv = jnp.dot(vt_ones[:, :nkeys], p.astype(BF16), preferred_element_type=F32)
            m_ref[x] = m_new
            l_ref[x] = alpha * l_ref[x] + pv[DIFF_DV:DIFF_DV + 1]
            acc_ref[x] = alpha * acc_ref[x] + pv[:DIFF_DV]

    m_ref[...] = jnp.full(m_ref.shape, -jnp.inf, F32)
    l_ref[...] = jnp.zeros_like(l_ref)
    acc_ref[...] = jnp.zeros_like(acc_ref)

    odd = qi & 1

    @pl.when(qi == 0)
    def _():
        put_scores(buf_a, 0)

    @pl.when(odd == 1)
    def _():
        put_scores(buf_a, 1)
        softmax_pv(0, buf_c, False)

    @pl.when((qi > 0) & (odd == 0))
    def _():
        put_scores(buf_b, 1)
        softmax_pv(0, buf_c, False)
        put_scores(buf_a, 2)
        softmax_pv(1, buf_b, False)

    first = 2 - odd

    def pair(b0):
        put_scores(buf_b, b0 + 1)
        softmax_pv(b0, buf_a, False)
        put_scores(buf_a, b0 + 2)
        softmax_pv(b0 + 1, buf_b, False)

    n_pairs = lax.shift_right_logical(jnp.maximum(qi - first, 0), 1)
    done = 0
    group = 1
    while group < ATTN_PAIRS_PER_TRIP:
        start = first + 2 * done

        @pl.when((n_pairs & group) != 0)
        def _(start=start, group=group):
            for g in range(group):
                pair(start + 2 * g)

        done = done + (n_pairs & group)
        group *= 2

    def trip(j, carry):
        b0 = first + 2 * done + 2 * ATTN_PAIRS_PER_TRIP * j
        for g in range(ATTN_PAIRS_PER_TRIP):
            pair(b0 + 2 * g)
        return carry

    lax.fori_loop(0, lax.shift_right_logical(n_pairs, ATTN_PAIRS_PER_TRIP.bit_length() - 1), trip, 0)

    put_scores(buf_c, 0, chain_queries(qn_ref[...]))
    softmax_pv(qi, buf_a, True)
    a0, a1 = (jnp.concatenate([acc_ref[2 * c] / l_ref[2 * c], acc_ref[2 * c + 1] / l_ref[2 * c + 1]], axis=1)
              for c in range(2))

    lam = lam_ref[...]
    lam_val = (jnp.exp(jnp.sum(lam[0:1] * lam[1:2], axis=1, keepdims=True))
               - jnp.exp(jnp.sum(lam[2:3] * lam[3:4], axis=1, keepdims=True)) + lam_init)
    o = a0 - lam_val * a1
    ms = jnp.mean(o * o, axis=0, keepdims=True)
    y = o * lax.rsqrt(ms + EPS) * g_ref[...] * (1.0 - lam_init)
    o_ref[...] = y.T.astype(o_ref.dtype)


def _diff_attn(qk, v_t, lam, norm_g, bsz, seq, layer_idx, tq=512):
    tq = min(tq, seq)
    nq = seq // tq
    nh = DIFF_HEADS
    lam_init = 0.8 - 0.6 * math.exp(-0.3 * layer_idx)
    return pl.pallas_call(
        functools.partial(_diff_attn_body, tq=tq, lam_init=lam_init),
        grid=(bsz, nh, nq),
        in_specs=[pl.BlockSpec((4, DIFF_DH), lambda b, h, i: (0, 0)),
                  pl.BlockSpec((DIFF_DV, 1), lambda b, h, i: (0, 0)),
                  pl.BlockSpec((tq, DIFF_DV), lambda b, h, i: (b * nq + i, h)),
                  pl.BlockSpec((tq, DIFF_DV), lambda b, h, i: (b * nq + jnp.minimum(i + 1, nq - 1), h)),
                  pl.BlockSpec((seq, DIFF_DV), lambda b, h, i: (b, nh + h)),
                  pl.BlockSpec((DIFF_DV, seq), lambda b, h, i: (b * nh + h, 0))],
        out_specs=pl.BlockSpec((tq, DIFF_DV), lambda b, h, i: (b * nq + i, h)),
        out_shape=jax.ShapeDtypeStruct((bsz * seq, nh * DIFF_DV), BF16),
        scratch_shapes=[pltpu.VMEM((4, tq, tq // 2), F32)] * 3 + [pltpu.VMEM((4, 1, tq // 2), F32)] * 3 + [
                        pltpu.VMEM((4, 1, tq // 2), F32), pltpu.VMEM((4, 1, tq // 2), F32),
                        pltpu.VMEM((4, DIFF_DV, tq // 2), F32)],
        compiler_params=_params("arbitrary", "arbitrary", "arbitrary"),
        name="diff_attn",
    )(lam.astype(F32), norm_g.reshape(DIFF_DV, 1).astype(F32), qk, qk, qk, v_t)


def _rope_tables(seq):
    inv = ROPE_THETA ** (-jnp.arange(0, DIFF_DH, 2, dtype=F32) / DIFF_DH)
    ang = jnp.arange(seq, dtype=F32)[:, None] * inv[None, :]
    cos, sin = jnp.cos(ang), jnp.sin(ang)
    cos_t = jnp.tile(cos, (1, LANES // cos.shape[1]))
    sin_t = jnp.concatenate([-sin, -sin, sin, sin], axis=1)
    return cos_t, sin_t


def _rope_lane_order():
    half = DIFF_DH // 2
    return [c * DIFF_DH + part * half + i for part in range(2) for c in range(2) for i in range(half)]


def _diff_layer(xn, w_in, lam, norm_g, bsz, seq, layer_idx):
    cos, sin = _rope_tables(seq)
    order = jnp.asarray(_rope_lane_order(), jnp.int32)
    nqk = 2 * DIFF_HEADS * DIFF_DV
    qk_cols = (jnp.arange(nqk // LANES, dtype=jnp.int32) * LANES)[:, None] + order[None, :]
    cols = jnp.concatenate([qk_cols.reshape(-1), jnp.arange(nqk, w_in.shape[1], dtype=jnp.int32)])
    qk, v_t = _rope_proj(xn, w_in.astype(BF16)[:, cols], cos, sin, bsz, seq)
    return _diff_attn(qk, v_t, lam, norm_g, bsz, seq, layer_idx)


SSD_BC = SSD_GROUPS * SSD_STATE
SSD_CONV_DIM = SSD_INNER + 2 * SSD_BC
SSD_MAIN = SSD_INNER + SSD_CONV_DIM


def _expand_heads(a, g, width):
    nr = SSD_HEADS_PER_GROUP
    lane = lax.broadcasted_iota(jnp.int32, (a.shape[0], width), 1)
    out = jnp.broadcast_to(a[:, nr * g + nr - 1:nr * g + nr], (a.shape[0], width))
    for r in range(nr - 2, -1, -1):
        out = jnp.where(lane < SSD_HEADDIM * (r + 1), a[:, nr * g + r:nr * g + r + 1], out)
    return out


def _ssd_body(z_ref, x_ref, bc_ref, dt_ref, cw_ref, cb_ref, dtb_ref, alog_ref, dsk_ref, ng_ref, e3_ref,
              out_ref, ext_ref, s_ref):
    L = SSD_CHUNK
    gw = SSD_GROUP_WIDTH
    ns = SSD_STATE

    @pl.when(pl.program_id(1) == 0)
    def _():
        ext_ref[0:CONV_HALO, :] = jnp.zeros((CONV_HALO, SSD_CONV_DIM), F32)
        s_ref[...] = jnp.zeros_like(s_ref)

    ext_ref[CONV_HALO:CONV_HALO + L, 0:SSD_INNER] = x_ref[...].astype(F32)
    ext_ref[CONV_HALO:CONV_HALO + L, SSD_INNER:SSD_CONV_DIM] = bc_ref[...].astype(F32)

    def conv_silu(lo, width):
        ext = ext_ref[:, lo:lo + width]
        acc = cb_ref[:, lo:lo + width] + cw_ref[SSD_CONV - 1:SSD_CONV, lo:lo + width] * ext[CONV_HALO:]
        for j in range(SSD_CONV - 1):
            shifted = pltpu.roll(ext, SSD_CONV - 1 - j, 0)[CONV_HALO:]
            acc = acc + cw_ref[j:j + 1, lo:lo + width] * shifted
        return _silu(acc)

    dt = _softplus(dt_ref[...] + dtb_ref[...])
    cum = _cumsum_rows(dt * (-jnp.exp(alog_ref[...])))
    cum_t = cum.T
    cum_last = cum[L - 1:L, :]
    exp_cum = jnp.exp(cum)
    decay_to_end = jnp.exp(cum_last - cum)
    chunk_decay = jnp.exp(cum_last)
    row = lax.broadcasted_iota(jnp.int32, (L, L), 0)
    col = lax.broadcasted_iota(jnp.int32, (L, L), 1)
    causal = col <= row
    lane_gw = lax.broadcasted_iota(jnp.int32, (1, gw), 1)
    head_rows = [jnp.where(lane_gw // SSD_HEADDIM == r, 1.0, 0.0).astype(BF16)
                 for r in range(SSD_HEADS_PER_GROUP)]

    def expand(a):
        hi = a.astype(BF16)
        rest = a - hi.astype(F32)
        mid = rest.astype(BF16)
        lo = (rest - mid.astype(F32)).astype(BF16)
        return jnp.dot(jnp.concatenate([hi, mid, lo], axis=1), e3_ref[...], preferred_element_type=F32)

    dt_x = expand(dt)
    exp_cum_x = expand(exp_cum)
    decay_to_end_x = expand(decay_to_end)

    for g in range(SSD_GROUPS):
        cols = slice(g * gw, (g + 1) * gw)
        xs = conv_silu(g * gw, gw)
        bm = conv_silu(SSD_INNER + g * ns, ns).astype(BF16)
        cm = conv_silu(SSD_INNER + SSD_BC + g * ns, ns).astype(BF16)
        xdt = xs * dt_x[:, cols]
        s_in = s_ref[g]
        cb = lax.dot_general(cm, bm, _NT, preferred_element_type=F32)
        y = exp_cum_x[:, cols] * jnp.dot(cm, s_in.astype(BF16), preferred_element_type=F32)
        weights, inputs = [], []
        xdt_lo = xdt.astype(BF16)
        for r in range(SSD_HEADS_PER_GROUP):
            hh = SSD_HEADS_PER_GROUP * g + r
            decay = jnp.exp(jnp.where(causal, cum[:, hh:hh + 1] - cum_t[hh:hh + 1, :], -jnp.inf))
            weights.append((cb * decay).astype(BF16))
            inputs.append(xdt_lo * head_rows[r])
        y = y + jnp.dot(jnp.concatenate(weights, axis=1), jnp.concatenate(inputs, axis=0),
                        preferred_element_type=F32)
        xdte = (xdt * decay_to_end_x[:, cols]).astype(BF16)
        states = jnp.dot(bm.astype(F32).T.astype(BF16), xdte, preferred_element_type=F32)
        s_ref[g] = _expand_heads(chunk_decay, g, gw) * s_in + states
        y = y + dsk_ref[:, g * gw:(g + 1) * gw] * xs
        y = y * _silu(z_ref[:, g * gw:(g + 1) * gw].astype(F32))
        out_ref[:, g * gw:(g + 1) * gw] = _rms(y, ng_ref[:, g * gw:(g + 1) * gw]).astype(out_ref.dtype)

    ext_ref[0:CONV_HALO, :] = ext_ref[L:L + CONV_HALO, :]


def _ssd_core(proj, dt, conv_w, conv_b, dt_bias, a_log, d_skip, norm_g, bsz, seq):
    L = SSD_CHUNK
    nc = seq // L
    rowmap = lambda b, c: (b * nc + c, 0)
    const = lambda b, c: (0, 0)
    head_of_channel = jnp.arange(SSD_INNER, dtype=jnp.int32) // SSD_HEADDIM
    expand1 = (jnp.arange(LANES, dtype=jnp.int32)[:, None] == head_of_channel[None, :]).astype(BF16)
    expand3 = jnp.concatenate([expand1] * 3, axis=0)
    return pl.pallas_call(
        _ssd_body,
        grid=(bsz, nc),
        in_specs=[pl.BlockSpec((L, SSD_INNER), rowmap),
                  pl.BlockSpec((L, SSD_INNER), lambda b, c: (b * nc + c, 1)),
                  pl.BlockSpec((L, 2 * SSD_BC), lambda b, c: (b * nc + c, 2)),
                  pl.BlockSpec((L, LANES), rowmap),
                  pl.BlockSpec((SSD_CONV, SSD_CONV_DIM), const),
                  pl.BlockSpec((1, SSD_CONV_DIM), const),
                  pl.BlockSpec((1, LANES), const),
                  pl.BlockSpec((1, LANES), const),
                  pl.BlockSpec((1, SSD_INNER), const),
                  pl.BlockSpec((1, SSD_INNER), const),
                  pl.BlockSpec((3 * LANES, SSD_INNER), const)],
        out_specs=pl.BlockSpec((L, SSD_INNER), rowmap),
        out_shape=jax.ShapeDtypeStruct((bsz * seq, SSD_INNER), BF16),
        scratch_shapes=[pltpu.VMEM((L + CONV_HALO, SSD_CONV_DIM), F32),
                        pltpu.VMEM((SSD_GROUPS, SSD_STATE, SSD_GROUP_WIDTH), F32)],
        compiler_params=_params("parallel", "arbitrary"),
        name="ssd_core",
    )(proj, proj, proj, dt, conv_w, conv_b, dt_bias, a_log, d_skip, norm_g, expand3)


def _ssd_layer(x, prenormed, ln_g, w_in, conv_w, conv_b, dt_bias, a_log, d_skip, norm_g, bsz, seq):
    w_dt = _pad_cols(w_in[:, SSD_MAIN:], LANES).astype(BF16)
    proj, dt = _norm_proj(x, ln_g, w_in.astype(BF16), w_dt, SSD_MAIN, prenormed)
    pad_heads = lambda a: jnp.pad(a.astype(F32), (0, LANES - SSD_HEADS)).reshape(1, LANES)
    mixed = _ssd_core(proj, dt, conv_w.astype(F32), conv_b.astype(F32).reshape(1, SSD_CONV_DIM),
                      pad_heads(dt_bias), pad_heads(a_log),
                      jnp.repeat(d_skip.astype(F32), SSD_HEADDIM).reshape(1, SSD_INNER),
                      norm_g.astype(F32).reshape(1, SSD_INNER), bsz, seq)
    return mixed


def kernel(x, ln_mix, ln_mlp, w_up, w_down, ln_f, mlstm_w_in, mlstm_b_gate, mlstm_norm, mlstm_w_out, diff_w_in, diff_lam, diff_norm, diff_w_out, ssd_w_in, ssd_conv_w, ssd_conv_b, ssd_dt_bias, ssd_A_log, ssd_D, ssd_norm, ssd_w_out):
    bsz, seq, d = x.shape
    h = x.reshape(bsz * seq, d)
    w_up_lo, w_down_lo = w_up.astype(BF16), w_down.astype(BF16)
    xn = None
    for i in range(DEPTH):
        kind, j = i % N_MIXERS, i // N_MIXERS
        prenormed = xn is not None
        mixer_in = xn if prenormed else h
        if kind == 0:
            mixed = _mlstm_layer(mixer_in, prenormed, ln_mix[i], mlstm_w_in[j], mlstm_b_gate[j],
                                 mlstm_norm[j], bsz, seq)
            w_out = mlstm_w_out[j]
        elif kind == 1:
            assert prenormed, "the attention projection expects a pre-normalised input"
            mixed = _diff_layer(mixer_in, diff_w_in[j], diff_lam[j], diff_norm[j], bsz, seq, i)
            w_out = diff_w_out[j]
        else:
            mixed = _ssd_layer(mixer_in, prenormed, ln_mix[i], ssd_w_in[j], ssd_conv_w[j], ssd_conv_b[j],
                               ssd_dt_bias[j], ssd_A_log[j], ssd_D[j], ssd_norm[j], bsz, seq)
            w_out = ssd_w_out[j]
        last = i == DEPTH - 1
        out = _mix_mlp(mixed, w_out.astype(BF16), h, ln_mlp[i], w_up_lo, w_down_lo, i,
                       ln_f if last else ln_mix[i + 1], final_norm=last)
        h, xn = (out, None) if last else out
    return h.reshape(bsz, seq, d)
```

```python
import functools
import math

import jax
import jax.numpy as jnp
from jax import lax
from jax.experimental import pallas as pl
from jax.experimental.pallas import tpu as pltpu

F32 = jnp.float32
BF16 = jnp.bfloat16

DEPTH = 4
N_MIXERS = 3
EPS = 1e-6

MLSTM_HEADS = 4
MLSTM_DV = 256
MLSTM_DQK = 128
MLSTM_CHUNK = 128
GATE_SOFTCAP = 15.0

DIFF_HEADS = 8
DIFF_DH = 64
DIFF_DV = 128
ROPE_THETA = 10000.0

SSD_INNER = 2048
SSD_HEADDIM = 64
SSD_HEADS = 32
SSD_GROUPS = 8
SSD_STATE = 128
SSD_CONV = 4
SSD_CHUNK = 128
SSD_GROUP_WIDTH = SSD_INNER // SSD_GROUPS
SSD_HEADS_PER_GROUP = SSD_HEADS // SSD_GROUPS

LANES = 128
CONV_HALO = 8
VMEM_LIMIT = 48 * 1024 * 1024
VMEM_LIMIT_MLP = 56 * 1024 * 1024

_NT = (((1,), (1,)), ((), ()))


def _params(*sem):
    return pltpu.CompilerParams(dimension_semantics=sem, vmem_limit_bytes=VMEM_LIMIT)


def _rms(x, g):
    ms = jnp.mean(x * x, axis=-1, keepdims=True)
    return x * lax.rsqrt(ms + EPS) * g


def _cumsum_rows(x):
    n = x.shape[0]
    row = lax.broadcasted_iota(jnp.int32, x.shape, 0)
    s = 1
    while s < n:
        x = x + jnp.where(row >= s, pltpu.roll(x, s, 0), 0.0)
        s *= 2
    return x


def _cummax_rows(x):
    n = x.shape[0]
    row = lax.broadcasted_iota(jnp.int32, x.shape, 0)
    s = 1
    while s < n:
        x = jnp.maximum(x, jnp.where(row >= s, pltpu.roll(x, s, 0), -jnp.inf))
        s *= 2
    return x


COL_REDUCE_SLAB = 64


def _col_reduce(x, pair_op, reduce_fn):
    slab = min(COL_REDUCE_SLAB, x.shape[0])
    acc = x[:slab]
    for r0 in range(slab, x.shape[0], slab):
        acc = pair_op(acc, x[r0:r0 + slab])
    while acc.shape[0] > 8:
        half = acc.shape[0] // 2
        acc = pair_op(acc[:half], acc[half:])
    return reduce_fn(acc, axis=0, keepdims=True)


def _sigmoid(x):
    return 0.5 + 0.5 * jnp.tanh(0.5 * x)


def _silu(x):
    half = 0.5 * x
    return half + half * jnp.tanh(half)


def _softplus(x):
    return jnp.maximum(x, 0.0) + jnp.log1p(jnp.exp(-jnp.abs(x)))


def _pad_cols(w, n):
    return jnp.pad(w, ((0, 0), (0, n - w.shape[1])))


def _norm_proj_body(x_ref, g_ref, w_ref, ws_ref, o_ref, os_ref, *scratch):
    first = pl.program_id(1) == 0
    if scratch:
        xn_ref, = scratch

        @pl.when(first)
        def _():
            xn_ref[...] = _rms(x_ref[...], g_ref[...]).astype(BF16)
    else:
        xn_ref = x_ref

    @pl.when(first)
    def _():
        os_ref[...] = jnp.dot(xn_ref[...], ws_ref[...], preferred_element_type=F32)

    o_ref[...] = jnp.dot(xn_ref[...], w_ref[...], preferred_element_type=F32).astype(o_ref.dtype)


def _norm_proj(x, g, w, w_side, n_main, prenormed, tn=3072, tm=1024):
    n, d = x.shape
    tm = min(tm, n)
    return pl.pallas_call(
        _norm_proj_body,
        grid=(n // tm, n_main // tn),
        in_specs=[pl.BlockSpec((tm, d), lambda i, j: (i, 0)),
                  pl.BlockSpec((1, d), lambda i, j: (0, 0)),
                  pl.BlockSpec((d, tn), lambda i, j: (0, j)),
                  pl.BlockSpec((d, LANES), lambda i, j: (0, 0))],
        out_specs=[pl.BlockSpec((tm, tn), lambda i, j: (i, j)),
                   pl.BlockSpec((tm, LANES), lambda i, j: (i, 0))],
        out_shape=[jax.ShapeDtypeStruct((n, n_main), BF16), jax.ShapeDtypeStruct((n, LANES), F32)],
        scratch_shapes=[] if prenormed else [pltpu.VMEM((tm, d), BF16)],
        compiler_params=_params("parallel", "arbitrary"),
        name="norm_proj",
    )(x, g.reshape(1, d), w, w_side)


ROPE_ROW_SPLIT = 4


def _rope_proj_body(xn_ref, w_ref, cos_ref, sin_ref, o_ref, vt_ref):
    d = xn_ref.shape[1]
    rows = xn_ref.shape[0] // ROPE_ROW_SPLIT
    for t in range(2):
        qscale = DIFF_DH ** -0.5 * math.log2(math.e) if t == 0 else 1.0
        for r in range(ROPE_ROW_SPLIT):
            slab = slice(r * rows, (r + 1) * rows)
            acc = jnp.dot(xn_ref[slab, :], w_ref[:, t * d:(t + 1) * d], preferred_element_type=F32)
            cos = cos_ref[slab, :] * qscale
            sin = sin_ref[slab, :] * qscale
            for c in range(d // LANES):
                xc = acc[:, c * LANES:(c + 1) * LANES]
                rot = pltpu.roll(xc, LANES // 2, 1)
                o_ref[slab, t * d + c * LANES:t * d + (c + 1) * LANES] = (xc * cos + rot * sin).astype(o_ref.dtype)

    for r in range(ROPE_ROW_SPLIT):
        slab = slice(r * rows, (r + 1) * rows)
        acc = jnp.dot(xn_ref[slab, :], w_ref[:, 2 * d:3 * d], preferred_element_type=F32)
        vt_ref[:, slab] = acc.T.astype(vt_ref.dtype)


def _rope_proj(xn, w, cos, sin, bsz, seq, tm=1024):
    n, d = xn.shape
    tm = min(tm, seq)
    nseq = seq // tm
    return pl.pallas_call(
        _rope_proj_body,
        grid=(n // tm,),
        in_specs=[pl.BlockSpec((tm, d), lambda i: (i, 0)),
                  pl.BlockSpec((d, 3 * d), lambda i: (0, 0)),
                  pl.BlockSpec((tm, LANES), lambda i: (i % nseq, 0)),
                  pl.BlockSpec((tm, LANES), lambda i: (i % nseq, 0))],
        out_specs=[pl.BlockSpec((tm, 2 * d), lambda i: (i, 0)),
                   pl.BlockSpec((d, tm), lambda i: (i // nseq, i % nseq))],
        out_shape=[jax.ShapeDtypeStruct((n, 2 * d), BF16), jax.ShapeDtypeStruct((bsz * d, seq), BF16)],
        compiler_params=_params("parallel"),
        name="rope_proj",
    )(xn, w, cos, sin)


MLP_ROW_SPLIT = 4


def _mix_mlp_body(a_ref, wo_ref, h_ref, g_ref, wu_ref, wd_ref, gn_ref, o_ref, *rest, final_norm):
    xn_ref = rest[-1]
    j = pl.program_id(1)

    rows = o_ref.shape[0] // MLP_ROW_SPLIT
    slabs = [slice(r * rows, (r + 1) * rows) for r in range(MLP_ROW_SPLIT)]

    @pl.when(j == 0)
    def _():
        for slab in slabs:
            h1 = h_ref[slab, :] + jnp.dot(a_ref[slab, :], wo_ref[...], preferred_element_type=F32)
            o_ref[slab, :] = h1
            xn_ref[slab, :] = _rms(h1, g_ref[...]).astype(BF16)

    def up_proj():
        up = jnp.dot(xn_ref[...], wu_ref[...], preferred_element_type=F32)
        return jnp.square(jnp.maximum(up, 0.0)).astype(BF16)

    last = j == pl.num_programs(1) - 1

    @pl.when(jnp.logical_not(last))
    def _():
        o_ref[...] += jnp.dot(up_proj(), wd_ref[...], preferred_element_type=F32)

    @pl.when(last)
    def _():
        up = up_proj()
        for slab in slabs:
            total = o_ref[slab, :] + jnp.dot(up[slab, :], wd_ref[...], preferred_element_type=F32)
            normed = _rms(total, gn_ref[...])
            if final_norm:
                o_ref[slab, :] = normed
            else:
                o_ref[slab, :] = total
                rest[0][slab, :] = normed.astype(BF16)


def _mix_mlp(a, w_out, h, g, w_up, w_down, layer, g_next, final_norm, tm=1024, tf=1024):
    n, d = h.shape
    k = a.shape[1]
    dff = w_up.shape[2]
    tm = min(tm, n)
    row = lambda i, j: (i, 0)
    const = lambda i, j: (0, 0)
    return pl.pallas_call(
        functools.partial(_mix_mlp_body, final_norm=final_norm),
        grid=(n // tm, dff // tf),
        in_specs=[pl.BlockSpec((tm, k), row),
                  pl.BlockSpec((k, d), const),
                  pl.BlockSpec((tm, d), row),
                  pl.BlockSpec((1, d), const),
                  pl.BlockSpec((None, d, tf), lambda i, j: (layer, 0, j)),
                  pl.BlockSpec((None, tf, d), lambda i, j: (layer, j, 0)),
                  pl.BlockSpec((1, d), const)],
        out_specs=pl.BlockSpec((tm, d), row) if final_norm else [pl.BlockSpec((tm, d), row)] * 2,
        out_shape=(jax.ShapeDtypeStruct((n, d), F32) if final_norm else
                   [jax.ShapeDtypeStruct((n, d), F32), jax.ShapeDtypeStruct((n, d), BF16)]),
        scratch_shapes=[pltpu.VMEM((tm, d), BF16)],
        compiler_params=pltpu.CompilerParams(dimension_semantics=("parallel", "arbitrary"),
                                             vmem_limit_bytes=VMEM_LIMIT_MLP),
        name="mix_mlp",
    )(a, w_out, h, g.reshape(1, d), w_up, w_down, g_next.reshape(1, d))


MLSTM_QK = MLSTM_HEADS * MLSTM_DQK
MLSTM_MAIN = 2 * MLSTM_QK + 2 * MLSTM_HEADS * MLSTM_DV


def _mlstm_body(p_ref, gate_ref, bias_ref, ng_ref, out_ref, *state_refs):
    L = MLSTM_CHUNK
    nh = MLSTM_HEADS
    n_batch = p_ref.shape[0]
    n_chain = n_batch * nh
    cn_refs = state_refs[:n_chain]
    m_refs = state_refs[n_chain:]
    ones_cols = jnp.ones((L, LANES), BF16)

    @pl.when(pl.program_id(0) == 0)
    def _():
        for ref in state_refs:
            ref[...] = jnp.zeros_like(ref)

    row = lax.broadcasted_iota(jnp.int32, (L, L), 0)
    col = lax.broadcasted_iota(jnp.int32, (L, L), 1)
    causal = col <= row
    qscale = MLSTM_DQK ** -0.5

    for bi in range(n_batch):
        gates = gate_ref[bi] + bias_ref[...]
        capped = GATE_SOFTCAP * jnp.tanh(gates / GATE_SOFTCAP)
        log_f = -_softplus(-capped)
        log_i = pltpu.roll(capped, nh, 1)
        b = _cumsum_rows(log_f)
        g = log_i - b
        g_max = _cummax_rows(g)
        m_in = m_refs[bi][...]
        m_run = jnp.maximum(m_in, g_max)
        m_t = b + m_run
        scale_all = jnp.exp(m_in - m_run)
        floor_all = jnp.exp(-m_t)
        b_last = b[L - 1:L, :]
        g_max_last = g_max[L - 1:L, :]
        m_run_last = m_run[L - 1:L, :]
        w_all = jnp.exp(g - g_max_last)
        a_old_all = jnp.exp(m_in - m_run_last)
        a_loc_all = jnp.exp(g_max_last - m_run_last)
        m_refs[bi][...] = b_last + m_run_last
        g_t = g.T

        for h in range(nh):
            st = bi * nh + h
            ln = nh + h
            v_off = 2 * MLSTM_QK + h * MLSTM_DV
            o_off = v_off + nh * MLSTM_DV
            q = p_ref[bi, :, h * MLSTM_DQK:(h + 1) * MLSTM_DQK]
            k = p_ref[bi, :, MLSTM_QK + h * MLSTM_DQK:MLSTM_QK + (h + 1) * MLSTM_DQK]
            v_ones = jnp.concatenate([p_ref[bi, :, v_off:v_off + MLSTM_DV], ones_cols], axis=1)
            cn_in = cn_refs[st][...]

            decay = jnp.exp(jnp.where(causal, g_t[ln:ln + 1, :] - m_run[:, ln:ln + 1], -jnp.inf))
            qk = lax.dot_general(q, k, _NT, preferred_element_type=F32) * qscale
            p = decay * qk
            scale = scale_all[:, ln:ln + 1]
            inter = jnp.dot(q, cn_in.astype(BF16), preferred_element_type=F32) * qscale
            both = jnp.dot(p.astype(BF16), v_ones, preferred_element_type=F32) + scale * inter
            den = both[:, MLSTM_DV:MLSTM_DV + 1]
            hs = both[:, :MLSTM_DV] / jnp.maximum(jnp.abs(den), floor_all[:, ln:ln + 1])

            kw = k.astype(F32) * w_all[:, ln:ln + 1]
            cn_loc = jnp.dot(kw.T.astype(BF16), v_ones, preferred_element_type=F32)
            cn_refs[st][...] = a_old_all[:, ln:ln + 1] * cn_in + a_loc_all[:, ln:ln + 1] * cn_loc

            hn = _rms(hs, ng_ref[:, h * MLSTM_DV:(h + 1) * MLSTM_DV])
            o_gate = _sigmoid(p_ref[bi, :, o_off:o_off + MLSTM_DV].astype(F32))
            out_ref[bi, :, h * MLSTM_DV:(h + 1) * MLSTM_DV] = (o_gate * hn).astype(out_ref.dtype)


def _mlstm_core(proj, gates, bias, norm_g, bsz, seq):
    L = MLSTM_CHUNK
    d = MLSTM_HEADS * MLSTM_DV
    n_chain = bsz * MLSTM_HEADS
    chunk = lambda c: (0, c, 0)
    return pl.pallas_call(
        _mlstm_body,
        grid=(seq // L,),
        in_specs=[pl.BlockSpec((bsz, L, MLSTM_MAIN), chunk),
                  pl.BlockSpec((bsz, L, LANES), chunk),
                  pl.BlockSpec((1, LANES), lambda c: (0, 0)),
                  pl.BlockSpec((1, d), lambda c: (0, 0))],
        out_specs=pl.BlockSpec((bsz, L, d), chunk),
        out_shape=jax.ShapeDtypeStruct((bsz, seq, d), BF16),
        scratch_shapes=([pltpu.VMEM((MLSTM_DQK, MLSTM_DV + LANES), F32)] * n_chain
                        + [pltpu.VMEM((1, LANES), F32)] * bsz),
        compiler_params=_params("arbitrary"),
        name="mlstm_core",
    )(proj.reshape(bsz, seq, MLSTM_MAIN), gates.reshape(bsz, seq, LANES), bias, norm_g.reshape(1, d))


def _mlstm_layer(x, prenormed, ln_g, w_in, b_gate, norm_g, bsz, seq):
    w_gate = _pad_cols(w_in[:, MLSTM_MAIN:], LANES).astype(BF16)
    bias = jnp.pad(b_gate.astype(F32), (0, LANES - b_gate.shape[0])).reshape(1, LANES)
    proj, gates = _norm_proj(x, ln_g, w_in.astype(BF16), w_gate, MLSTM_MAIN, prenormed)
    mixed = _mlstm_core(proj, gates, bias, norm_g, bsz, seq)
    return mixed.reshape(bsz * seq, -1)


ATTN_PAIRS_PER_TRIP = 4


def _diff_attn_body(lam_ref, g_ref, q_ref, qn_ref, k_ref, vt_ref, o_ref, sc_a, sc_b, sc_c, cm_a, cm_b, cm_c,
                    m_ref, l_ref, acc_ref, *, tq, lam_init):
    qi = pl.program_id(2)
    q = q_ref[...]
    lane = lax.broadcasted_iota(jnp.int32, q.shape, 1)
    zero = jnp.zeros_like(q)
    is_c1 = (lane & (DIFF_DH // 2)) != 0
    th = tq // 2
    chains = [(c, half) for c in range(2) for half in range(2)]

    def chain_queries(qblk):
        parts = (jnp.where(is_c1, zero, qblk), jnp.where(is_c1, qblk, zero))
        return [parts[c][half * th:(half + 1) * th] for c, half in chains]

    q_ch = chain_queries(q)
    kpos = lax.broadcasted_iota(jnp.int32, (tq, th), 0)
    qpos = lax.broadcasted_iota(jnp.int32, (tq, th), 1)
    causal = [kpos <= qpos + half * th for half in range(2)]
    ones_rows = jnp.ones((16, tq), BF16)

    buf_a, buf_b, buf_c = (sc_a, cm_a), (sc_b, cm_b), (sc_c, cm_c)

    def put_scores(buf, ki, queries=q_ch):
        sc_ref, cm_ref = buf
        k = k_ref[pl.ds(pl.multiple_of(ki * tq, tq), tq), :]
        for x in range(4):
            s = lax.dot_general(k, queries[x], _NT, preferred_element_type=F32)
            sc_ref[x] = s
            cm_ref[x] = _col_reduce(s, jnp.maximum, jnp.max)

    def softmax_pv(ki, buf, masked):
        sc_ref, cm_ref = buf
        vt = vt_ref[:, pl.ds(pl.multiple_of(ki * tq, tq), tq)]
        vt_ones = jnp.concatenate([vt, ones_rows], axis=0)
        for x, (c, half) in enumerate(chains):
            nkeys = th if masked and half == 0 else tq

            def load_scores():
                s = sc_ref[x, :nkeys, :]
                return jnp.where(causal[half][:nkeys], s, -jnp.inf) if masked else s

            m = m_ref[x]
            col_max = _col_reduce(load_scores(), jnp.maximum, jnp.max) if masked else cm_ref[x]
            m_new = jnp.maximum(m, col_max)
            alpha = jnp.exp2(m - m_new)
            p = jnp.exp2(load_scores() - m_new)
            pv = jnp.dot(vt_ones[:, :nkeys], p.astype(BF16), preferred_element_type=F32)
            m_ref[x] = m_new
            l_ref[x] = alpha * l_ref[x] + pv[DIFF_DV:DIFF_DV + 1]
            acc_ref[x] = alpha * acc_ref[x] + pv[:DIFF_DV]

    m_ref[...] = jnp.full(m_ref.shape, -jnp.inf, F32)
    l_ref[...] = jnp.zeros_like(l_ref)
    acc_ref[...] = jnp.zeros_like(acc_ref)

    odd = qi & 1

    @pl.when(qi == 0)
    def _():
        put_scores(buf_a, 0)

    @pl.when(odd == 1)
    def _():
        put_scores(buf_a, 1)
        softmax_pv(0, buf_c, False)

    @pl.when((qi > 0) & (odd == 0))
    def _():
        put_scores(buf_b, 1)
        softmax_pv(0, buf_c, False)
        put_scores(buf_a, 2)
        softmax_pv(1, buf_b, False)

    first = 2 - odd

    def pair(b0):
        put_scores(buf_b, b0 + 1)
        softmax_pv(b0, buf_a, False)
        put_scores(buf_a, b0 + 2)
        softmax_pv(b0 + 1, buf_b, False)

    n_pairs = lax.shift_right_logical(jnp.maximum(qi - first, 0), 1)
    done = 0
    group = 1
    while group < ATTN_PAIRS_PER_TRIP:
        start = first + 2 * done

        @pl.when((n_pairs & group) != 0)
        def _(start=start, group=group):
            for g in range(group):
                pair(start + 2 * g)

        done = done + (n_pairs & group)
        group *= 2

    def trip(j, carry):
        b0 = first + 2 * done + 2 * ATTN_PAIRS_PER_TRIP * j
        for g in range(ATTN_PAIRS_PER_TRIP):
            pair(b0 + 2 * g)
        return carry

    lax.fori_loop(0, lax.shift_right_logical(n_pairs, ATTN_PAIRS_PER_TRIP.bit_length() - 1), trip, 0)

    put_scores(buf_c, 0, chain_queries(qn_ref[...]))
    softmax_pv(qi, buf_a, True)
    a0, a1 = (jnp.concatenate([acc_ref[2 * c] / l_ref[2 * c], acc_ref[2 * c + 1] / l_ref[2 * c + 1]], axis=1)
              for c in range(2))

    lam = lam_ref[...]
    lam_val = (jnp.exp(jnp.sum(lam[0:1] * lam[1:2], axis=1, keepdims=True))
               - jnp.exp(jnp.sum(lam[2:3] * lam[3:4], axis=1, keepdims=True)) + lam_init)
    o = a0 - lam_val * a1
    ms = jnp.mean(o * o, axis=0, keepdims=True)
    y = o * lax.rsqrt(ms + EPS) * g_ref[...] * (1.0 - lam_init)
    o_ref[...] = y.T.astype(o_ref.dtype)


def _diff_attn(qk, v_t, lam, norm_g, bsz, seq, layer_idx, tq=512):
    tq = min(tq, seq)
    nq = seq // tq
    nh = DIFF_HEADS
    lam_init = 0.8 - 0.6 * math.exp(-0.3 * layer_idx)
    return pl.pallas_call(
        functools.partial(_diff_attn_body, tq=tq, lam_init=lam_init),
        grid=(bsz, nh, nq),
        in_specs=[pl.BlockSpec((4, DIFF_DH), lambda b, h, i: (0, 0)),
                  pl.BlockSpec((DIFF_DV, 1), lambda b, h, i: (0, 0)),
                  pl.BlockSpec((tq, DIFF_DV), lambda b, h, i: (b * nq + i, h)),
                  pl.BlockSpec((tq, DIFF_DV), lambda b, h, i: (b * nq + jnp.minimum(i + 1, nq - 1), h)),
                  pl.BlockSpec((seq, DIFF_DV), lambda b, h, i: (b, nh + h)),
                  pl.BlockSpec((DIFF_DV, seq), lambda b, h, i: (b * nh + h, 0))],
        out_specs=pl.BlockSpec((tq, DIFF_DV), lambda b, h, i: (b * nq + i, h)),
        out_shape=jax.ShapeDtypeStruct((bsz * seq, nh * DIFF_DV), BF16),
        scratch_shapes=[pltpu.VMEM((4, tq, tq // 2), F32)] * 3 + [pltpu.VMEM((4, 1, tq // 2), F32)] * 3 + [
                        pltpu.VMEM((4, 1, tq // 2), F32), pltpu.VMEM((4, 1, tq // 2), F32),
                        pltpu.VMEM((4, DIFF_DV, tq // 2), F32)],
        compiler_params=_params("arbitrary", "arbitrary", "arbitrary"),
        name="diff_attn",
    )(lam.astype(F32), norm_g.reshape(DIFF_DV, 1).astype(F32), qk, qk, qk, v_t)


def _rope_tables(seq):
    inv = ROPE_THETA ** (-jnp.arange(0, DIFF_DH, 2, dtype=F32) / DIFF_DH)
    ang = jnp.arange(seq, dtype=F32)[:, None] * inv[None, :]
    cos, sin = jnp.cos(ang), jnp.sin(ang)
    cos_t = jnp.tile(cos, (1, LANES // cos.shape[1]))
    sin_t = jnp.concatenate([-sin, -sin, sin, sin], axis=1)
    return cos_t, sin_t


def _rope_lane_order():
    half = DIFF_DH // 2
    return [c * DIFF_DH + part * half + i for part in range(2) for c in range(2) for i in range(half)]


def _diff_layer(xn, w_in, lam, norm_g, bsz, seq, layer_idx):
    cos, sin = _rope_tables(seq)
    d = w_in.shape[0]
    nqk = 2 * DIFF_HEADS * DIFF_DV
    half = DIFF_DH // 2
    w_lo = w_in.astype(BF16)
    w_qk = w_lo[:, :nqk].reshape(d, nqk // LANES, 2, 2, half).transpose(0, 1, 3, 2, 4).reshape(d, nqk)
    qk, v_t = _rope_proj(xn, jnp.concatenate([w_qk, w_lo[:, nqk:]], axis=1), cos, sin, bsz, seq)
    return _diff_attn(qk, v_t, lam, norm_g, bsz, seq, layer_idx)


SSD_BC = SSD_GROUPS * SSD_STATE
SSD_CONV_DIM = SSD_INNER + 2 * SSD_BC
SSD_MAIN = SSD_INNER + SSD_CONV_DIM


def _expand_heads(a, g, width):
    nr = SSD_HEADS_PER_GROUP
    lane = lax.broadcasted_iota(jnp.int32, (a.shape[0], width), 1)
    out = jnp.broadcast_to(a[:, nr * g + nr - 1:nr * g + nr], (a.shape[0], width))
    for r in range(nr - 2, -1, -1):
        out = jnp.where(lane < SSD_HEADDIM * (r + 1), a[:, nr * g + r:nr * g + r + 1], out)
    return out


def _ssd_body(z_ref, x_ref, bc_ref, dt_ref, cw_ref, cb_ref, dtb_ref, alog_ref, dsk_ref, ng_ref, e3_ref,
              out_ref, ext_ref, s_ref):
    L = SSD_CHUNK
    gw = SSD_GROUP_WIDTH
    ns = SSD_STATE

    @pl.when(pl.program_id(1) == 0)
    def _():
        ext_ref[0:CONV_HALO, :] = jnp.zeros((CONV_HALO, SSD_CONV_DIM), F32)
        s_ref[...] = jnp.zeros_like(s_ref)

    ext_ref[CONV_HALO:CONV_HALO + L, 0:SSD_INNER] = x_ref[...].astype(F32)
    ext_ref[CONV_HALO:CONV_HALO + L, SSD_INNER:SSD_CONV_DIM] = bc_ref[...].astype(F32)

    def conv_silu(lo, width):
        ext = ext_ref[:, lo:lo + width]
        acc = cb_ref[:, lo:lo + width] + cw_ref[SSD_CONV - 1:SSD_CONV, lo:lo + width] * ext[CONV_HALO:]
        for j in range(SSD_CONV - 1):
            shifted = pltpu.roll(ext, SSD_CONV - 1 - j, 0)[CONV_HALO:]
            acc = acc + cw_ref[j:j + 1, lo:lo + width] * shifted
        return _silu(acc)

    dt = _softplus(dt_ref[...] + dtb_ref[...])
    cum = _cumsum_rows(dt * (-jnp.exp(alog_ref[...])))
    cum_t = cum.T
    cum_last = cum[L - 1:L, :]
    exp_cum = jnp.exp(cum)
    decay_to_end = jnp.exp(cum_last - cum)
    chunk_decay = jnp.exp(cum_last)
    row = lax.broadcasted_iota(jnp.int32, (L, L), 0)
    col = lax.broadcasted_iota(jnp.int32, (L, L), 1)
    causal = col <= row
    lane_gw = lax.broadcasted_iota(jnp.int32, (1, gw), 1)
    head_rows = [jnp.where(lane_gw // SSD_HEADDIM == r, 1.0, 0.0).astype(BF16)
                 for r in range(SSD_HEADS_PER_GROUP)]

    def expand(a):
        hi = a.astype(BF16)
        rest = a - hi.astype(F32)
        mid = rest.astype(BF16)
        lo = (rest - mid.astype(F32)).astype(BF16)
        return jnp.dot(jnp.concatenate([hi, mid, lo], axis=1), e3_ref[...], preferred_element_type=F32)

    dt_x = expand(dt)
    exp_cum_x = expand(exp_cum)
    decay_to_end_x = expand(decay_to_end)

    for g in range(SSD_GROUPS):
        cols = slice(g * gw, (g + 1) * gw)
        xs = conv_silu(g * gw, gw)
        bm = conv_silu(SSD_INNER + g * ns, ns).astype(BF16)
        cm = conv_silu(SSD_INNER + SSD_BC + g * ns, ns).astype(BF16)
        xdt = xs * dt_x[:, cols]
        s_in = s_ref[g]
        cb = lax.dot_general(cm, bm, _NT, preferred_element_type=F32)
        y = exp_cum_x[:, cols] * jnp.dot(cm, s_in.astype(BF16), preferred_element_type=F32)
        weights, inputs = [], []
        xdt_lo = xdt.astype(BF16)
        for r in range(SSD_HEADS_PER_GROUP):
            hh = SSD_HEADS_PER_GROUP * g + r
            decay = jnp.exp(jnp.where(causal, cum[:, hh:hh + 1] - cum_t[hh:hh + 1, :], -jnp.inf))
            weights.append((cb * decay).astype(BF16))
            inputs.append(xdt_lo * head_rows[r])
        y = y + jnp.dot(jnp.concatenate(weights, axis=1), jnp.concatenate(inputs, axis=0),
                        preferred_element_type=F32)
        xdte = (xdt * decay_to_end_x[:, cols]).astype(BF16)
        states = jnp.dot(bm.astype(F32).T.astype(BF16), xdte, preferred_element_type=F32)
        s_ref[g] = _expand_heads(chunk_decay, g, gw) * s_in + states
        y = y + dsk_ref[:, g * gw:(g + 1) * gw] * xs
        y = y * _silu(z_ref[:, g * gw:(g + 1) * gw].astype(F32))
        out_ref[:, g * gw:(g + 1) * gw] = _rms(y, ng_ref[:, g * gw:(g + 1) * gw]).astype(out_ref.dtype)

    ext_ref[0:CONV_HALO, :] = ext_ref[L:L + CONV_HALO, :]


def _ssd_core(proj, dt, conv_w, conv_b, dt_bias, a_log, d_skip, norm_g, bsz, seq):
    L = SSD_CHUNK
    nc = seq // L
    rowmap = lambda b, c: (b * nc + c, 0)
    const = lambda b, c: (0, 0)
    head_of_channel = jnp.arange(SSD_INNER, dtype=jnp.int32) // SSD_HEADDIM
    expand1 = (jnp.arange(LANES, dtype=jnp.int32)[:, None] == head_of_channel[None, :]).astype(BF16)
    expand3 = jnp.concatenate([expand1] * 3, axis=0)
    return pl.pallas_call(
        _ssd_body,
        grid=(bsz, nc),
        in_specs=[pl.BlockSpec((L, SSD_INNER), rowmap),
                  pl.BlockSpec((L, SSD_INNER), lambda b, c: (b * nc + c, 1)),
                  pl.BlockSpec((L, 2 * SSD_BC), lambda b, c: (b * nc + c, 2)),
                  pl.BlockSpec((L, LANES), rowmap),
                  pl.BlockSpec((SSD_CONV, SSD_CONV_DIM), const),
                  pl.BlockSpec((1, SSD_CONV_DIM), const),
                  pl.BlockSpec((1, LANES), const),
                  pl.BlockSpec((1, LANES), const),
                  pl.BlockSpec((1, SSD_INNER), const),
                  pl.BlockSpec((1, SSD_INNER), const),
                  pl.BlockSpec((3 * LANES, SSD_INNER), const)],
        out_specs=pl.BlockSpec((L, SSD_INNER), rowmap),
        out_shape=jax.ShapeDtypeStruct((bsz * seq, SSD_INNER), BF16),
        scratch_shapes=[pltpu.VMEM((L + CONV_HALO, SSD_CONV_DIM), F32),
                        pltpu.VMEM((SSD_GROUPS, SSD_STATE, SSD_GROUP_WIDTH), F32)],
        compiler_params=_params("parallel", "arbitrary"),
        name="ssd_core",
    )(proj, proj, proj, dt, conv_w, conv_b, dt_bias, a_log, d_skip, norm_g, expand3)


def _ssd_layer(x, prenormed, ln_g, w_in, conv_w, conv_b, dt_bias, a_log, d_skip, norm_g, bsz, seq):
    w_dt = _pad_cols(w_in[:, SSD_MAIN:], LANES).astype(BF16)
    proj, dt = _norm_proj(x, ln_g, w_in.astype(BF16), w_dt, SSD_MAIN, prenormed)
    pad_heads = lambda a: jnp.pad(a.astype(F32), (0, LANES - SSD_HEADS)).reshape(1, LANES)
    mixed = _ssd_core(proj, dt, conv_w.astype(F32), conv_b.astype(F32).reshape(1, SSD_CONV_DIM),
                      pad_heads(dt_bias), pad_heads(a_log),
                      jnp.repeat(d_skip.astype(F32), SSD_HEADDIM).reshape(1, SSD_INNER),
                      norm_g.astype(F32).reshape(1, SSD_INNER), bsz, seq)
    return mixed


def kernel(x, ln_mix, ln_mlp, w_up, w_down, ln_f, mlstm_w_in, mlstm_b_gate, mlstm_norm, mlstm_w_out, diff_w_in, diff_lam, diff_norm, diff_w_out, ssd_w_in, ssd_conv_w, ssd_conv_b, ssd_dt_bias, ssd_A_log, ssd_D, ssd_norm, ssd_w_out):
    bsz, seq, d = x.shape
    h = x.reshape(bsz * seq, d)
    w_up_lo, w_down_lo = w_up.astype(BF16), w_down.astype(BF16)
    xn = None
    for i in range(DEPTH):
        kind, j = i % N_MIXERS, i // N_MIXERS
        prenormed = xn is not None
        mixer_in = xn if prenormed else h
        if kind == 0:
            mixed = _mlstm_layer(mixer_in, prenormed, ln_mix[i], mlstm_w_in[j], mlstm_b_gate[j],
                                 mlstm_norm[j], bsz, seq)
            w_out = mlstm_w_out[j]
        elif kind == 1:
            assert prenormed, "the attention projection expects a pre-normalised input"
            mixed = _diff_layer(mixer_in, diff_w_in[j], diff_lam[j], diff_norm[j], bsz, seq, i)
            w_out = diff_w_out[j]
        else:
            mixed = _ssd_layer(mixer_in, prenormed, ln_mix[i], ssd_w_in[j], ssd_conv_w[j], ssd_conv_b[j],
                               ssd_dt_bias[j], ssd_A_log[j], ssd_D[j], ssd_norm[j], bsz, seq)
            w_out = ssd_w_out[j]
        last = i == DEPTH - 1
        out = _mix_mlp(mixed, w_out.astype(BF16), h, ln_mlp[i], w_up_lo, w_down_lo, i,
                       ln_f if last else ln_mix[i + 1], final_norm=last)
        h, xn = (out, None) if last else out
    return h.reshape(bsz, seq, d)
```

```python
import functools
import math

import jax
import jax.numpy as jnp
from jax import lax
from jax.experimental import pallas as pl
from jax.experimental.pallas import tpu as pltpu

F32 = jnp.float32
BF16 = jnp.bfloat16

DEPTH = 4
N_MIXERS = 3
EPS = 1e-6

MLSTM_HEADS = 4
MLSTM_DV = 256
MLSTM_DQK = 128
MLSTM_CHUNK = 128
GATE_SOFTCAP = 15.0

DIFF_HEADS = 8
DIFF_DH = 64
DIFF_DV = 128
ROPE_THETA = 10000.0

SSD_INNER = 2048
SSD_HEADDIM = 64
SSD_HEADS = 32
SSD_GROUPS = 8
SSD_STATE = 128
SSD_CONV = 4
SSD_CHUNK = 128
SSD_GROUP_WIDTH = SSD_INNER // SSD_GROUPS
SSD_HEADS_PER_GROUP = SSD_HEADS // SSD_GROUPS

LANES = 128
CONV_HALO = 8
VMEM_LIMIT = 48 * 1024 * 1024
VMEM_LIMIT_MLP = 56 * 1024 * 1024

_NT = (((1,), (1,)), ((), ()))


def _params(*sem):
    return pltpu.CompilerParams(dimension_semantics=sem, vmem_limit_bytes=VMEM_LIMIT)


def _rms(x, g):
    ms = jnp.mean(x * x, axis=-1, keepdims=True)
    return x * lax.rsqrt(ms + EPS) * g


def _cumsum_rows(x):
    n = x.shape[0]
    row = lax.broadcasted_iota(jnp.int32, x.shape, 0)
    s = 1
    while s < n:
        x = x + jnp.where(row >= s, pltpu.roll(x, s, 0), 0.0)
        s *= 2
    return x


def _cummax_rows(x):
    n = x.shape[0]
    row = lax.broadcasted_iota(jnp.int32, x.shape, 0)
    s = 1
    while s < n:
        x = jnp.maximum(x, jnp.where(row >= s, pltpu.roll(x, s, 0), -jnp.inf))
        s *= 2
    return x


COL_REDUCE_SLAB = 64


def _col_reduce(x, pair_op, reduce_fn):
    slab = min(COL_REDUCE_SLAB, x.shape[0])
    acc = x[:slab]
    for r0 in range(slab, x.shape[0], slab):
        acc = pair_op(acc, x[r0:r0 + slab])
    while acc.shape[0] > 8:
        half = acc.shape[0] // 2
        acc = pair_op(acc[:half], acc[half:])
    return reduce_fn(acc, axis=0, keepdims=True)


def _sigmoid(x):
    return 0.5 + 0.5 * jnp.tanh(0.5 * x)


def _silu(x):
    half = 0.5 * x
    return half + half * jnp.tanh(half)


def _softplus(x):
    return jnp.maximum(x, 0.0) + jnp.log1p(jnp.exp(-jnp.abs(x)))


def _pad_cols(w, n):
    return jnp.pad(w, ((0, 0), (0, n - w.shape[1])))


def _norm_proj_body(x_ref, g_ref, w_ref, ws_ref, o_ref, os_ref, *scratch):
    first = pl.program_id(1) == 0
    if scratch:
        xn_ref, = scratch

        @pl.when(first)
        def _():
            xn_ref[...] = _rms(x_ref[...], g_ref[...]).astype(BF16)
    else:
        xn_ref = x_ref

    @pl.when(first)
    def _():
        os_ref[...] = jnp.dot(xn_ref[...], ws_ref[...], preferred_element_type=F32)

    o_ref[...] = jnp.dot(xn_ref[...], w_ref[...], preferred_element_type=F32).astype(o_ref.dtype)


def _norm_proj(x, g, w, w_side, n_main, prenormed, tn=3072, tm=1024):
    n, d = x.shape
    tm = min(tm, n)
    return pl.pallas_call(
        _norm_proj_body,
        grid=(n // tm, n_main // tn),
        in_specs=[pl.BlockSpec((tm, d), lambda i, j: (i, 0)),
                  pl.BlockSpec((1, d), lambda i, j: (0, 0)),
                  pl.BlockSpec((d, tn), lambda i, j: (0, j)),
                  pl.BlockSpec((d, LANES), lambda i, j: (0, 0))],
        out_specs=[pl.BlockSpec((tm, tn), lambda i, j: (i, j)),
                   pl.BlockSpec((tm, LANES), lambda i, j: (i, 0))],
        out_shape=[jax.ShapeDtypeStruct((n, n_main), BF16), jax.ShapeDtypeStruct((n, LANES), F32)],
        scratch_shapes=[] if prenormed else [pltpu.VMEM((tm, d), BF16)],
        compiler_params=pltpu.CompilerParams(
            dimension_semantics=("parallel", "arbitrary"),
            vmem_limit_bytes=VMEM_LIMIT_MLP if tn > 3072 else VMEM_LIMIT),
        name="norm_proj",
    )(x, g.reshape(1, d), w, w_side)


ROPE_ROW_SPLIT = 4


def _rope_proj_body(xn_ref, w_ref, cos_ref, sin_ref, o_ref, vt_ref):
    d = xn_ref.shape[1]
    rows = xn_ref.shape[0] // ROPE_ROW_SPLIT
    for t in range(2):
        qscale = DIFF_DH ** -0.5 * math.log2(math.e) if t == 0 else 1.0
        for r in range(ROPE_ROW_SPLIT):
            slab = slice(r * rows, (r + 1) * rows)
            acc = jnp.dot(xn_ref[slab, :], w_ref[:, t * d:(t + 1) * d], preferred_element_type=F32)
            cos = cos_ref[slab, :] * qscale
            sin = sin_ref[slab, :] * qscale
            for c in range(d // LANES):
                xc = acc[:, c * LANES:(c + 1) * LANES]
                rot = pltpu.roll(xc, LANES // 2, 1)
                o_ref[slab, t * d + c * LANES:t * d + (c + 1) * LANES] = (xc * cos + rot * sin).astype(o_ref.dtype)

    for r in range(ROPE_ROW_SPLIT):
        slab = slice(r * rows, (r + 1) * rows)
        acc = jnp.dot(xn_ref[slab, :], w_ref[:, 2 * d:3 * d], preferred_element_type=F32)
        vt_ref[:, slab] = acc.T.astype(vt_ref.dtype)


def _rope_proj(xn, w, cos, sin, bsz, seq, tm=1024):
    n, d = xn.shape
    tm = min(tm, seq)
    nseq = seq // tm
    return pl.pallas_call(
        _rope_proj_body,
        grid=(n // tm,),
        in_specs=[pl.BlockSpec((tm, d), lambda i: (i, 0)),
                  pl.BlockSpec((d, 3 * d), lambda i: (0, 0)),
                  pl.BlockSpec((tm, LANES), lambda i: (i % nseq, 0)),
                  pl.BlockSpec((tm, LANES), lambda i: (i % nseq, 0))],
        out_specs=[pl.BlockSpec((tm, 2 * d), lambda i: (i, 0)),
                   pl.BlockSpec((d, tm), lambda i: (i // nseq, i % nseq))],
        out_shape=[jax.ShapeDtypeStruct((n, 2 * d), BF16), jax.ShapeDtypeStruct((bsz * d, seq), BF16)],
        compiler_params=_params("parallel"),
        name="rope_proj",
    )(xn, w, cos, sin)


MLP_ROW_SPLIT = 4


def _mix_mlp_body(a_ref, wo_ref, h_ref, g_ref, wu_ref, wd_ref, gn_ref, o_ref, *rest, final_norm):
    xn_ref = rest[-1]
    j = pl.program_id(1)

    rows = o_ref.shape[0] // MLP_ROW_SPLIT
    slabs = [slice(r * rows, (r + 1) * rows) for r in range(MLP_ROW_SPLIT)]

    @pl.when(j == 0)
    def _():
        for slab in slabs:
            h1 = h_ref[slab, :] + jnp.dot(a_ref[slab, :], wo_ref[...], preferred_element_type=F32)
            o_ref[slab, :] = h1
            xn_ref[slab, :] = _rms(h1, g_ref[...]).astype(BF16)

    def up_proj():
        up = jnp.dot(xn_ref[...], wu_ref[...], preferred_element_type=F32)
        return jnp.square(jnp.maximum(up, 0.0)).astype(BF16)

    last = j == pl.num_programs(1) - 1

    @pl.when(jnp.logical_not(last))
    def _():
        o_ref[...] += jnp.dot(up_proj(), wd_ref[...], preferred_element_type=F32)

    @pl.when(last)
    def _():
        up = up_proj()
        for slab in slabs:
            total = o_ref[slab, :] + jnp.dot(up[slab, :], wd_ref[...], preferred_element_type=F32)
            normed = _rms(total, gn_ref[...])
            if final_norm:
                o_ref[slab, :] = normed
            else:
                o_ref[slab, :] = total
                rest[0][slab, :] = normed.astype(BF16)


def _mix_mlp(a, w_out, h, g, w_up, w_down, layer, g_next, final_norm, tm=1024, tf=1024):
    n, d = h.shape
    k = a.shape[1]
    dff = w_up.shape[2]
    tm = min(tm, n)
    row = lambda i, j: (i, 0)
    const = lambda i, j: (0, 0)
    return pl.pallas_call(
        functools.partial(_mix_mlp_body, final_norm=final_norm),
        grid=(n // tm, dff // tf),
        in_specs=[pl.BlockSpec((tm, k), row),
                  pl.BlockSpec((k, d), const),
                  pl.BlockSpec((tm, d), row),
                  pl.BlockSpec((1, d), const),
                  pl.BlockSpec((None, d, tf), lambda i, j: (layer, 0, j)),
                  pl.BlockSpec((None, tf, d), lambda i, j: (layer, j, 0)),
                  pl.BlockSpec((1, d), const)],
        out_specs=pl.BlockSpec((tm, d), row) if final_norm else [pl.BlockSpec((tm, d), row)] * 2,
        out_shape=(jax.ShapeDtypeStruct((n, d), F32) if final_norm else
                   [jax.ShapeDtypeStruct((n, d), F32), jax.ShapeDtypeStruct((n, d), BF16)]),
        scratch_shapes=[pltpu.VMEM((tm, d), BF16)],
        compiler_params=pltpu.CompilerParams(dimension_semantics=("parallel", "arbitrary"),
                                             vmem_limit_bytes=VMEM_LIMIT_MLP),
        name="mix_mlp",
    )(a, w_out, h, g.reshape(1, d), w_up, w_down, g_next.reshape(1, d))


MLSTM_QK = MLSTM_HEADS * MLSTM_DQK
MLSTM_MAIN = 2 * MLSTM_QK + 2 * MLSTM_HEADS * MLSTM_DV


def _mlstm_body(p_ref, gate_ref, bias_ref, ng_ref, out_ref, *state_refs):
    L = MLSTM_CHUNK
    nh = MLSTM_HEADS
    n_batch = p_ref.shape[0]
    n_chain = n_batch * nh
    cn_refs = state_refs[:n_chain]
    m_refs = state_refs[n_chain:]
    ones_cols = jnp.ones((L, LANES), BF16)

    @pl.when(pl.program_id(0) == 0)
    def _():
        for ref in state_refs:
            ref[...] = jnp.zeros_like(ref)

    row = lax.broadcasted_iota(jnp.int32, (L, L), 0)
    col = lax.broadcasted_iota(jnp.int32, (L, L), 1)
    causal = col <= row
    qscale = MLSTM_DQK ** -0.5

    for bi in range(n_batch):
        gates = gate_ref[bi] + bias_ref[...]
        capped = GATE_SOFTCAP * jnp.tanh(gates / GATE_SOFTCAP)
        log_f = -_softplus(-capped)
        log_i = pltpu.roll(capped, nh, 1)
        b = _cumsum_rows(log_f)
        g = log_i - b
        g_max = _cummax_rows(g)
        m_in = m_refs[bi][...]
        m_run = jnp.maximum(m_in, g_max)
        m_t = b + m_run
        scale_all = jnp.exp(m_in - m_run)
        floor_all = jnp.exp(-m_t)
        b_last = b[L - 1:L, :]
        g_max_last = g_max[L - 1:L, :]
        m_run_last = m_run[L - 1:L, :]
        w_all = jnp.exp(g - g_max_last)
        a_old_all = jnp.exp(m_in - m_run_last)
        a_loc_all = jnp.exp(g_max_last - m_run_last)
        m_refs[bi][...] = b_last + m_run_last
        g_t = g.T

        for h in range(nh):
            st = bi * nh + h
            ln = nh + h
            v_off = 2 * MLSTM_QK + h * MLSTM_DV
            o_off = v_off + nh * MLSTM_DV
            q = p_ref[bi, :, h * MLSTM_DQK:(h + 1) * MLSTM_DQK]
            k = p_ref[bi, :, MLSTM_QK + h * MLSTM_DQK:MLSTM_QK + (h + 1) * MLSTM_DQK]
            v_ones = jnp.concatenate([p_ref[bi, :, v_off:v_off + MLSTM_DV], ones_cols], axis=1)
            cn_in = cn_refs[st][...]

            decay = jnp.exp(jnp.where(causal, g_t[ln:ln + 1, :] - m_run[:, ln:ln + 1], -jnp.inf))
            qk = lax.dot_general(q, k, _NT, preferred_element_type=F32) * qscale
            p = decay * qk
            scale = scale_all[:, ln:ln + 1]
            inter = jnp.dot(q, cn_in.astype(BF16), preferred_element_type=F32) * qscale
            both = jnp.dot(p.astype(BF16), v_ones, preferred_element_type=F32) + scale * inter
            den = both[:, MLSTM_DV:MLSTM_DV + 1]
            hs = both[:, :MLSTM_DV] / jnp.maximum(jnp.abs(den), floor_all[:, ln:ln + 1])

            kw = k.astype(F32) * w_all[:, ln:ln + 1]
            cn_loc = jnp.dot(kw.T.astype(BF16), v_ones, preferred_element_type=F32)
            cn_refs[st][...] = a_old_all[:, ln:ln + 1] * cn_in + a_loc_all[:, ln:ln + 1] * cn_loc

            hn = _rms(hs, ng_ref[:, h * MLSTM_DV:(h + 1) * MLSTM_DV])
            o_gate = _sigmoid(p_ref[bi, :, o_off:o_off + MLSTM_DV].astype(F32))
            out_ref[bi, :, h * MLSTM_DV:(h + 1) * MLSTM_DV] = (o_gate * hn).astype(out_ref.dtype)


def _mlstm_core(proj, gates, bias, norm_g, bsz, seq):
    L = MLSTM_CHUNK
    d = MLSTM_HEADS * MLSTM_DV
    n_chain = bsz * MLSTM_HEADS
    chunk = lambda c: (0, c, 0)
    return pl.pallas_call(
        _mlstm_body,
        grid=(seq // L,),
        in_specs=[pl.BlockSpec((bsz, L, MLSTM_MAIN), chunk),
                  pl.BlockSpec((bsz, L, LANES), chunk),
                  pl.BlockSpec((1, LANES), lambda c: (0, 0)),
                  pl.BlockSpec((1, d), lambda c: (0, 0))],
        out_specs=pl.BlockSpec((bsz, L, d), chunk),
        out_shape=jax.ShapeDtypeStruct((bsz, seq, d), BF16),
        scratch_shapes=([pltpu.VMEM((MLSTM_DQK, MLSTM_DV + LANES), F32)] * n_chain
                        + [pltpu.VMEM((1, LANES), F32)] * bsz),
        compiler_params=_params("arbitrary"),
        name="mlstm_core",
    )(proj.reshape(bsz, seq, MLSTM_MAIN), gates.reshape(bsz, seq, LANES), bias, norm_g.reshape(1, d))


def _mlstm_layer(x, prenormed, ln_g, w_in, b_gate, norm_g, bsz, seq):
    w_gate = _pad_cols(w_in[:, MLSTM_MAIN:], LANES).astype(BF16)
    bias = jnp.pad(b_gate.astype(F32), (0, LANES - b_gate.shape[0])).reshape(1, LANES)
    proj, gates = _norm_proj(x, ln_g, w_in.astype(BF16), w_gate, MLSTM_MAIN, prenormed)
    mixed = _mlstm_core(proj, gates, bias, norm_g, bsz, seq)
    return mixed.reshape(bsz * seq, -1)


ATTN_PAIRS_PER_TRIP = 4


def _diff_attn_body(lam_ref, g_ref, q_ref, qn_ref, k_ref, vt_ref, o_ref, sc_a, sc_b, sc_c, cm_a, cm_b, cm_c,
                    m_ref, l_ref, acc_ref, *, tq, lam_init):
    qi = pl.program_id(2)
    q = q_ref[...]
    lane = lax.broadcasted_iota(jnp.int32, q.shape, 1)
    zero = jnp.zeros_like(q)
    is_c1 = (lane & (DIFF_DH // 2)) != 0
    th = tq // 2
    chains = [(c, half) for c in range(2) for half in range(2)]

    def chain_queries(qblk):
        parts = (jnp.where(is_c1, zero, qblk), jnp.where(is_c1, qblk, zero))
        return [parts[c][half * th:(half + 1) * th] for c, half in chains]

    q_ch = chain_queries(q)
    kpos = lax.broadcasted_iota(jnp.int32, (tq, th), 0)
    qpos = lax.broadcasted_iota(jnp.int32, (tq, th), 1)
    causal = [kpos <= qpos + half * th for half in range(2)]
    ones_rows = jnp.ones((16, tq), BF16)

    buf_a, buf_b, buf_c = (sc_a, cm_a), (sc_b, cm_b), (sc_c, cm_c)

    def put_scores(buf, ki, queries=q_ch):
        sc_ref, cm_ref = buf
        k = k_ref[pl.ds(pl.multiple_of(ki * tq, tq), tq), :]
        for x in range(4):
            s = lax.dot_general(k, queries[x], _NT, preferred_element_type=F32)
            sc_ref[x] = s
            cm_ref[x] = _col_reduce(s, jnp.maximum, jnp.max)

    def softmax_pv(ki, buf, masked):
        sc_ref, cm_ref = buf
        vt = vt_ref[:, pl.ds(pl.multiple_of(ki * tq, tq), tq)]
        vt_ones = jnp.concatenate([vt, ones_rows], axis=0)
        for x, (c, half) in enumerate(chains):
            nkeys = th if masked and half == 0 else tq

            def load_scores():
                s = sc_ref[x, :nkeys, :]
                return jnp.where(causal[half][:nkeys], s, -jnp.inf) if masked else s

            m = m_ref[x]
            col_max = _col_reduce(load_scores(), jnp.maximum, jnp.max) if masked else cm_ref[x]
            m_new = jnp.maximum(m, col_max)
            alpha = jnp.exp2(m - m_new)
            p = jnp.exp2(load_scores() - m_new)
            pv = jnp.dot(vt_ones[:, :nkeys], p.astype(BF16), preferred_element_type=F32)
            m_ref[x] = m_new
            l_ref[x] = alpha * l_ref[x] + pv[DIFF_DV:DIFF_DV + 1]
            acc_ref[x] = alpha * acc_ref[x] + pv[:DIFF_DV]

    m_ref[...] = jnp.full(m_ref.shape, -jnp.inf, F32)
    l_ref[...] = jnp.zeros_like(l_ref)
    acc_ref[...] = jnp.zeros_like(acc_ref)

    odd = qi & 1

    @pl.when(qi == 0)
    def _():
        put_scores(buf_a, 0)

    @pl.when(odd == 1)
    def _():
        put_scores(buf_a, 1)
        softmax_pv(0, buf_c, False)

    @pl.when((qi > 0) & (odd == 0))
    def _():
        put_scores(buf_b, 1)
        softmax_pv(0, buf_c, False)
        put_scores(buf_a, 2)
        softmax_pv(1, buf_b, False)

    first = 2 - odd

    def pair(b0):
        put_scores(buf_b, b0 + 1)
        softmax_pv(b0, buf_a, False)
        put_scores(buf_a, b0 + 2)
        softmax_pv(b0 + 1, buf_b, False)

    n_pairs = lax.shift_right_logical(jnp.maximum(qi - first, 0), 1)
    done = 0
    group = 1
    while group < ATTN_PAIRS_PER_TRIP:
        start = first + 2 * done

        @pl.when((n_pairs & group) != 0)
        def _(start=start, group=group):
            for g in range(group):
                pair(start + 2 * g)

        done = done + (n_pairs & group)
        group *= 2

    def trip(j, carry):
        b0 = first + 2 * done + 2 * ATTN_PAIRS_PER_TRIP * j
        for g in range(ATTN_PAIRS_PER_TRIP):
            pair(b0 + 2 * g)
        return carry

    lax.fori_loop(0, lax.shift_right_logical(n_pairs, ATTN_PAIRS_PER_TRIP.bit_length() - 1), trip, 0)

    put_scores(buf_c, 0, chain_queries(qn_ref[...]))
    softmax_pv(qi, buf_a, True)
    a0, a1 = (jnp.concatenate([acc_ref[2 * c] / l_ref[2 * c], acc_ref[2 * c + 1] / l_ref[2 * c + 1]], axis=1)
              for c in range(2))

    lam = lam_ref[...]
    lam_val = (jnp.exp(jnp.sum(lam[0:1] * lam[1:2], axis=1, keepdims=True))
               - jnp.exp(jnp.sum(lam[2:3] * lam[3:4], axis=1, keepdims=True)) + lam_init)
    o = a0 - lam_val * a1
    ms = jnp.mean(o * o, axis=0, keepdims=True)
    y = o * lax.rsqrt(ms + EPS) * g_ref[...] * (1.0 - lam_init)
    o_ref[...] = y.T.astype(o_ref.dtype)


def _diff_attn(qk, v_t, lam, norm_g, bsz, seq, layer_idx, tq=512):
    tq = min(tq, seq)
    nq = seq // tq
    nh = DIFF_HEADS
    lam_init = 0.8 - 0.6 * math.exp(-0.3 * layer_idx)
    return pl.pallas_call(
        functools.partial(_diff_attn_body, tq=tq, lam_init=lam_init),
        grid=(bsz, nh, nq),
        in_specs=[pl.BlockSpec((4, DIFF_DH), lambda b, h, i: (0, 0)),
                  pl.BlockSpec((DIFF_DV, 1), lambda b, h, i: (0, 0)),
                  pl.BlockSpec((tq, DIFF_DV), lambda b, h, i: (b * nq + i, h)),
                  pl.BlockSpec((tq, DIFF_DV), lambda b, h, i: (b * nq + jnp.minimum(i + 1, nq - 1), h)),
                  pl.BlockSpec((seq, DIFF_DV), lambda b, h, i: (b, nh + h)),
                  pl.BlockSpec((DIFF_DV, seq), lambda b, h, i: (b * nh + h, 0))],
        out_specs=pl.BlockSpec((tq, DIFF_DV), lambda b, h, i: (b * nq + i, h)),
        out_shape=jax.ShapeDtypeStruct((bsz * seq, nh * DIFF_DV), BF16),
        scratch_shapes=[pltpu.VMEM((4, tq, tq // 2), F32)] * 3 + [pltpu.VMEM((4, 1, tq // 2), F32)] * 3 + [
                        pltpu.VMEM((4, 1, tq // 2), F32), pltpu.VMEM((4, 1, tq // 2), F32),
                        pltpu.VMEM((4, DIFF_DV, tq // 2), F32)],
        compiler_params=_params("arbitrary", "arbitrary", "arbitrary"),
        name="diff_attn",
    )(lam.astype(F32), norm_g.reshape(DIFF_DV, 1).astype(F32), qk, qk, qk, v_t)


def _rope_tables(seq):
    inv = ROPE_THETA ** (-jnp.arange(0, DIFF_DH, 2, dtype=F32) / DIFF_DH)
    ang = jnp.arange(seq, dtype=F32)[:, None] * inv[None, :]
    cos, sin = jnp.cos(ang), jnp.sin(ang)
    cos_t = jnp.tile(cos, (1, LANES // cos.shape[1]))
    sin_t = jnp.concatenate([-sin, -sin, sin, sin], axis=1)
    return cos_t, sin_t


def _rope_lane_order():
    half = DIFF_DH // 2
    return [c * DIFF_DH + part * half + i for part in range(2) for c in range(2) for i in range(half)]


def _diff_layer(xn, w_in, lam, norm_g, bsz, seq, layer_idx):
    cos, sin = _rope_tables(seq)
    d = w_in.shape[0]
    nqk = 2 * DIFF_HEADS * DIFF_DV
    half = DIFF_DH // 2
    w_lo = w_in.astype(BF16)
    w_qk = w_lo[:, :nqk].reshape(d, nqk // LANES, 2, 2, half).transpose(0, 1, 3, 2, 4).reshape(d, nqk)
    qk, v_t = _rope_proj(xn, jnp.concatenate([w_qk, w_lo[:, nqk:]], axis=1), cos, sin, bsz, seq)
    return _diff_attn(qk, v_t, lam, norm_g, bsz, seq, layer_idx)


SSD_BC = SSD_GROUPS * SSD_STATE
SSD_CONV_DIM = SSD_INNER + 2 * SSD_BC
SSD_MAIN = SSD_INNER + SSD_CONV_DIM


def _expand_heads(a, g, width):
    nr = SSD_HEADS_PER_GROUP
    lane = lax.broadcasted_iota(jnp.int32, (a.shape[0], width), 1)
    out = jnp.broadcast_to(a[:, nr * g + nr - 1:nr * g + nr], (a.shape[0], width))
    for r in range(nr - 2, -1, -1):
        out = jnp.where(lane < SSD_HEADDIM * (r + 1), a[:, nr * g + r:nr * g + r + 1], out)
    return out


def _ssd_body(z_ref, x_ref, bc_ref, dt_ref, cw_ref, cb_ref, dtb_ref, alog_ref, dsk_ref, ng_ref, e3_ref,
              out_ref, ext_ref, s_ref):
    L = SSD_CHUNK
    gw = SSD_GROUP_WIDTH
    ns = SSD_STATE

    @pl.when(pl.program_id(1) == 0)
    def _():
        ext_ref[0:CONV_HALO, :] = jnp.zeros((CONV_HALO, SSD_CONV_DIM), F32)
        s_ref[...] = jnp.zeros_like(s_ref)

    ext_ref[CONV_HALO:CONV_HALO + L, 0:SSD_INNER] = x_ref[...].astype(F32)
    ext_ref[CONV_HALO:CONV_HALO + L, SSD_INNER:SSD_CONV_DIM] = bc_ref[...].astype(F32)

    def conv_silu(lo, width):
        ext = ext_ref[:, lo:lo + width]
        acc = cb_ref[:, lo:lo + width] + cw_ref[SSD_CONV - 1:SSD_CONV, lo:lo + width] * ext[CONV_HALO:]
        for j in range(SSD_CONV - 1):
            shifted = pltpu.roll(ext, SSD_CONV - 1 - j, 0)[CONV_HALO:]
            acc = acc + cw_ref[j:j + 1, lo:lo + width] * shifted
        return _silu(acc)

    dt = _softplus(dt_ref[...] + dtb_ref[...])
    cum = _cumsum_rows(dt * (-jnp.exp(alog_ref[...])))
    cum_t = cum.T
    cum_last = cum[L - 1:L, :]
    exp_cum = jnp.exp(cum)
    decay_to_end = jnp.exp(cum_last - cum)
    chunk_decay = jnp.exp(cum_last)
    row = lax.broadcasted_iota(jnp.int32, (L, L), 0)
    col = lax.broadcasted_iota(jnp.int32, (L, L), 1)
    causal = col <= row
    lane_gw = lax.broadcasted_iota(jnp.int32, (1, gw), 1)
    head_rows = [jnp.where(lane_gw // SSD_HEADDIM == r, 1.0, 0.0).astype(BF16)
                 for r in range(SSD_HEADS_PER_GROUP)]

    def expand(a):
        hi = a.astype(BF16)
        rest = a - hi.astype(F32)
        mid = rest.astype(BF16)
        lo = (rest - mid.astype(F32)).astype(BF16)
        return jnp.dot(jnp.concatenate([hi, mid, lo], axis=1), e3_ref[...], preferred_element_type=F32)

    dt_x = expand(dt)
    exp_cum_x = expand(exp_cum)
    decay_to_end_x = expand(decay_to_end)

    for g in range(SSD_GROUPS):
        cols = slice(g * gw, (g + 1) * gw)
        xs = conv_silu(g * gw, gw)
        bm = conv_silu(SSD_INNER + g * ns, ns).astype(BF16)
        cm = conv_silu(SSD_INNER + SSD_BC + g * ns, ns).astype(BF16)
        xdt = xs * dt_x[:, cols]
        s_in = s_ref[g]
        cb = lax.dot_general(cm, bm, _NT, preferred_element_type=F32)
        y = exp_cum_x[:, cols] * jnp.dot(cm, s_in.astype(BF16), preferred_element_type=F32)
        weights, inputs = [], []
        xdt_lo = xdt.astype(BF16)
        for r in range(SSD_HEADS_PER_GROUP):
            hh = SSD_HEADS_PER_GROUP * g + r
            decay = jnp.exp(jnp.where(causal, cum[:, hh:hh + 1] - cum_t[hh:hh + 1, :], -jnp.inf))
            weights.append((cb * decay).astype(BF16))
            inputs.append(xdt_lo * head_rows[r])
        y = y + jnp.dot(jnp.concatenate(weights, axis=1), jnp.concatenate(inputs, axis=0),
                        preferred_element_type=F32)
        xdte = (xdt * decay_to_end_x[:, cols]).astype(BF16)
        states = jnp.dot(bm.astype(F32).T.astype(BF16), xdte, preferred_element_type=F32)
        s_ref[g] = _expand_heads(chunk_decay, g, gw) * s_in + states
        y = y + dsk_ref[:, g * gw:(g + 1) * gw] * xs
        y = y * _silu(z_ref[:, g * gw:(g + 1) * gw].astype(F32))
        out_ref[:, g * gw:(g + 1) * gw] = _rms(y, ng_ref[:, g * gw:(g + 1) * gw]).astype(out_ref.dtype)

    ext_ref[0:CONV_HALO, :] = ext_ref[L:L + CONV_HALO, :]


def _ssd_core(proj, dt, conv_w, conv_b, dt_bias, a_log, d_skip, norm_g, bsz, seq):
    L = SSD_CHUNK
    nc = seq // L
    rowmap = lambda b, c: (b * nc + c, 0)
    const = lambda b, c: (0, 0)
    head_of_channel = jnp.arange(SSD_INNER, dtype=jnp.int32) // SSD_HEADDIM
    expand1 = (jnp.arange(LANES, dtype=jnp.int32)[:, None] == head_of_channel[None, :]).astype(BF16)
    expand3 = jnp.concatenate([expand1] * 3, axis=0)
    return pl.pallas_call(
        _ssd_body,
        grid=(bsz, nc),
        in_specs=[pl.BlockSpec((L, SSD_INNER), rowmap),
                  pl.BlockSpec((L, SSD_INNER), lambda b, c: (b * nc + c, 1)),
                  pl.BlockSpec((L, 2 * SSD_BC), lambda b, c: (b * nc + c, 2)),
                  pl.BlockSpec((L, LANES), rowmap),
                  pl.BlockSpec((SSD_CONV, SSD_CONV_DIM), const),
                  pl.BlockSpec((1, SSD_CONV_DIM), const),
                  pl.BlockSpec((1, LANES), const),
                  pl.BlockSpec((1, LANES), const),
                  pl.BlockSpec((1, SSD_INNER), const),
                  pl.BlockSpec((1, SSD_INNER), const),
                  pl.BlockSpec((3 * LANES, SSD_INNER), const)],
        out_specs=pl.BlockSpec((L, SSD_INNER), rowmap),
        out_shape=jax.ShapeDtypeStruct((bsz * seq, SSD_INNER), BF16),
        scratch_shapes=[pltpu.VMEM((L + CONV_HALO, SSD_CONV_DIM), F32),
                        pltpu.VMEM((SSD_GROUPS, SSD_STATE, SSD_GROUP_WIDTH), F32)],
        compiler_params=_params("parallel", "arbitrary"),
        name="ssd_core",
    )(proj, proj, proj, dt, conv_w, conv_b, dt_bias, a_log, d_skip, norm_g, expand3)


def _ssd_layer(x, prenormed, ln_g, w_in, conv_w, conv_b, dt_bias, a_log, d_skip, norm_g, bsz, seq):
    w_dt = _pad_cols(w_in[:, SSD_MAIN:], LANES).astype(BF16)
    proj, dt = _norm_proj(x, ln_g, w_in.astype(BF16), w_dt, SSD_MAIN, prenormed, tn=SSD_MAIN)
    pad_heads = lambda a: jnp.pad(a.astype(F32), (0, LANES - SSD_HEADS)).reshape(1, LANES)
    mixed = _ssd_core(proj, dt, conv_w.astype(F32), conv_b.astype(F32).reshape(1, SSD_CONV_DIM),
                      pad_heads(dt_bias), pad_heads(a_log),
                      jnp.repeat(d_skip.astype(F32), SSD_HEADDIM).reshape(1, SSD_INNER),
                      norm_g.astype(F32).reshape(1, SSD_INNER), bsz, seq)
    return mixed


def kernel(x, ln_mix, ln_mlp, w_up, w_down, ln_f, mlstm_w_in, mlstm_b_gate, mlstm_norm, mlstm_w_out, diff_w_in, diff_lam, diff_norm, diff_w_out, ssd_w_in, ssd_conv_w, ssd_conv_b, ssd_dt_bias, ssd_A_log, ssd_D, ssd_norm, ssd_w_out):
    bsz, seq, d = x.shape
    h = x.reshape(bsz * seq, d)
    w_up_lo, w_down_lo = w_up.astype(BF16), w_down.astype(BF16)
    xn = None
    for i in range(DEPTH):
        kind, j = i % N_MIXERS, i // N_MIXERS
        prenormed = xn is not None
        mixer_in = xn if prenormed else h
        if kind == 0:
            mixed = _mlstm_layer(mixer_in, prenormed, ln_mix[i], mlstm_w_in[j], mlstm_b_gate[j],
                                 mlstm_norm[j], bsz, seq)
            w_out = mlstm_w_out[j]
        elif kind == 1:
            assert prenormed, "the attention projection expects a pre-normalised input"
            mixed = _diff_layer(mixer_in, diff_w_in[j], diff_lam[j], diff_norm[j], bsz, seq, i)
            w_out = diff_w_out[j]
        else:
            mixed = _ssd_layer(mixer_in, prenormed, ln_mix[i], ssd_w_in[j], ssd_conv_w[j], ssd_conv_b[j],
                               ssd_dt_bias[j], ssd_A_log[j], ssd_D[j], ssd_norm[j], bsz, seq)
            w_out = ssd_w_out[j]
        last = i == DEPTH - 1
        out = _mix_mlp(mixed, w_out.astype(BF16), h, ln_mlp[i], w_up_lo, w_down_lo, i,
                       ln_f if last else ln_mix[i + 1], final_norm=last)
        h, xn = (out, None) if last else out
    return h.reshape(bsz, seq, d)
```

```python
import functools
import math

import jax
import jax.numpy as jnp
from jax import lax
from jax.experimental import pallas as pl
from jax.experimental.pallas import tpu as pltpu

F32 = jnp.float32
BF16 = jnp.bfloat16

DEPTH = 4
N_MIXERS = 3
EPS = 1e-6

MLSTM_HEADS = 4
MLSTM_DV = 256
MLSTM_DQK = 128
MLSTM_CHUNK = 128
GATE_SOFTCAP = 15.0

DIFF_HEADS = 8
DIFF_DH = 64
DIFF_DV = 128
ROPE_THETA = 10000.0

SSD_INNER = 2048
SSD_HEADDIM = 64
SSD_HEADS = 32
SSD_GROUPS = 8
SSD_STATE = 128
SSD_CONV = 4
SSD_CHUNK = 128
SSD_GROUP_WIDTH = SSD_INNER // SSD_GROUPS
SSD_HEADS_PER_GROUP = SSD_HEADS // SSD_GROUPS

LANES = 128
CONV_HALO = 8
VMEM_LIMIT = 48 * 1024 * 1024
VMEM_LIMIT_MLP = 56 * 1024 * 1024

LOG2E = math.log2(math.e)
_NT = (((1,), (1,)), ((), ()))


def _params(*sem):
    return pltpu.CompilerParams(dimension_semantics=sem, vmem_limit_bytes=VMEM_LIMIT)


def _rms(x, g):
    ms = jnp.mean(x * x, axis=-1, keepdims=True)
    return x * lax.rsqrt(ms + EPS) * g


def _cumsum_rows(x):
    n = x.shape[0]
    row = lax.broadcasted_iota(jnp.int32, x.shape, 0)
    s = 1
    while s < n:
        x = x + jnp.where(row >= s, pltpu.roll(x, s, 0), 0.0)
        s *= 2
    return x


def _cummax_rows(x):
    n = x.shape[0]
    row = lax.broadcasted_iota(jnp.int32, x.shape, 0)
    s = 1
    while s < n:
        x = jnp.maximum(x, jnp.where(row >= s, pltpu.roll(x, s, 0), -jnp.inf))
        s *= 2
    return x


COL_REDUCE_SLAB = 64


def _col_reduce(x, pair_op, reduce_fn):
    slab = min(COL_REDUCE_SLAB, x.shape[0])
    acc = x[:slab]
    for r0 in range(slab, x.shape[0], slab):
        acc = pair_op(acc, x[r0:r0 + slab])
    while acc.shape[0] > 8:
        half = acc.shape[0] // 2
        acc = pair_op(acc[:half], acc[half:])
    return reduce_fn(acc, axis=0, keepdims=True)


def _sigmoid(x):
    return 0.5 + 0.5 * jnp.tanh(0.5 * x)


def _silu(x):
    half = 0.5 * x
    return half + half * jnp.tanh(half)


def _softplus(x):
    return jnp.maximum(x, 0.0) + jnp.log1p(jnp.exp(-jnp.abs(x)))


def _pad_cols(w, n):
    return jnp.pad(w, ((0, 0), (0, n - w.shape[1])))


def _norm_proj_body(x_ref, g_ref, w_ref, ws_ref, o_ref, os_ref, *scratch):
    first = pl.program_id(1) == 0
    if scratch:
        xn_ref, = scratch

        @pl.when(first)
        def _():
            xn_ref[...] = _rms(x_ref[...], g_ref[...]).astype(BF16)
    else:
        xn_ref = x_ref

    @pl.when(first)
    def _():
        os_ref[...] = jnp.dot(xn_ref[...], ws_ref[...], preferred_element_type=F32)

    o_ref[...] = jnp.dot(xn_ref[...], w_ref[...], preferred_element_type=F32).astype(o_ref.dtype)


def _norm_proj(x, g, w, w_side, n_main, prenormed, tn=3072, tm=1024):
    n, d = x.shape
    tm = min(tm, n)
    return pl.pallas_call(
        _norm_proj_body,
        grid=(n // tm, n_main // tn),
        in_specs=[pl.BlockSpec((tm, d), lambda i, j: (i, 0)),
                  pl.BlockSpec((1, d), lambda i, j: (0, 0)),
                  pl.BlockSpec((d, tn), lambda i, j: (0, j)),
                  pl.BlockSpec((d, LANES), lambda i, j: (0, 0))],
        out_specs=[pl.BlockSpec((tm, tn), lambda i, j: (i, j)),
                   pl.BlockSpec((tm, LANES), lambda i, j: (i, 0))],
        out_shape=[jax.ShapeDtypeStruct((n, n_main), BF16), jax.ShapeDtypeStruct((n, LANES), F32)],
        scratch_shapes=[] if prenormed else [pltpu.VMEM((tm, d), BF16)],
        compiler_params=pltpu.CompilerParams(
            dimension_semantics=("parallel", "arbitrary"),
            vmem_limit_bytes=VMEM_LIMIT_MLP if tn > 3072 else VMEM_LIMIT),
        name="norm_proj",
    )(x, g.reshape(1, d), w, w_side)


ROPE_ROW_SPLIT = 4


def _rope_proj_body(xn_ref, w_ref, cos_ref, sin_ref, o_ref, vt_ref):
    d = xn_ref.shape[1]
    rows = xn_ref.shape[0] // ROPE_ROW_SPLIT
    for t in range(2):
        qscale = DIFF_DH ** -0.5 * math.log2(math.e) if t == 0 else 1.0
        for r in range(ROPE_ROW_SPLIT):
            slab = slice(r * rows, (r + 1) * rows)
            acc = jnp.dot(xn_ref[slab, :], w_ref[:, t * d:(t + 1) * d], preferred_element_type=F32)
            cos = cos_ref[slab, :] * qscale
            sin = sin_ref[slab, :] * qscale
            for c in range(d // LANES):
                xc = acc[:, c * LANES:(c + 1) * LANES]
                rot = pltpu.roll(xc, LANES // 2, 1)
                o_ref[slab, t * d + c * LANES:t * d + (c + 1) * LANES] = (xc * cos + rot * sin).astype(o_ref.dtype)

    for r in range(ROPE_ROW_SPLIT):
        slab = slice(r * rows, (r + 1) * rows)
        acc = jnp.dot(xn_ref[slab, :], w_ref[:, 2 * d:3 * d], preferred_element_type=F32)
        vt_ref[:, slab] = acc.T.astype(vt_ref.dtype)


def _rope_proj(xn, w, cos, sin, bsz, seq, tm=1024):
    n, d = xn.shape
    tm = min(tm, seq)
    nseq = seq // tm
    return pl.pallas_call(
        _rope_proj_body,
        grid=(n // tm,),
        in_specs=[pl.BlockSpec((tm, d), lambda i: (i, 0)),
                  pl.BlockSpec((d, 3 * d), lambda i: (0, 0)),
                  pl.BlockSpec((tm, LANES), lambda i: (i % nseq, 0)),
                  pl.BlockSpec((tm, LANES), lambda i: (i % nseq, 0))],
        out_specs=[pl.BlockSpec((tm, 2 * d), lambda i: (i, 0)),
                   pl.BlockSpec((d, tm), lambda i: (i // nseq, i % nseq))],
        out_shape=[jax.ShapeDtypeStruct((n, 2 * d), BF16), jax.ShapeDtypeStruct((bsz * d, seq), BF16)],
        compiler_params=_params("parallel"),
        name="rope_proj",
    )(xn, w, cos, sin)


MLP_ROW_SPLIT = 4


def _mix_mlp_body(a_ref, wo_ref, h_ref, g_ref, wu_ref, wd_ref, gn_ref, o_ref, *rest, final_norm):
    xn_ref = rest[-1]
    j = pl.program_id(1)

    rows = o_ref.shape[0] // MLP_ROW_SPLIT
    slabs = [slice(r * rows, (r + 1) * rows) for r in range(MLP_ROW_SPLIT)]

    @pl.when(j == 0)
    def _():
        for slab in slabs:
            h1 = h_ref[slab, :] + jnp.dot(a_ref[slab, :], wo_ref[...], preferred_element_type=F32)
            o_ref[slab, :] = h1
            xn_ref[slab, :] = _rms(h1, g_ref[...]).astype(BF16)

    def up_proj():
        up = jnp.dot(xn_ref[...], wu_ref[...], preferred_element_type=F32)
        return jnp.square(jnp.maximum(up, 0.0)).astype(BF16)

    last = j == pl.num_programs(1) - 1

    @pl.when(jnp.logical_not(last))
    def _():
        o_ref[...] += jnp.dot(up_proj(), wd_ref[...], preferred_element_type=F32)

    @pl.when(last)
    def _():
        up = up_proj()
        for slab in slabs:
            total = o_ref[slab, :] + jnp.dot(up[slab, :], wd_ref[...], preferred_element_type=F32)
            normed = _rms(total, gn_ref[...])
            if final_norm:
                o_ref[slab, :] = normed
            else:
                o_ref[slab, :] = total
                rest[0][slab, :] = normed.astype(BF16)


def _mix_mlp(a, w_out, h, g, w_up, w_down, layer, g_next, final_norm, tm=1024, tf=1024):
    n, d = h.shape
    k = a.shape[1]
    dff = w_up.shape[2]
    tm = min(tm, n)
    row = lambda i, j: (i, 0)
    const = lambda i, j: (0, 0)
    return pl.pallas_call(
        functools.partial(_mix_mlp_body, final_norm=final_norm),
        grid=(n // tm, dff // tf),
        in_specs=[pl.BlockSpec((tm, k), row),
                  pl.BlockSpec((k, d), const),
                  pl.BlockSpec((tm, d), row),
                  pl.BlockSpec((1, d), const),
                  pl.BlockSpec((None, d, tf), lambda i, j: (layer, 0, j)),
                  pl.BlockSpec((None, tf, d), lambda i, j: (layer, j, 0)),
                  pl.BlockSpec((1, d), const)],
        out_specs=pl.BlockSpec((tm, d), row) if final_norm else [pl.BlockSpec((tm, d), row)] * 2,
        out_shape=(jax.ShapeDtypeStruct((n, d), F32) if final_norm else
                   [jax.ShapeDtypeStruct((n, d), F32), jax.ShapeDtypeStruct((n, d), BF16)]),
        scratch_shapes=[pltpu.VMEM((tm, d), BF16)],
        compiler_params=pltpu.CompilerParams(dimension_semantics=("parallel", "arbitrary"),
                                             vmem_limit_bytes=VMEM_LIMIT_MLP),
        name="mix_mlp",
    )(a, w_out, h, g.reshape(1, d), w_up, w_down, g_next.reshape(1, d))


MLSTM_QK = MLSTM_HEADS * MLSTM_DQK
MLSTM_MAIN = 2 * MLSTM_QK + 2 * MLSTM_HEADS * MLSTM_DV


def _mlstm_body(p_ref, gate_ref, bias_ref, ng_ref, out_ref, *state_refs):
    L = MLSTM_CHUNK
    nh = MLSTM_HEADS
    n_batch = p_ref.shape[0]
    n_chain = n_batch * nh
    cn_refs = state_refs[:n_chain]
    m_refs = state_refs[n_chain:]
    ones_cols = jnp.ones((L, LANES), BF16)

    @pl.when(pl.program_id(0) == 0)
    def _():
        for ref in state_refs:
            ref[...] = jnp.zeros_like(ref)

    row = lax.broadcasted_iota(jnp.int32, (L, L), 0)
    col = lax.broadcasted_iota(jnp.int32, (L, L), 1)
    causal = col <= row
    qscale = MLSTM_DQK ** -0.5

    for bi in range(n_batch):
        gates = gate_ref[bi] + bias_ref[...]
        capped = GATE_SOFTCAP * jnp.tanh(gates / GATE_SOFTCAP)
        log_f = -_softplus(-capped)
        log_i = pltpu.roll(capped, nh, 1)
        b = _cumsum_rows(log_f)
        g = log_i - b
        g_max = _cummax_rows(g)
        m_in = m_refs[bi][...]
        m_run = jnp.maximum(m_in, g_max)
        m_t = b + m_run
        scale_all = jnp.exp(m_in - m_run)
        floor_all = jnp.exp(-m_t)
        b_last = b[L - 1:L, :]
        g_max_last = g_max[L - 1:L, :]
        m_run_last = m_run[L - 1:L, :]
        w_all = jnp.exp(g - g_max_last)
        a_old_all = jnp.exp(m_in - m_run_last)
        a_loc_all = jnp.exp(g_max_last - m_run_last)
        m_refs[bi][...] = b_last + m_run_last
        g2_t = (g * LOG2E).T
        m2_run = m_run * LOG2E

        for h in range(nh):
            st = bi * nh + h
            ln = nh + h
            v_off = 2 * MLSTM_QK + h * MLSTM_DV
            o_off = v_off + nh * MLSTM_DV
            q = p_ref[bi, :, h * MLSTM_DQK:(h + 1) * MLSTM_DQK]
            k = p_ref[bi, :, MLSTM_QK + h * MLSTM_DQK:MLSTM_QK + (h + 1) * MLSTM_DQK]
            v_ones = jnp.concatenate([p_ref[bi, :, v_off:v_off + MLSTM_DV], ones_cols], axis=1)
            cn_in = cn_refs[st][...]

            decay = jnp.exp2(jnp.where(causal, g2_t[ln:ln + 1, :] - m2_run[:, ln:ln + 1], -jnp.inf))
            qk = lax.dot_general(q, k, _NT, preferred_element_type=F32) * qscale
            p = decay * qk
            scale = scale_all[:, ln:ln + 1]
            inter = jnp.dot(q, cn_in.astype(BF16), preferred_element_type=F32) * qscale
            both = jnp.dot(p.astype(BF16), v_ones, preferred_element_type=F32) + scale * inter
            den = both[:, MLSTM_DV:MLSTM_DV + 1]
            hs = both[:, :MLSTM_DV] / jnp.maximum(jnp.abs(den), floor_all[:, ln:ln + 1])

            kw = k.astype(F32) * w_all[:, ln:ln + 1]
            cn_loc = jnp.dot(kw.T.astype(BF16), v_ones, preferred_element_type=F32)
            cn_refs[st][...] = a_old_all[:, ln:ln + 1] * cn_in + a_loc_all[:, ln:ln + 1] * cn_loc

            hn = _rms(hs, ng_ref[:, h * MLSTM_DV:(h + 1) * MLSTM_DV])
            o_gate = _sigmoid(p_ref[bi, :, o_off:o_off + MLSTM_DV].astype(F32))
            out_ref[bi, :, h * MLSTM_DV:(h + 1) * MLSTM_DV] = (o_gate * hn).astype(out_ref.dtype)


def _mlstm_core(proj, gates, bias, norm_g, bsz, seq):
    L = MLSTM_CHUNK
    d = MLSTM_HEADS * MLSTM_DV
    n_chain = bsz * MLSTM_HEADS
    chunk = lambda c: (0, c, 0)
    return pl.pallas_call(
        _mlstm_body,
        grid=(seq // L,),
        in_specs=[pl.BlockSpec((bsz, L, MLSTM_MAIN), chunk),
                  pl.BlockSpec((bsz, L, LANES), chunk),
                  pl.BlockSpec((1, LANES), lambda c: (0, 0)),
                  pl.BlockSpec((1, d), lambda c: (0, 0))],
        out_specs=pl.BlockSpec((bsz, L, d), chunk),
        out_shape=jax.ShapeDtypeStruct((bsz, seq, d), BF16),
        scratch_shapes=([pltpu.VMEM((MLSTM_DQK, MLSTM_DV + LANES), F32)] * n_chain
                        + [pltpu.VMEM((1, LANES), F32)] * bsz),
        compiler_params=_params("arbitrary"),
        name="mlstm_core",
    )(proj.reshape(bsz, seq, MLSTM_MAIN), gates.reshape(bsz, seq, LANES), bias, norm_g.reshape(1, d))


def _mlstm_layer(x, prenormed, ln_g, w_in, b_gate, norm_g, bsz, seq):
    w_gate = _pad_cols(w_in[:, MLSTM_MAIN:], LANES).astype(BF16)
    bias = jnp.pad(b_gate.astype(F32), (0, LANES - b_gate.shape[0])).reshape(1, LANES)
    proj, gates = _norm_proj(x, ln_g, w_in.astype(BF16), w_gate, MLSTM_MAIN, prenormed)
    mixed = _mlstm_core(proj, gates, bias, norm_g, bsz, seq)
    return mixed.reshape(bsz * seq, -1)


ATTN_PAIRS_PER_TRIP = 4


def _diff_attn_body(lam_ref, g_ref, q_ref, qn_ref, k_ref, vt_ref, o_ref, sc_a, sc_b, sc_c, cm_a, cm_b, cm_c,
                    m_ref, l_ref, acc_ref, *, tq, lam_init):
    qi = pl.program_id(2)
    q = q_ref[...]
    lane = lax.broadcasted_iota(jnp.int32, q.shape, 1)
    zero = jnp.zeros_like(q)
    is_c1 = (lane & (DIFF_DH // 2)) != 0
    th = tq // 2
    chains = [(c, half) for c in range(2) for half in range(2)]

    def chain_queries(qblk):
        parts = (jnp.where(is_c1, zero, qblk), jnp.where(is_c1, qblk, zero))
        return [parts[c][half * th:(half + 1) * th] for c, half in chains]

    q_ch = chain_queries(q)
    kpos = lax.broadcasted_iota(jnp.int32, (tq, th), 0)
    qpos = lax.broadcasted_iota(jnp.int32, (tq, th), 1)
    causal = [kpos <= qpos + half * th for half in range(2)]
    ones_rows = jnp.ones((16, tq), BF16)

    buf_a, buf_b, buf_c = (sc_a, cm_a), (sc_b, cm_b), (sc_c, cm_c)

    def put_scores(buf, ki, queries=q_ch):
        sc_ref, cm_ref = buf
        k = k_ref[pl.ds(pl.multiple_of(ki * tq, tq), tq), :]
        for x in range(4):
            s = lax.dot_general(k, queries[x], _NT, preferred_element_type=F32)
            sc_ref[x] = s
            cm_ref[x] = _col_reduce(s, jnp.maximum, jnp.max)

    def softmax_pv(ki, buf, masked):
        sc_ref, cm_ref = buf
        vt = vt_ref[:, pl.ds(pl.multiple_of(ki * tq, tq), tq)]
        vt_ones = jnp.concatenate([vt, ones_rows], axis=0)
        for x, (c, half) in enumerate(chains):
            nkeys = th if masked and half == 0 else tq

            def load_scores():
                s = sc_ref[x, :nkeys, :]
                return jnp.where(causal[half][:nkeys], s, -jnp.inf) if masked else s

            m = m_ref[x]
            col_max = _col_reduce(load_scores(), jnp.maximum, jnp.max) if masked else cm_ref[x]
            m_new = jnp.maximum(m, col_max)
            alpha = jnp.exp2(m - m_new)
            p = jnp.exp2(load_scores() - m_new)
            pv = jnp.dot(vt_ones[:, :nkeys], p.astype(BF16), preferred_element_type=F32)
            m_ref[x] = m_new
            l_ref[x] = alpha * l_ref[x] + pv[DIFF_DV:DIFF_DV + 1]
            acc_ref[x] = alpha * acc_ref[x] + pv[:DIFF_DV]

    m_ref[...] = jnp.full(m_ref.shape, -jnp.inf, F32)
    l_ref[...] = jnp.zeros_like(l_ref)
    acc_ref[...] = jnp.zeros_like(acc_ref)

    odd = qi & 1

    @pl.when(qi == 0)
    def _():
        put_scores(buf_a, 0)

    @pl.when(odd == 1)
    def _():
        put_scores(buf_a, 1)
        softmax_pv(0, buf_c, False)

    @pl.when((qi > 0) & (odd == 0))
    def _():
        put_scores(buf_b, 1)
        softmax_pv(0, buf_c, False)
        put_scores(buf_a, 2)
        softmax_pv(1, buf_b, False)

    first = 2 - odd

    def pair(b0):
        put_scores(buf_b, b0 + 1)
        softmax_pv(b0, buf_a, False)
        put_scores(buf_a, b0 + 2)
        softmax_pv(b0 + 1, buf_b, False)

    n_pairs = lax.shift_right_logical(jnp.maximum(qi - first, 0), 1)
    done = 0
    group = 1
    while group < ATTN_PAIRS_PER_TRIP:
        start = first + 2 * done

        @pl.when((n_pairs & group) != 0)
        def _(start=start, group=group):
            for g in range(group):
                pair(start + 2 * g)

        done = done + (n_pairs & group)
        group *= 2

    def trip(j, carry):
        b0 = first + 2 * done + 2 * ATTN_PAIRS_PER_TRIP * j
        for g in range(ATTN_PAIRS_PER_TRIP):
            pair(b0 + 2 * g)
        return carry

    lax.fori_loop(0, lax.shift_right_logical(n_pairs, ATTN_PAIRS_PER_TRIP.bit_length() - 1), trip, 0)

    put_scores(buf_c, 0, chain_queries(qn_ref[...]))
    softmax_pv(qi, buf_a, True)
    a0, a1 = (jnp.concatenate([acc_ref[2 * c] / l_ref[2 * c], acc_ref[2 * c + 1] / l_ref[2 * c + 1]], axis=1)
              for c in range(2))

    lam = lam_ref[...]
    lam_val = (jnp.exp(jnp.sum(lam[0:1] * lam[1:2], axis=1, keepdims=True))
               - jnp.exp(jnp.sum(lam[2:3] * lam[3:4], axis=1, keepdims=True)) + lam_init)
    o = a0 - lam_val * a1
    ms = jnp.mean(o * o, axis=0, keepdims=True)
    y = o * lax.rsqrt(ms + EPS) * g_ref[...] * (1.0 - lam_init)
    o_ref[...] = y.T.astype(o_ref.dtype)


def _diff_attn(qk, v_t, lam, norm_g, bsz, seq, layer_idx, tq=512):
    tq = min(tq, seq)
    nq = seq // tq
    nh = DIFF_HEADS
    lam_init = 0.8 - 0.6 * math.exp(-0.3 * layer_idx)
    return pl.pallas_call(
        functools.partial(_diff_attn_body, tq=tq, lam_init=lam_init),
        grid=(bsz, nh, nq),
        in_specs=[pl.BlockSpec((4, DIFF_DH), lambda b, h, i: (0, 0)),
                  pl.BlockSpec((DIFF_DV, 1), lambda b, h, i: (0, 0)),
                  pl.BlockSpec((tq, DIFF_DV), lambda b, h, i: (b * nq + i, h)),
                  pl.BlockSpec((tq, DIFF_DV), lambda b, h, i: (b * nq + jnp.minimum(i + 1, nq - 1), h)),
                  pl.BlockSpec((seq, DIFF_DV), lambda b, h, i: (b, nh + h)),
                  pl.BlockSpec((DIFF_DV, seq), lambda b, h, i: (b * nh + h, 0))],
        out_specs=pl.BlockSpec((tq, DIFF_DV), lambda b, h, i: (b * nq + i, h)),
        out_shape=jax.ShapeDtypeStruct((bsz * seq, nh * DIFF_DV), BF16),
        scratch_shapes=[pltpu.VMEM((4, tq, tq // 2), F32)] * 3 + [pltpu.VMEM((4, 1, tq // 2), F32)] * 3 + [
                        pltpu.VMEM((4, 1, tq // 2), F32), pltpu.VMEM((4, 1, tq // 2), F32),
                        pltpu.VMEM((4, DIFF_DV, tq // 2), F32)],
        compiler_params=_params("arbitrary", "arbitrary", "arbitrary"),
        name="diff_attn",
    )(lam.astype(F32), norm_g.reshape(DIFF_DV, 1).astype(F32), qk, qk, qk, v_t)


def _rope_tables(seq):
    inv = ROPE_THETA ** (-jnp.arange(0, DIFF_DH, 2, dtype=F32) / DIFF_DH)
    ang = jnp.arange(seq, dtype=F32)[:, None] * inv[None, :]
    cos, sin = jnp.cos(ang), jnp.sin(ang)
    cos_t = jnp.tile(cos, (1, LANES // cos.shape[1]))
    sin_t = jnp.concatenate([-sin, -sin, sin, sin], axis=1)
    return cos_t, sin_t


def _rope_lane_order():
    half = DIFF_DH // 2
    return [c * DIFF_DH + part * half + i for part in range(2) for c in range(2) for i in range(half)]


def _diff_layer(xn, w_in, lam, norm_g, bsz, seq, layer_idx):
    cos, sin = _rope_tables(seq)
    d = w_in.shape[0]
    nqk = 2 * DIFF_HEADS * DIFF_DV
    half = DIFF_DH // 2
    w_lo = w_in.astype(BF16)
    w_qk = w_lo[:, :nqk].reshape(d, nqk // LANES, 2, 2, half).transpose(0, 1, 3, 2, 4).reshape(d, nqk)
    qk, v_t = _rope_proj(xn, jnp.concatenate([w_qk, w_lo[:, nqk:]], axis=1), cos, sin, bsz, seq)
    return _diff_attn(qk, v_t, lam, norm_g, bsz, seq, layer_idx)


SSD_BC = SSD_GROUPS * SSD_STATE
SSD_CONV_DIM = SSD_INNER + 2 * SSD_BC
SSD_MAIN = SSD_INNER + SSD_CONV_DIM


def _expand_heads(a, g, width):
    nr = SSD_HEADS_PER_GROUP
    lane = lax.broadcasted_iota(jnp.int32, (a.shape[0], width), 1)
    out = jnp.broadcast_to(a[:, nr * g + nr - 1:nr * g + nr], (a.shape[0], width))
    for r in range(nr - 2, -1, -1):
        out = jnp.where(lane < SSD_HEADDIM * (r + 1), a[:, nr * g + r:nr * g + r + 1], out)
    return out


def _ssd_body(z_ref, x_ref, bc_ref, dt_ref, cw_ref, cb_ref, dtb_ref, alog_ref, dsk_ref, ng_ref, e3_ref,
              out_ref, ext_ref, s_ref):
    L = SSD_CHUNK
    gw = SSD_GROUP_WIDTH
    ns = SSD_STATE

    @pl.when(pl.program_id(1) == 0)
    def _():
        ext_ref[0:CONV_HALO, :] = jnp.zeros((CONV_HALO, SSD_CONV_DIM), F32)
        s_ref[...] = jnp.zeros_like(s_ref)

    ext_ref[CONV_HALO:CONV_HALO + L, 0:SSD_INNER] = x_ref[...].astype(F32)
    ext_ref[CONV_HALO:CONV_HALO + L, SSD_INNER:SSD_CONV_DIM] = bc_ref[...].astype(F32)

    def conv_silu(lo, width):
        ext = ext_ref[:, lo:lo + width]
        acc = cb_ref[:, lo:lo + width] + cw_ref[SSD_CONV - 1:SSD_CONV, lo:lo + width] * ext[CONV_HALO:]
        for j in range(SSD_CONV - 1):
            shifted = pltpu.roll(ext, SSD_CONV - 1 - j, 0)[CONV_HALO:]
            acc = acc + cw_ref[j:j + 1, lo:lo + width] * shifted
        return _silu(acc)

    dt = _softplus(dt_ref[...] + dtb_ref[...])
    cum = _cumsum_rows(dt * (-jnp.exp(alog_ref[...])))
    cum2 = cum * LOG2E
    cum2_t = cum2.T
    cum_last = cum[L - 1:L, :]
    exp_cum = jnp.exp(cum)
    decay_to_end = jnp.exp(cum_last - cum)
    chunk_decay = jnp.exp(cum_last)
    row = lax.broadcasted_iota(jnp.int32, (L, L), 0)
    col = lax.broadcasted_iota(jnp.int32, (L, L), 1)
    causal = col <= row
    lane_gw = lax.broadcasted_iota(jnp.int32, (1, gw), 1)
    head_rows = [jnp.where(lane_gw // SSD_HEADDIM == r, 1.0, 0.0).astype(BF16)
                 for r in range(SSD_HEADS_PER_GROUP)]

    def expand(a):
        hi = a.astype(BF16)
        rest = a - hi.astype(F32)
        mid = rest.astype(BF16)
        lo = (rest - mid.astype(F32)).astype(BF16)
        return jnp.dot(jnp.concatenate([hi, mid, lo], axis=1), e3_ref[...], preferred_element_type=F32)

    dt_x = expand(dt)
    exp_cum_x = expand(exp_cum)
    decay_to_end_x = expand(decay_to_end)

    for g in range(SSD_GROUPS):
        cols = slice(g * gw, (g + 1) * gw)
        xs = conv_silu(g * gw, gw)
        bm = conv_silu(SSD_INNER + g * ns, ns).astype(BF16)
        cm = conv_silu(SSD_INNER + SSD_BC + g * ns, ns).astype(BF16)
        xdt = xs * dt_x[:, cols]
        s_in = s_ref[g]
        cb = lax.dot_general(cm, bm, _NT, preferred_element_type=F32)
        y = exp_cum_x[:, cols] * jnp.dot(cm, s_in.astype(BF16), preferred_element_type=F32)
        weights, inputs = [], []
        xdt_lo = xdt.astype(BF16)
        for r in range(SSD_HEADS_PER_GROUP):
            hh = SSD_HEADS_PER_GROUP * g + r
            decay = jnp.exp2(jnp.where(causal, cum2[:, hh:hh + 1] - cum2_t[hh:hh + 1, :], -jnp.inf))
            weights.append((cb * decay).astype(BF16))
            inputs.append(xdt_lo * head_rows[r])
        y = y + jnp.dot(jnp.concatenate(weights, axis=1), jnp.concatenate(inputs, axis=0),
                        preferred_element_type=F32)
        xdte = (xdt * decay_to_end_x[:, cols]).astype(BF16)
        states = jnp.dot(bm.astype(F32).T.astype(BF16), xdte, preferred_element_type=F32)
        s_ref[g] = _expand_heads(chunk_decay, g, gw) * s_in + states
        y = y + dsk_ref[:, g * gw:(g + 1) * gw] * xs
        y = y * _silu(z_ref[:, g * gw:(g + 1) * gw].astype(F32))
        out_ref[:, g * gw:(g + 1) * gw] = _rms(y, ng_ref[:, g * gw:(g + 1) * gw]).astype(out_ref.dtype)

    ext_ref[0:CONV_HALO, :] = ext_ref[L:L + CONV_HALO, :]


def _ssd_core(proj, dt, conv_w, conv_b, dt_bias, a_log, d_skip, norm_g, bsz, seq):
    L = SSD_CHUNK
    nc = seq // L
    rowmap = lambda b, c: (b * nc + c, 0)
    const = lambda b, c: (0, 0)
    head_of_channel = jnp.arange(SSD_INNER, dtype=jnp.int32) // SSD_HEADDIM
    expand1 = (jnp.arange(LANES, dtype=jnp.int32)[:, None] == head_of_channel[None, :]).astype(BF16)
    expand3 = jnp.concatenate([expand1] * 3, axis=0)
    return pl.pallas_call(
        _ssd_body,
        grid=(bsz, nc),
        in_specs=[pl.BlockSpec((L, SSD_INNER), rowmap),
                  pl.BlockSpec((L, SSD_INNER), lambda b, c: (b * nc + c, 1)),
                  pl.BlockSpec((L, 2 * SSD_BC), lambda b, c: (b * nc + c, 2)),
                  pl.BlockSpec((L, LANES), rowmap),
                  pl.BlockSpec((SSD_CONV, SSD_CONV_DIM), const),
                  pl.BlockSpec((1, SSD_CONV_DIM), const),
                  pl.BlockSpec((1, LANES), const),
                  pl.BlockSpec((1, LANES), const),
                  pl.BlockSpec((1, SSD_INNER), const),
                  pl.BlockSpec((1, SSD_INNER), const),
                  pl.BlockSpec((3 * LANES, SSD_INNER), const)],
        out_specs=pl.BlockSpec((L, SSD_INNER), rowmap),
        out_shape=jax.ShapeDtypeStruct((bsz * seq, SSD_INNER), BF16),
        scratch_shapes=[pltpu.VMEM((L + CONV_HALO, SSD_CONV_DIM), F32),
                        pltpu.VMEM((SSD_GROUPS, SSD_STATE, SSD_GROUP_WIDTH), F32)],
        compiler_params=_params("parallel", "arbitrary"),
        name="ssd_core",
    )(proj, proj, proj, dt, conv_w, conv_b, dt_bias, a_log, d_skip, norm_g, expand3)


def _ssd_layer(x, prenormed, ln_g, w_in, conv_w, conv_b, dt_bias, a_log, d_skip, norm_g, bsz, seq):
    w_dt = _pad_cols(w_in[:, SSD_MAIN:], LANES).astype(BF16)
    proj, dt = _norm_proj(x, ln_g, w_in.astype(BF16), w_dt, SSD_MAIN, prenormed, tn=SSD_MAIN)
    pad_heads = lambda a: jnp.pad(a.astype(F32), (0, LANES - SSD_HEADS)).reshape(1, LANES)
    mixed = _ssd_core(proj, dt, conv_w.astype(F32), conv_b.astype(F32).reshape(1, SSD_CONV_DIM),
                      pad_heads(dt_bias), pad_heads(a_log),
                      jnp.repeat(d_skip.astype(F32), SSD_HEADDIM).reshape(1, SSD_INNER),
                      norm_g.astype(F32).reshape(1, SSD_INNER), bsz, seq)
    return mixed


def kernel(x, ln_mix, ln_mlp, w_up, w_down, ln_f, mlstm_w_in, mlstm_b_gate, mlstm_norm, mlstm_w_out, diff_w_in, diff_lam, diff_norm, diff_w_out, ssd_w_in, ssd_conv_w, ssd_conv_b, ssd_dt_bias, ssd_A_log, ssd_D, ssd_norm, ssd_w_out):
    bsz, seq, d = x.shape
    h = x.reshape(bsz * seq, d)
    w_up_lo, w_down_lo = w_up.astype(BF16), w_down.astype(BF16)
    xn = None
    for i in range(DEPTH):
        kind, j = i % N_MIXERS, i // N_MIXERS
        prenormed = xn is not None
        mixer_in = xn if prenormed else h
        if kind == 0:
            mixed = _mlstm_layer(mixer_in, prenormed, ln_mix[i], mlstm_w_in[j], mlstm_b_gate[j],
                                 mlstm_norm[j], bsz, seq)
            w_out = mlstm_w_out[j]
        elif kind == 1:
            assert prenormed, "the attention projection expects a pre-normalised input"
            mixed = _diff_layer(mixer_in, diff_w_in[j], diff_lam[j], diff_norm[j], bsz, seq, i)
            w_out = diff_w_out[j]
        else:
            mixed = _ssd_layer(mixer_in, prenormed, ln_mix[i], ssd_w_in[j], ssd_conv_w[j], ssd_conv_b[j],
                               ssd_dt_bias[j], ssd_A_log[j], ssd_D[j], ssd_norm[j], bsz, seq)
            w_out = ssd_w_out[j]
        last = i == DEPTH - 1
        out = _mix_mlp(mixed, w_out.astype(BF16), h, ln_mlp[i], w_up_lo, w_down_lo, i,
                       ln_f if last else ln_mix[i + 1], final_norm=last)
        h, xn = (out, None) if last else out
    return h.reshape(bsz, seq, d)
```
